```python
import math
import jax, jax.numpy as jnp
from jax import lax
import numpy as np

D_MODEL = 1024
BATCH = 8
SEQ = 2048
DEPTH = 4
DEC_BATCH = 128
DEC_SEQ = 4
PAST_LEN = 16384
PAGE_SIZE = 128

N_EVEN = (DEPTH + 1) // 2
N_ODD = DEPTH // 2
CONV_W = 4
EPS = 1e-6
D_RNN = D_MODEL
LRU_BLOCKS = 8
LRU_BW = D_RNN // LRU_BLOCKS
LRU_C = 8.0
D_INNER = 2 * D_MODEL
SSD_HEADDIM = 64
SSD_HEADS = D_INNER // SSD_HEADDIM
SSD_GROUPS = 4
SSD_HPG = SSD_HEADS // SSD_GROUPS
SSD_STATE = 128
SSD_CONV_DIM = D_INNER + 2 * SSD_GROUPS * SSD_STATE
SSD_IN_DIM = D_INNER + SSD_CONV_DIM + SSD_HEADS
SSD_CHUNK = 128
N_MEM = 256
MEM_HEADS = 4
MEM_HD = D_MODEL // MEM_HEADS
D_FF = 2816
N_EXPERTS = 8
TOP_K = 2
D_FF_EXPERT = 3584

kernel_name = 'hybrid_rglru_ssd_memxattn_decode_step'


def rms_norm(x, g):
    xf = x.astype(jnp.float32)
    y = xf * lax.rsqrt(jnp.mean(xf * xf, axis=-1, keepdims=True) + EPS)
    return (y * g.astype(jnp.float32)).astype(x.dtype)


def causal_conv(x, prev, w, b):
    T = x.shape[1]
    xp = jnp.concatenate([prev.astype(x.dtype), x], axis=1)
    y = xp[:, 0:T] * w[0]
    for k in range(1, CONV_W):
        y = y + xp[:, k:k + T] * w[k]
    return y + b, xp[:, -(CONV_W - 1):]


def rglru_block(h, conv_prev, h_prev, w_in, conv_w, conv_b, w_a, b_a, w_x, b_x, lam, w_out):
    B, T, _ = h.shape
    proj = h @ w_in
    gate_branch, xb = jnp.split(proj, 2, axis=-1)
    xc, conv_new = causal_conv(xb, conv_prev, conv_w, conv_b)
    xblk = xc.reshape(B, T, LRU_BLOCKS, LRU_BW)
    r = jax.nn.sigmoid(jnp.einsum('btki,kij->btkj', xblk, w_a).reshape(B, T, D_RNN) + b_a)
    ig = jax.nn.sigmoid(jnp.einsum('btki,kij->btkj', xblk, w_x).reshape(B, T, D_RNN) + b_x)
    log_a = (-LRU_C * r.astype(jnp.float32)) * jax.nn.softplus(-lam.astype(jnp.float32))
    a = jnp.exp(log_a)
    u = jnp.sqrt(-jnp.expm1(2.0 * log_a)) * (ig * xc).astype(jnp.float32)
    u = u.at[:, 0].add(a[:, 0] * h_prev.astype(jnp.float32))

    def combine(left, right):
        a1, b1 = left
        a2, b2 = right
        return a1 * a2, a2 * b1 + b2

    _, hs = lax.associative_scan(combine, (a, u), axis=1)
    y = hs.astype(h.dtype) * jax.nn.gelu(gate_branch)
    return y @ w_out, hs[:, -1].astype(h_prev.dtype), conv_new


def segsum(a):
    l = a.shape[-1]
    cs = jnp.cumsum(a, axis=-1)
    diff = cs[..., :, None] - cs[..., None, :]
    mask = jnp.tril(jnp.ones((l, l), dtype=bool))
    return jnp.where(mask, diff, -jnp.inf)


def ssd_chunked(X, Adt, Bm, Cm, h0):
    b, T = X.shape[:2]
    Q = min(SSD_CHUNK, T)
    nc = -(-T // Q)
    pad = nc * Q - T
    if pad:
        padT = lambda t: jnp.pad(t, [(0, 0), (0, pad)] + [(0, 0)] * (t.ndim - 2))
        X, Adt, Bm, Cm = padT(X), padT(Adt), padT(Bm), padT(Cm)
    g, e, p = X.shape[2:]
    n = Bm.shape[-1]
    X = X.reshape(b, nc, Q, g, e, p)
    Bm = Bm.reshape(b, nc, Q, g, n)
    Cm = Cm.reshape(b, nc, Q, g, n)
    A = Adt.reshape(b, nc, Q, g, e).transpose(0, 3, 4, 1, 2)
    A_cs = jnp.cumsum(A, axis=-1)
    L = jnp.exp(segsum(A))
    CB = jnp.einsum('bclgn,bcsgn->bcgls', Cm, Bm)
    y_diag = jnp.einsum('bcgls,bgecls,bcsgep->bclgep', CB, L, X)
    decay_states = jnp.exp(A_cs[..., -1:] - A_cs)
    states = jnp.einsum('bclgn,bgecl,bclgep->bcgepn', Bm, decay_states, X)
    states = jnp.concatenate([h0[:, None], states], axis=1)
    chunk_decay = jnp.exp(segsum(jnp.pad(A_cs[..., -1], ((0, 0), (0, 0), (0, 0), (1, 0)))))
    new_states = jnp.einsum('bgezc,bcgepn->bzgepn', chunk_decay, states)
    states, final = new_states[:, :-1], new_states[:, -1]
    y_off = jnp.einsum('bclgn,bcgepn,bgecl->bclgep', Cm, states, jnp.exp(A_cs))
    y = (y_diag + y_off).reshape(b, nc * Q, g, e, p)[:, :T]
    return y, final


def ssd_block(h, conv_prev, s_prev, w_in, conv_w, conv_b, dt_bias, a_log, d_skip, norm_g, w_out):
    B, T, _ = h.shape
    proj = h @ w_in
    z = proj[..., :D_INNER]
    xbc = proj[..., D_INNER:D_INNER + SSD_CONV_DIM]
    dt = proj[..., D_INNER + SSD_CONV_DIM:]
    xbc, conv_new = causal_conv(xbc, conv_prev, conv_w, conv_b)
    xbc = jax.nn.silu(xbc).astype(jnp.float32)
    gn = SSD_GROUPS * SSD_STATE
    xs = xbc[..., :D_INNER].reshape(B, T, SSD_GROUPS, SSD_HPG, SSD_HEADDIM)
    Bm = xbc[..., D_INNER:D_INNER + gn].reshape(B, T, SSD_GROUPS, SSD_STATE)
    Cm = xbc[..., D_INNER + gn:].reshape(B, T, SSD_GROUPS, SSD_STATE)
    dt = jax.nn.softplus(dt.astype(jnp.float32) + dt_bias.astype(jnp.float32)).reshape(B, T, SSD_GROUPS, SSD_HPG)
    A = -jnp.exp(a_log.astype(jnp.float32)).reshape(SSD_GROUPS, SSD_HPG)
    h0 = s_prev.astype(jnp.float32).reshape(B, SSD_GROUPS, SSD_HPG, SSD_HEADDIM, SSD_STATE)
    y, s_new = ssd_chunked(xs * dt[..., None], A * dt, Bm, Cm, h0)
    y = y + d_skip.astype(jnp.float32).reshape(SSD_GROUPS, SSD_HPG, 1) * xs
    y = y.reshape(B, T, D_INNER) * jax.nn.silu(z.astype(jnp.float32))
    yg = y.reshape(B, T, SSD_GROUPS, D_INNER // SSD_GROUPS)
    yg = yg * lax.rsqrt(jnp.mean(yg * yg, axis=-1, keepdims=True) + EPS)
    y = (yg.reshape(B, T, D_INNER) * norm_g.astype(jnp.float32)).astype(h.dtype)
    s_new = s_new.reshape(B, SSD_HEADS, SSD_HEADDIM, SSD_STATE).astype(s_prev.dtype)
    return y @ w_out, s_new, conv_new


def mem_kv(mem, g, w_k, w_v):
    B = mem.shape[0]
    m = rms_norm(mem, g)
    k = (m @ w_k).reshape(B, N_MEM, MEM_HEADS, MEM_HD)
    v = (m @ w_v).reshape(B, N_MEM, MEM_HEADS, MEM_HD)
    return k, v


def mem_attend(h, k, v, w_q, w_o):
    B, T, _ = h.shape
    q = (h @ w_q).reshape(B, T, MEM_HEADS, MEM_HD)
    s = jnp.einsum('bthd,bmhd->bhtm', q, k).astype(jnp.float32) * (MEM_HD ** -0.5)
    pr = jax.nn.softmax(s, axis=-1).astype(h.dtype)
    o = jnp.einsum('bhtm,bmhd->bthd', pr, v).reshape(B, T, D_MODEL)
    return o @ w_o


def swiglu(h, w1, w3, w2):
    return (jax.nn.silu(h @ w1) * (h @ w3)) @ w2


def moe_swiglu(h, w_router, w1, w3, w2):
    logits = (h @ w_router).astype(jnp.float32)
    top_vals, top_idx = lax.top_k(logits, TOP_K)
    gates = jax.nn.softmax(top_vals, axis=-1)
    dense_gate = jnp.sum(jax.nn.one_hot(top_idx, N_EXPERTS, dtype=jnp.float32) * gates[..., None], axis=-2)
    dense_gate = dense_gate.astype(h.dtype)
    out = dense_gate[..., 0:1] * swiglu(h, w1[0], w3[0], w2[0])
    for ex in range(1, N_EXPERTS):
        out = out + dense_gate[..., ex:ex + 1] * swiglu(h, w1[ex], w3[ex], w2[ex])
    return out


def setup_inputs(seed: int = 0) -> dict:
    key = jax.random.key(seed)
    keys = iter(jax.random.split(key, 64))

    def nrm(shape, scale):
        return jax.random.normal(next(keys), shape, jnp.float32) * scale

    def uni(shape, lo, hi):
        return jax.random.uniform(next(keys), shape, jnp.float32, lo, hi)

    x_prompt = nrm((BATCH, SEQ, D_MODEL), 1.0)
    x_sample = nrm((DEC_BATCH, DEC_SEQ, D_MODEL), 1.0)
    state_lru_h = nrm((N_EVEN, DEC_BATCH, D_RNN), 1.0)
    state_lru_conv = nrm((N_EVEN, DEC_BATCH, CONV_W - 1, D_RNN), 1.0)
    state_ssd = nrm((N_ODD, DEC_BATCH, SSD_HEADS, SSD_HEADDIM, SSD_STATE), 0.1)
    state_ssd_conv = nrm((N_ODD, DEC_BATCH, CONV_W - 1, SSD_CONV_DIM), 1.0)
    cache_mem_k = nrm((DEPTH, DEC_BATCH, N_MEM, MEM_HEADS, MEM_HD), 1.0)
    cache_mem_v = nrm((DEPTH, DEC_BATCH, N_MEM, MEM_HEADS, MEM_HD), 1.0)
    mem_prompt = nrm((BATCH, N_MEM, D_MODEL), 1.0)
    norm_mix = 1.0 + nrm((DEPTH, D_MODEL), 0.01)
    norm_mem = 1.0 + nrm((DEPTH, D_MODEL), 0.01)
    norm_memkv = 1.0 + nrm((DEPTH, D_MODEL), 0.01)
    norm_ffn = 1.0 + nrm((DEPTH, D_MODEL), 0.01)
    norm_final = 1.0 + nrm((D_MODEL,), 0.01)
    lru_w_in = nrm((N_EVEN, D_MODEL, 2 * D_RNN), D_MODEL ** -0.5)
    lru_conv_w = nrm((N_EVEN, CONV_W, D_RNN), CONV_W ** -0.5)
    lru_conv_b = nrm((N_EVEN, D_RNN), 0.01)
    lru_w_a = nrm((N_EVEN, LRU_BLOCKS, LRU_BW, LRU_BW), LRU_BW ** -0.5)
    lru_b_a = nrm((N_EVEN, D_RNN), 0.01)
    lru_w_x = nrm((N_EVEN, LRU_BLOCKS, LRU_BW, LRU_BW), LRU_BW ** -0.5)
    lru_b_x = nrm((N_EVEN, D_RNN), 0.01)
    a_base = uni((N_EVEN, D_RNN), 0.9, 0.999) ** (1.0 / LRU_C)
    lru_lam = jnp.log(a_base) - jnp.log1p(-a_base)
    lru_w_out = nrm((N_EVEN, D_RNN, D_MODEL), D_RNN ** -0.5)
    ssd_w_in = nrm((N_ODD, D_MODEL, SSD_IN_DIM), D_MODEL ** -0.5)
    ssd_conv_w = nrm((N_ODD, CONV_W, SSD_CONV_DIM), CONV_W ** -0.5)
    ssd_conv_b = nrm((N_ODD, SSD_CONV_DIM), 0.01)
    dt0 = jnp.exp(uni((N_ODD, SSD_HEADS), math.log(1e-3), math.log(1e-1)))
    ssd_dt_bias = dt0 + jnp.log(-jnp.expm1(-dt0))
    ssd_a_log = jnp.log(uni((N_ODD, SSD_HEADS), 1.0, 16.0))
    ssd_d = 1.0 + nrm((N_ODD, SSD_HEADS), 0.01)
    ssd_norm_g = 1.0 + nrm((N_ODD, D_INNER), 0.01)
    ssd_w_out = nrm((N_ODD, D_INNER, D_MODEL), D_INNER ** -0.5)
    mem_w_q = nrm((DEPTH, D_MODEL, D_MODEL), D_MODEL ** -0.5)
    mem_w_k = nrm((DEPTH, D_MODEL, D_MODEL), D_MODEL ** -0.5)
    mem_w_v = nrm((DEPTH, D_MODEL, D_MODEL), D_MODEL ** -0.5)
    mem_w_o = nrm((DEPTH, D_MODEL, D_MODEL), D_MODEL ** -0.5)
    ffn_w1 = nrm((N_EVEN, D_MODEL, D_FF), D_MODEL ** -0.5)
    ffn_w3 = nrm((N_EVEN, D_MODEL, D_FF), D_MODEL ** -0.5)
    ffn_w2 = nrm((N_EVEN, D_FF, D_MODEL), D_FF ** -0.5)
    moe_router = nrm((N_ODD, D_MODEL, N_EXPERTS), D_MODEL ** -0.5)
    moe_w1 = nrm((N_ODD, N_EXPERTS, D_MODEL, D_FF_EXPERT), D_MODEL ** -0.5)
    moe_w3 = nrm((N_ODD, N_EXPERTS, D_MODEL, D_FF_EXPERT), D_MODEL ** -0.5)
    moe_w2 = nrm((N_ODD, N_EXPERTS, D_FF_EXPERT, D_MODEL), D_FF_EXPERT ** -0.5)
    return {'x_prompt': x_prompt, 'x_sample': x_sample,
            'state_lru_h': state_lru_h, 'state_lru_conv': state_lru_conv,
            'state_ssd': state_ssd, 'state_ssd_conv': state_ssd_conv,
            'cache_mem_k': cache_mem_k, 'cache_mem_v': cache_mem_v,
            'mem_prompt': mem_prompt,
            'norm_mix': norm_mix, 'norm_mem': norm_mem, 'norm_memkv': norm_memkv,
            'norm_ffn': norm_ffn, 'norm_final': norm_final,
            'lru_w_in': lru_w_in, 'lru_conv_w': lru_conv_w, 'lru_conv_b': lru_conv_b,
            'lru_w_a': lru_w_a, 'lru_b_a': lru_b_a, 'lru_w_x': lru_w_x, 'lru_b_x': lru_b_x,
            'lru_lam': lru_lam, 'lru_w_out': lru_w_out,
            'ssd_w_in': ssd_w_in, 'ssd_conv_w': ssd_conv_w, 'ssd_conv_b': ssd_conv_b,
            'ssd_dt_bias': ssd_dt_bias, 'ssd_a_log': ssd_a_log, 'ssd_d': ssd_d,
            'ssd_norm_g': ssd_norm_g, 'ssd_w_out': ssd_w_out,
            'mem_w_q': mem_w_q, 'mem_w_k': mem_w_k, 'mem_w_v': mem_w_v, 'mem_w_o': mem_w_o,
            'ffn_w1': ffn_w1, 'ffn_w3': ffn_w3, 'ffn_w2': ffn_w2,
            'moe_router': moe_router, 'moe_w1': moe_w1, 'moe_w3': moe_w3, 'moe_w2': moe_w2}


def reference(x_prompt, x_sample, state_lru_h, state_lru_conv, state_ssd, state_ssd_conv,
              cache_mem_k, cache_mem_v, mem_prompt,
              norm_mix, norm_mem, norm_memkv, norm_ffn, norm_final,
              lru_w_in, lru_conv_w, lru_conv_b, lru_w_a, lru_b_a, lru_w_x, lru_b_x, lru_lam, lru_w_out,
              ssd_w_in, ssd_conv_w, ssd_conv_b, ssd_dt_bias, ssd_a_log, ssd_d, ssd_norm_g, ssd_w_out,
              mem_w_q, mem_w_k, mem_w_v, mem_w_o,
              ffn_w1, ffn_w3, ffn_w2, moe_router, moe_w1, moe_w3, moe_w2):

    def layer(i, x, conv_prev, rec_prev, mk, mv):
        j = i // 2
        h = rms_norm(x, norm_mix[i])
        if i % 2 == 0:
            y, rec_new, conv_new = rglru_block(h, conv_prev, rec_prev, lru_w_in[j], lru_conv_w[j], lru_conv_b[j],
                                               lru_w_a[j], lru_b_a[j], lru_w_x[j], lru_b_x[j], lru_lam[j], lru_w_out[j])
        else:
            y, rec_new, conv_new = ssd_block(h, conv_prev, rec_prev, ssd_w_in[j], ssd_conv_w[j], ssd_conv_b[j],
                                             ssd_dt_bias[j], ssd_a_log[j], ssd_d[j], ssd_norm_g[j], ssd_w_out[j])
        x = x + y
        x = x + mem_attend(rms_norm(x, norm_mem[i]), mk, mv, mem_w_q[i], mem_w_o[i])
        h = rms_norm(x, norm_ffn[i])
        if i % 2 == 0:
            x = x + swiglu(h, ffn_w1[j], ffn_w3[j], ffn_w2[j])
        else:
            x = x + moe_swiglu(h, moe_router[j], moe_w1[j], moe_w3[j], moe_w2[j])
        return x, rec_new, conv_new

    bp = x_prompt.shape[0]
    xp, xs = x_prompt, x_sample
    p_lru_h, p_lru_conv, p_ssd, p_ssd_conv, p_mk, p_mv = [], [], [], [], [], []
    s_lru_h, s_lru_conv, s_ssd, s_ssd_conv = [], [], [], []
    for i in range(DEPTH):
        j = i // 2
        mk, mv = mem_kv(mem_prompt, norm_memkv[i], mem_w_k[i], mem_w_v[i])
        p_mk.append(mk)
        p_mv.append(mv)
        if i % 2 == 0:
            conv0 = jnp.zeros((bp, CONV_W - 1, D_RNN), x_prompt.dtype)
            rec0 = jnp.zeros((bp, D_RNN), state_lru_h.dtype)
            xp, r, c = layer(i, xp, conv0, rec0, mk, mv)
            p_lru_h.append(r)
            p_lru_conv.append(c)
            xs, r, c = layer(i, xs, state_lru_conv[j], state_lru_h[j], cache_mem_k[i], cache_mem_v[i])
            s_lru_h.append(r)
            s_lru_conv.append(c)
        else:
            conv0 = jnp.zeros((bp, CONV_W - 1, SSD_CONV_DIM), x_prompt.dtype)
            rec0 = jnp.zeros((bp, SSD_HEADS, SSD_HEADDIM, SSD_STATE), state_ssd.dtype)
            xp, r, c = layer(i, xp, conv0, rec0, mk, mv)
            p_ssd.append(r)
            p_ssd_conv.append(c)
            xs, r, c = layer(i, xs, state_ssd_conv[j], state_ssd[j], cache_mem_k[i], cache_mem_v[i])
            s_ssd.append(r)
            s_ssd_conv.append(c)
    y_prompt = rms_norm(xp, norm_final)
    y_sample = rms_norm(xs, norm_final)
    return (y_prompt, y_sample,
            jnp.stack(p_lru_h), jnp.stack(p_lru_conv), jnp.stack(p_ssd), jnp.stack(p_ssd_conv),
            jnp.stack(p_mk), jnp.stack(p_mv),
            jnp.stack(s_lru_h), jnp.stack(s_lru_conv), jnp.stack(s_ssd), jnp.stack(s_ssd_conv))
```

```python
import functools
import math

import jax
import jax.numpy as jnp
from jax import lax
from jax.experimental import pallas as pl
from jax.experimental.pallas import tpu as pltpu

F32 = jnp.float32
BF16 = jnp.bfloat16

D_MODEL = 1024
BATCH = 8
SEQ = 2048
DEPTH = 4
DEC_BATCH = 128
DEC_SEQ = 4
CONV_W = 4
EPS = 1e-6
D_RNN = D_MODEL
LRU_BLOCKS = 8
LRU_BW = D_RNN // LRU_BLOCKS
LRU_C = 8.0
D_INNER = 2 * D_MODEL
SSD_HEADDIM = 64
SSD_HEADS = D_INNER // SSD_HEADDIM
SSD_GROUPS = 4
SSD_GROUP_W = D_INNER // SSD_GROUPS
SSD_STATE = 128
SSD_GN = SSD_GROUPS * SSD_STATE
SSD_CONV_DIM = D_INNER + 2 * SSD_GN
SSD_CHUNK = 128
N_MEM = 256
MEM_HEADS = 4
MEM_HD = D_MODEL // MEM_HEADS
D_FF = 2816
N_EXPERTS = 8
D_FF_EXPERT = 3584

LANES = 128
SUBLANES = 8
VMEM_LIMIT_BYTES = 56 * 1024 * 1024

N_PROMPT = BATCH * SEQ
N_SAMPLE = DEC_BATCH * DEC_SEQ
N_ROWS = N_PROMPT + N_SAMPLE
ROW_TILE = 512
SAMPLE_BLOCK = N_PROMPT // N_SAMPLE
HALO = SUBLANES


def _cparams(*sem):
    return pltpu.CompilerParams(dimension_semantics=sem, vmem_limit_bytes=VMEM_LIMIT_BYTES)


def _rms(x, g):
    return x * lax.rsqrt(jnp.mean(x * x, axis=-1, keepdims=True) + EPS) * g


def _sigmoid(x):
    return 1.0 / (1.0 + jnp.exp(-x))


def _softplus(x):
    return jnp.maximum(x, 0.0) + jnp.log1p(jnp.exp(-jnp.abs(x)))


def _split3(x):
    a = x.astype(BF16)
    r = x - a.astype(F32)
    b = r.astype(BF16)
    c = (r - b.astype(F32)).astype(BF16)
    return a, b, c


def _dot(a, b):
    return jnp.dot(a, b, preferred_element_type=F32)


def _dot_f32_lhs(x, m):
    a, b, c = _split3(x)
    return _dot(a, m) + _dot(b, m) + _dot(c, m)


def _dot_f32_rhs(m, x):
    a, b, c = _split3(x)
    return _dot(m, a) + _dot(m, b) + _dot(m, c)


def _norm_matmul_kernel(n_w, x_ref, g_ref, *refs):
    h = _rms(x_ref[...], g_ref[...]).astype(BF16)
    for w_ref, o_ref in zip(refs[:n_w], refs[n_w:]):
        o_ref[...] = _dot(h, w_ref[...]).astype(o_ref.dtype)


def norm_matmul(x, g, ws, out_dtypes, tm=ROW_TILE):
    rows, k = x.shape
    in_specs = [pl.BlockSpec((tm, k), lambda i: (i, 0)), pl.BlockSpec((1, k), lambda i: (0, 0))]
    in_specs += [pl.BlockSpec(w.shape, lambda i: (0, 0)) for w in ws]
    out_specs = [pl.BlockSpec((tm, w.shape[1]), lambda i: (i, 0)) for w in ws]
    out_shape = [jax.ShapeDtypeStruct((rows, w.shape[1]), dt) for w, dt in zip(ws, out_dtypes)]
    return pl.pallas_call(
        functools.partial(_norm_matmul_kernel, len(ws)),
        grid=(rows // tm,),
        in_specs=in_specs,
        out_specs=out_specs,
        out_shape=out_shape,
        compiler_params=_cparams("parallel"),
        name="norm_matmul",
    )(x, g.reshape(1, k), *ws)


def _matmul_residual_kernel(y_ref, w_ref, r_ref, o_ref):
    o_ref[...] = r_ref[...] + _dot(y_ref[...].astype(BF16), w_ref[...])


def matmul_residual(y, w, res, tm=ROW_TILE):
    rows, k = y.shape
    n = w.shape[1]
    return pl.pallas_call(
        _matmul_residual_kernel,
        grid=(rows // tm,),
        in_specs=[
            pl.BlockSpec((tm, k), lambda i: (i, 0)),
            pl.BlockSpec((k, n), lambda i: (0, 0)),
            pl.BlockSpec((tm, n), lambda i: (i, 0)),
        ],
        out_specs=pl.BlockSpec((tm, n), lambda i: (i, 0)),
        out_shape=jax.ShapeDtypeStruct((rows, n), F32),
        compiler_params=_cparams("parallel"),
        name="matmul_residual",
    )(y, w, res)


def _rmsnorm_kernel(x_ref, g_ref, o_ref):
    o_ref[...] = _rms(x_ref[...], g_ref[...])


def rmsnorm_rows(x, g, first_block, n_blocks, tm=ROW_TILE):
    k = x.shape[1]
    return pl.pallas_call(
        _rmsnorm_kernel,
        grid=(n_blocks,),
        in_specs=[pl.BlockSpec((tm, k), lambda i: (i + first_block, 0)), pl.BlockSpec((1, k), lambda i: (0, 0))],
        out_specs=pl.BlockSpec((tm, k), lambda i: (i, 0)),
        out_shape=jax.ShapeDtypeStruct((n_blocks * tm, k), F32),
        compiler_params=_cparams("parallel"),
        name="final_norm",
    )(x, g.reshape(1, k))


def _swiglu_partial(h, w1_ref, w3_ref, w2_ref):
    a = _dot(h, w1_ref[...])
    b = _dot(h, w3_ref[...])
    return _dot((a * _sigmoid(a) * b).astype(BF16), w2_ref[...])


def _ffn_kernel(x_ref, g_ref, w1_ref, w3_ref, w2_ref, o_ref, h_scr, acc_scr):
    j = pl.program_id(1)

    @pl.when(j == 0)
    def _():
        h_scr[...] = _rms(x_ref[...], g_ref[...]).astype(BF16)
        acc_scr[...] = jnp.zeros_like(acc_scr)

    acc_scr[...] += _swiglu_partial(h_scr[...], w1_ref, w3_ref, w2_ref)

    @pl.when(j == pl.num_programs(1) - 1)
    def _():
        o_ref[...] = x_ref[...] + acc_scr[...]


def ffn(x, g, w1, w3, w2, tf, tm=ROW_TILE):
    rows, k = x.shape
    f = w1.shape[1]
    return pl.pallas_call(
        _ffn_kernel,
        grid=(rows // tm, f // tf),
        in_specs=[
            pl.BlockSpec((tm, k), lambda i, j: (i, 0)),
            pl.BlockSpec((1, k), lambda i, j: (0, 0)),
            pl.BlockSpec((k, tf), lambda i, j: (0, j)),
            pl.BlockSpec((k, tf), lambda i, j: (0, j)),
            pl.BlockSpec((tf, k), lambda i, j: (j, 0)),
        ],
        out_specs=pl.BlockSpec((tm, k), lambda i, j: (i, 0)),
        out_shape=jax.ShapeDtypeStruct((rows, k), F32),
        scratch_shapes=[pltpu.VMEM((tm, k), BF16), pltpu.VMEM((tm, k), F32)],
        compiler_params=_cparams("parallel", "arbitrary"),
        name="ffn",
    )(x, g.reshape(1, k), w1, w3, w2)


def _router_kernel(x_ref, g_ref, wr_ref, h_ref, gate_ref):
    h = _rms(x_ref[...], g_ref[...])
    h_ref[...] = h.astype(BF16)
    h1 = h.astype(BF16)
    h2 = (h - h1.astype(F32)).astype(BF16)
    w = wr_ref[...]
    w1 = w.astype(BF16)
    w2 = (w - w1.astype(F32)).astype(BF16)
    logits = _dot(h1, w1) + _dot(h1, w2) + _dot(h2, w1)
    lane = lax.broadcasted_iota(jnp.int32, logits.shape, 1).astype(F32)
    neg = jnp.float32(-jnp.inf)
    logits = jnp.where(lane < N_EXPERTS, logits, neg)
    m1 = jnp.max(logits, axis=-1, keepdims=True)
    i1 = jnp.min(jnp.where(logits == m1, lane, float(LANES)), axis=-1, keepdims=True)
    rest = jnp.where(lane == i1, neg, logits)
    m2 = jnp.max(rest, axis=-1, keepdims=True)
    i2 = jnp.min(jnp.where(rest == m2, lane, float(LANES)), axis=-1, keepdims=True)
    e2 = jnp.exp(m2 - m1)
    den = 1.0 + e2
    gate_ref[...] = jnp.where(lane == i1, 1.0 / den, 0.0) + jnp.where(lane == i2, e2 / den, 0.0)


def router(x, g, w_router, tm=ROW_TILE):
    rows, k = x.shape
    wr = jnp.pad(w_router, ((0, 0), (0, LANES - N_EXPERTS)))
    return pl.pallas_call(
        _router_kernel,
        grid=(rows // tm,),
        in_specs=[
            pl.BlockSpec((tm, k), lambda i: (i, 0)),
            pl.BlockSpec((1, k), lambda i: (0, 0)),
            pl.BlockSpec((k, LANES), lambda i: (0, 0)),
        ],
        out_specs=[pl.BlockSpec((tm, k), lambda i: (i, 0)), pl.BlockSpec((tm, LANES), lambda i: (i, 0))],
        out_shape=[jax.ShapeDtypeStruct((rows, k), BF16), jax.ShapeDtypeStruct((rows, LANES), F32)],
        compiler_params=_cparams("parallel"),
        name="router",
    )(x, g.reshape(1, k), wr)


def _expert_kernel(expert, h_ref, gate_ref, acc_in_ref, w1_ref, w3_ref, w2_ref, o_ref, acc_scr):
    j = pl.program_id(1)

    @pl.when(j == 0)
    def _():
        acc_scr[...] = jnp.zeros_like(acc_scr)

    acc_scr[...] += _swiglu_partial(h_ref[...], w1_ref, w3_ref, w2_ref)

    @pl.when(j == pl.num_programs(1) - 1)
    def _():
        o_ref[...] = acc_in_ref[...] + gate_ref[:, expert : expert + 1] * acc_scr[...]


def expert_accumulate(h, gate, acc, w1, w3, w2, expert, tf, tm=ROW_TILE):
    rows, k = h.shape
    f = w1.shape[1]
    return pl.pallas_call(
        functools.partial(_expert_kernel, expert),
        grid=(rows // tm, f // tf),
        in_specs=[
            pl.BlockSpec((tm, k), lambda i, j: (i, 0)),
            pl.BlockSpec((tm, LANES), lambda i, j: (i, 0)),
            pl.BlockSpec((tm, k), lambda i, j: (i, 0)),
            pl.BlockSpec((k, tf), lambda i, j: (0, j)),
            pl.BlockSpec((k, tf), lambda i, j: (0, j)),
            pl.BlockSpec((tf, k), lambda i, j: (j, 0)),
        ],
        out_specs=pl.BlockSpec((tm, k), lambda i, j: (i, 0)),
        out_shape=jax.ShapeDtypeStruct((rows, k), F32),
        scratch_shapes=[pltpu.VMEM((tm, k), F32)],
        compiler_params=_cparams("parallel", "arbitrary"),
        name="moe_expert",
    )(h, gate, acc, w1, w3, w2)


def _conv_taps(xpad, base, rows, step, cw, cb):
    y = xpad[base - 3 * step : base - 3 * step + rows, :] * cw[0:1]
    for k in range(1, CONV_W):
        lo = base - (CONV_W - 1 - k) * step
        y = y + xpad[lo : lo + rows, :] * cw[k : k + 1]
    return y + cb


def _lru_gates(xc, wa_ref, ba_ref, wx_ref, bx_ref, lam_ref):
    xcb = xc.astype(BF16)
    r_parts, i_parts = [], []
    for k in range(LRU_BLOCKS):
        blk = xcb[:, k * LRU_BW : (k + 1) * LRU_BW]
        r_parts.append(_dot(blk, wa_ref[k]))
        i_parts.append(_dot(blk, wx_ref[k]))
    r = _sigmoid(jnp.concatenate(r_parts, axis=1) + ba_ref[...])
    ig = _sigmoid(jnp.concatenate(i_parts, axis=1) + bx_ref[...])
    log_a = (-LRU_C * r) * _softplus(-lam_ref[...])
    a = jnp.exp(log_a)
    one_minus_a2 = -jnp.tanh(log_a) * (a * a + 1.0)
    u = jnp.sqrt(one_minus_a2) * (ig * xc)
    return a, u


def _lru_prompt_kernel(tt, proj_ref, cw_ref, cb_ref, wa_ref, ba_ref, wx_ref, bx_ref, lam_ref,
                       y_ref, hlast_ref, convn_ref, xpad, a_scr, u_scr, hs_scr, h_scr):
    t = pl.program_id(1)

    @pl.when(t == 0)
    def _():
        xpad[0:HALO, :] = jnp.zeros((HALO, D_RNN), F32)
        h_scr[...] = jnp.zeros_like(h_scr)

    @pl.when(t > 0)
    def _():
        xpad[0:HALO, :] = xpad[tt : tt + HALO, :]

    xpad[HALO : HALO + tt, :] = proj_ref[:, D_RNN:]
    xc = _conv_taps(xpad, HALO, tt, 1, cw_ref[...], cb_ref[...])
    a, u = _lru_gates(xc, wa_ref, ba_ref, wx_ref, bx_ref, lam_ref)
    a_scr[...] = a
    u_scr[...] = u

    def body(i, h):
        h = a_scr[pl.ds(i, 1), :] * h + u_scr[pl.ds(i, 1), :]
        hs_scr[pl.ds(i, 1), :] = h
        return h

    h_scr[...] = lax.fori_loop(0, tt, body, h_scr[...], unroll=8)
    y_ref[...] = hs_scr[...] * jax.nn.gelu(proj_ref[:, :D_RNN])

    @pl.when(t == pl.num_programs(1) - 1)
    def _():
        hlast_ref[...] = h_scr[...]
        convn_ref[...] = xpad[tt : tt + HALO, :]


def lru_prompt(proj, p, tt=256):
    n_t = SEQ // tt
    wspec = lambda shape: pl.BlockSpec(shape, lambda b, t: (0,) * len(shape))
    return pl.pallas_call(
        functools.partial(_lru_prompt_kernel, tt),
        grid=(BATCH, n_t),
        in_specs=[
            pl.BlockSpec((tt, 2 * D_RNN), lambda b, t: (b * n_t + t, 0)),
            wspec((CONV_W, D_RNN)), wspec((1, D_RNN)),
            wspec((LRU_BLOCKS, LRU_BW, LRU_BW)), wspec((1, D_RNN)),
            wspec((LRU_BLOCKS, LRU_BW, LRU_BW)), wspec((1, D_RNN)),
            wspec((1, D_RNN)),
        ],
        out_specs=[
            pl.BlockSpec((tt, D_RNN), lambda b, t: (b * n_t + t, 0)),
            pl.BlockSpec((None, 1, D_RNN), lambda b, t: (b, 0, 0)),
            pl.BlockSpec((None, HALO, D_RNN), lambda b, t: (b, 0, 0)),
        ],
        out_shape=[
            jax.ShapeDtypeStruct((N_ROWS, D_RNN), F32),
            jax.ShapeDtypeStruct((BATCH, 1, D_RNN), F32),
            jax.ShapeDtypeStruct((BATCH, HALO, D_RNN), F32),
        ],
        scratch_shapes=[
            pltpu.VMEM((HALO + tt, D_RNN), F32),
            pltpu.VMEM((tt, D_RNN), F32),
            pltpu.VMEM((tt, D_RNN), F32),
            pltpu.VMEM((tt, D_RNN), F32),
            pltpu.VMEM((1, D_RNN), F32),
        ],
        compiler_params=_cparams("parallel", "arbitrary"),
        name="lru_prompt",
    )(proj, p["conv_w"], p["conv_b"], p["w_a"], p["b_a"], p["w_x"], p["b_x"], p["lam"])


def _lru_sample_kernel(proj_ref, convp_ref, hprev_ref, cw_ref, cb_ref, wa_ref, ba_ref, wx_ref, bx_ref, lam_ref,
                       y_in_ref, y_ref, hlast_ref, convn_ref, xpad):
    del y_in_ref
    hist = (CONV_W - 1) * DEC_BATCH
    xpad[0:hist, :] = convp_ref[...]
    xpad[hist:, :] = proj_ref[:, D_RNN:]
    xc = _conv_taps(xpad, hist, N_SAMPLE, DEC_BATCH, cw_ref[...], cb_ref[...])
    a, u = _lru_gates(xc, wa_ref, ba_ref, wx_ref, bx_ref, lam_ref)
    gate = jax.nn.gelu(proj_ref[:, :D_RNN])
    h = hprev_ref[...]
    for t in range(DEC_SEQ):
        rows = slice(t * DEC_BATCH, (t + 1) * DEC_BATCH)
        h = a[rows] * h + u[rows]
        y_ref[rows, :] = h * gate[rows]
    hlast_ref[...] = h
    convn_ref[...] = xpad[N_SAMPLE:, :]


def lru_sample(proj, y, conv_prev, h_prev, p):
    hist = (CONV_W - 1) * DEC_BATCH
    full = lambda shape: pl.BlockSpec(shape, lambda i: (0,) * len(shape))
    return pl.pallas_call(
        _lru_sample_kernel,
        grid=(1,),
        in_specs=[
            pl.BlockSpec((N_SAMPLE, 2 * D_RNN), lambda i: (SAMPLE_BLOCK, 0)),
            full((hist, D_RNN)), full((DEC_BATCH, D_RNN)),
            full((CONV_W, D_RNN)), full((1, D_RNN)),
            full((LRU_BLOCKS, LRU_BW, LRU_BW)), full((1, D_RNN)),
            full((LRU_BLOCKS, LRU_BW, LRU_BW)), full((1, D_RNN)),
            full((1, D_RNN)),
            pl.BlockSpec(memory_space=pl.ANY),
        ],
        out_specs=[
            pl.BlockSpec((N_SAMPLE, D_RNN), lambda i: (SAMPLE_BLOCK, 0)),
            full((DEC_BATCH, D_RNN)),
            full((hist, D_RNN)),
        ],
        out_shape=[
            jax.ShapeDtypeStruct((N_ROWS, D_RNN), F32),
            jax.ShapeDtypeStruct((DEC_BATCH, D_RNN), F32),
            jax.ShapeDtypeStruct((hist, D_RNN), F32),
        ],
        scratch_shapes=[pltpu.VMEM((hist + N_SAMPLE, D_RNN), F32)],
        input_output_aliases={10: 0},
        compiler_params=_cparams("arbitrary"),
        name="lru_sample",
    )(proj, conv_prev, h_prev, p["conv_w"], p["conv_b"], p["w_a"], p["b_a"], p["w_x"], p["b_x"], p["lam"], y)


def _attend(q, k, v):
    outs = []
    for h in range(MEM_HEADS):
        hs = slice(h * MEM_HD, (h + 1) * MEM_HD)
        s = lax.dot_general(q[:, hs], k[:, hs], (((1,), (1,)), ((), ())), preferred_element_type=F32)
        s = s * (MEM_HD ** -0.5)
        e = jnp.exp(s - jnp.max(s, axis=-1, keepdims=True))
        p = e / jnp.sum(e, axis=-1, keepdims=True)
        outs.append(_dot(p.astype(BF16), v[:, hs]))
    return jnp.concatenate(outs, axis=1)


def _attn_prompt_kernel(q_ref, k_ref, v_ref, o_ref):
    o_ref[...] = _attend(q_ref[...], k_ref[...].astype(BF16), v_ref[...].astype(BF16)).astype(o_ref.dtype)


def attn_prompt(q, k, v, tt=512):
    n_t = SEQ // tt
    return pl.pallas_call(
        _attn_prompt_kernel,
        grid=(BATCH, n_t),
        in_specs=[
            pl.BlockSpec((tt, D_MODEL), lambda b, t: (b * n_t + t, 0)),
            pl.BlockSpec((None, N_MEM, D_MODEL), lambda b, t: (b, 0, 0)),
            pl.BlockSpec((None, N_MEM, D_MODEL), lambda b, t: (b, 0, 0)),
        ],
        out_specs=pl.BlockSpec((tt, D_MODEL), lambda b, t: (b * n_t + t, 0)),
        out_shape=jax.ShapeDtypeStruct((N_ROWS, D_MODEL), BF16),
        compiler_params=_cparams("parallel", "arbitrary"),
        name="attn_prompt",
    )(q, k, v)


def _attn_sample_kernel(bb, q_ref, k_ref, v_ref, o_ref):
    for i in range(bb):
        o_ref[i] = _attend(q_ref[i], k_ref[i].astype(BF16), v_ref[i].astype(BF16)).astype(o_ref.dtype)


def attn_sample(q, k, v, bb=4):
    return pl.pallas_call(
        functools.partial(_attn_sample_kernel, bb),
        grid=(DEC_BATCH // bb,),
        in_specs=[
            pl.BlockSpec((bb, DEC_SEQ, D_MODEL), lambda i: (i, 0, 0)),
            pl.BlockSpec((bb, N_MEM, D_MODEL), lambda i: (i, 0, 0)),
            pl.BlockSpec((bb, N_MEM, D_MODEL), lambda i: (i, 0, 0)),
        ],
        out_specs=pl.BlockSpec((bb, DEC_SEQ, D_MODEL), lambda i: (i, 0, 0)),
        out_shape=jax.ShapeDtypeStruct((DEC_BATCH, DEC_SEQ, D_MODEL), F32),
        compiler_params=_cparams("parallel"),
        name="attn_sample",
    )(q, k, v)


def _ssd_gate_norm(y, z, ng):
    y = y * (z * _sigmoid(z))
    outs = []
    for g in range(SSD_GROUPS):
        yg = y[:, g * SSD_GROUP_W : (g + 1) * SSD_GROUP_W]
        outs.append(yg * lax.rsqrt(jnp.mean(yg * yg, axis=-1, keepdims=True) + EPS))
    return jnp.concatenate(outs, axis=1) * ng


def _ssd_prompt_kernel(q, z_ref, xbc_ref, dt_ref, cw_ref, cb_ref, dtb_ref, alog_ref, dskip_ref, ng_ref, e_ref,
                       y_ref, st_ref, convn_ref, xpad, s_t, y_scr):
    t = pl.program_id(1)

    @pl.when(t == 0)
    def _():
        xpad[0:HALO, :] = jnp.zeros((HALO, SSD_CONV_DIM), F32)
        s_t[...] = jnp.zeros_like(s_t)

    @pl.when(t > 0)
    def _():
        xpad[0:HALO, :] = xpad[q : q + HALO, :]

    xpad[HALO : HALO + q, :] = xbc_ref[...]
    xc = _conv_taps(xpad, HALO, q, 1, cw_ref[...], cb_ref[...])
    xbc = xc * _sigmoid(xc)
    xs = xbc[:, :D_INNER]

    dt = _softplus(dt_ref[...] + dtb_ref[...])
    adt = -jnp.exp(alog_ref[...]) * dt
    row_i = lax.broadcasted_iota(jnp.int32, (q, q), 0)
    col_i = lax.broadcasted_iota(jnp.int32, (q, q), 1)
    tril = row_i >= col_i
    a_cs = _dot_f32_rhs(jnp.where(tril, 1.0, 0.0).astype(BF16), adt)
    a_cs_t = a_cs.T
    a_end = a_cs[q - 1 : q, :]
    expand = e_ref[...]
    dt_x = _dot_f32_lhs(dt, expand)
    ecs_x = _dot_f32_lhs(jnp.exp(a_cs), expand)
    dst_x = _dot_f32_lhs(jnp.exp(a_end - a_cs), expand)
    x_dt = xs * dt_x
    xb = x_dt.astype(BF16)
    xd = (x_dt * dst_x).astype(BF16)

    for g in range(SSD_GROUPS):
        gc = slice(g * SSD_GROUP_W, (g + 1) * SSD_GROUP_W)
        bg = xbc[:, D_INNER + g * SSD_STATE : D_INNER + (g + 1) * SSD_STATE]
        cg = xbc[:, D_INNER + SSD_GN + g * SSD_STATE : D_INNER + SSD_GN + (g + 1) * SSD_STATE].astype(BF16)
        cb_mat = lax.dot_general(cg, bg.astype(BF16), (((1,), (1,)), ((), ())), preferred_element_type=F32)
        sg = s_t[:, gc]
        y_scr[:, gc] = _dot(cg, sg.astype(BF16)) * ecs_x[:, gc]
        s_t[:, gc] = ecs_x[q - 1 : q, gc] * sg + _dot(bg.T.astype(BF16), xd[:, gc])
        for e in range(SSD_HEADS // SSD_GROUPS):
            h = g * (SSD_HEADS // SSD_GROUPS) + e
            hc = slice(h * SSD_HEADDIM, (h + 1) * SSD_HEADDIM)
            seg = a_cs[:, h : h + 1] - a_cs_t[h : h + 1, :]
            decay = jnp.where(tril, jnp.exp(jnp.minimum(seg, 0.0)), 0.0)
            y_scr[:, hc] += _dot((cb_mat * decay).astype(BF16), xb[:, hc])

    y = y_scr[...] + dskip_ref[...] * xs
    y_ref[...] = _ssd_gate_norm(y, z_ref[...], ng_ref[...]).astype(y_ref.dtype)

    @pl.when(t == pl.num_programs(1) - 1)
    def _():
        for j in range(D_INNER // LANES):
            st_ref[j * LANES : (j + 1) * LANES, :] = s_t[:, j * LANES : (j + 1) * LANES].T
        convn_ref[...] = xpad[q : q + HALO, :]


def ssd_prompt(z, xbc, dt, p, q=SSD_CHUNK):
    n_t = SEQ // q
    rows = lambda w: pl.BlockSpec((q, w), lambda b, t: (b * n_t + t, 0))
    wspec = lambda shape: pl.BlockSpec(shape, lambda b, t: (0,) * len(shape))
    return pl.pallas_call(
        functools.partial(_ssd_prompt_kernel, q),
        grid=(BATCH, n_t),
        in_specs=[
            rows(D_INNER), rows(SSD_CONV_DIM), rows(LANES),
            wspec((CONV_W, SSD_CONV_DIM)), wspec((1, SSD_CONV_DIM)),
            wspec((1, LANES)), wspec((1, LANES)), wspec((1, D_INNER)), wspec((1, D_INNER)),
            wspec((LANES, D_INNER)),
        ],
        out_specs=[
            rows(D_INNER),
            pl.BlockSpec((None, D_INNER, SSD_STATE), lambda b, t: (b, 0, 0)),
            pl.BlockSpec((None, HALO, SSD_CONV_DIM), lambda b, t: (b, 0, 0)),
        ],
        out_shape=[
            jax.ShapeDtypeStruct((N_ROWS, D_INNER), BF16),
            jax.ShapeDtypeStruct((BATCH, D_INNER, SSD_STATE), F32),
            jax.ShapeDtypeStruct((BATCH, HALO, SSD_CONV_DIM), F32),
        ],
        scratch_shapes=[
            pltpu.VMEM((HALO + q, SSD_CONV_DIM), F32),
            pltpu.VMEM((SSD_STATE, D_INNER), F32),
            pltpu.VMEM((q, D_INNER), F32),
        ],
        compiler_params=_cparams("parallel", "arbitrary"),
        name="ssd_prompt",
    )(z, xbc, dt, p["conv_w"], p["conv_b"], p["dt_bias"], p["a_log"], p["d_skip"], p["norm_g"], p["expand"])


def _ssd_sample_pre_kernel(xbc_ref, dt_ref, convp_ref, cw_ref, cb_ref, dtb_ref, alog_ref, e_ref,
                           xs_ref, xdt_ref, dec_ref, bm_ref, cm_ref, convn_ref, xpad):
    hist = (CONV_W - 1) * DEC_BATCH
    xpad[0:hist, :] = convp_ref[...]
    xpad[hist:, :] = xbc_ref[...]
    xc = _conv_taps(xpad, hist, N_SAMPLE, DEC_BATCH, cw_ref[...], cb_ref[...])
    xbc = xc * _sigmoid(xc)
    xs = xbc[:, :D_INNER]
    dt = _softplus(dt_ref[...] + dtb_ref[...])
    adt = -jnp.exp(alog_ref[...]) * dt
    expand = e_ref[...]
    xs_ref[...] = xs
    xdt_ref[...] = xs * _dot_f32_lhs(dt, expand)
    dec_ref[...] = _dot_f32_lhs(jnp.exp(adt), expand)
    bm_ref[...] = xbc[:, D_INNER : D_INNER + SSD_GN]
    cm_ref[...] = xbc[:, D_INNER + SSD_GN :]
    convn_ref[...] = xpad[N_SAMPLE:, :]


def ssd_sample_pre(xbc, dt, conv_prev, p):
    hist = (CONV_W - 1) * DEC_BATCH
    full = lambda shape: pl.BlockSpec(shape, lambda i: (0,) * len(shape))
    out_w = [D_INNER, D_INNER, D_INNER, SSD_GN, SSD_GN]
    return pl.pallas_call(
        _ssd_sample_pre_kernel,
        grid=(1,),
        in_specs=[
            pl.BlockSpec((N_SAMPLE, SSD_CONV_DIM), lambda i: (SAMPLE_BLOCK, 0)),
            pl.BlockSpec((N_SAMPLE, LANES), lambda i: (SAMPLE_BLOCK, 0)),
            full((hist, SSD_CONV_DIM)),
            full((CONV_W, SSD_CONV_DIM)), full((1, SSD_CONV_DIM)),
            full((1, LANES)), full((1, LANES)), full((LANES, D_INNER)),
        ],
        out_specs=[full((N_SAMPLE, w)) for w in out_w] + [full((hist, SSD_CONV_DIM))],
        out_shape=[jax.ShapeDtypeStruct((N_SAMPLE, w), F32) for w in out_w]
        + [jax.ShapeDtypeStruct((hist, SSD_CONV_DIM), F32)],
        scratch_shapes=[pltpu.VMEM((hist + N_SAMPLE, SSD_CONV_DIM), F32)],
        compiler_params=_cparams("arbitrary"),
        name="ssd_sample_pre",
    )(xbc, dt, conv_prev, p["conv_w"], p["conv_b"], p["dt_bias"], p["a_log"], p["expand"])


def _ssd_recur_kernel(xdt_ref, dec_ref, bm_ref, cm_ref, st_in_ref, y_ref, st_out_ref):
    xdt = xdt_ref[...]
    dec = dec_ref[...]
    heads_per_block = LANES // SSD_HEADDIM
    for g in range(SSD_GROUPS):
        b_t = bm_ref[:, g * SSD_STATE : (g + 1) * SSD_STATE].T
        c_t = cm_ref[:, g * SSD_STATE : (g + 1) * SSD_STATE].T
        b_cols = [jnp.broadcast_to(b_t[:, t : t + 1], (SSD_STATE, LANES)) for t in range(DEC_SEQ)]
        c_cols = [jnp.broadcast_to(c_t[:, t : t + 1], (SSD_STATE, LANES)) for t in range(DEC_SEQ)]
        for jb in range(SSD_GROUP_W // LANES):
            j = g * (SSD_GROUP_W // LANES) + jb
            cols = slice(j * LANES, (j + 1) * LANES)
            s = st_in_ref[cols, :].T
            for t in range(DEC_SEQ):
                s = dec[t : t + 1, cols] * s + b_cols[t] * xdt[t : t + 1, cols]
                y_ref[t : t + 1, cols] = jnp.sum(c_cols[t] * s, axis=0, keepdims=True)
            st_out_ref[cols, :] = s.T
    del heads_per_block


def ssd_recur(xdt, dec, bm, cm, state):
    seq = lambda w: pl.BlockSpec((None, DEC_SEQ, w), lambda b: (b, 0, 0))
    st = pl.BlockSpec((None, D_INNER, SSD_STATE), lambda b: (b, 0, 0))
    return pl.pallas_call(
        _ssd_recur_kernel,
        grid=(DEC_BATCH,),
        in_specs=[seq(D_INNER), seq(D_INNER), seq(SSD_GN), seq(SSD_GN), st],
        out_specs=[seq(D_INNER), st],
        out_shape=[
            jax.ShapeDtypeStruct((DEC_BATCH, DEC_SEQ, D_INNER), F32),
            jax.ShapeDtypeStruct((DEC_BATCH, D_INNER, SSD_STATE), F32),
        ],
        compiler_params=_cparams("parallel"),
        name="ssd_recur",
    )(xdt, dec, bm, cm, state)


def _ssd_sample_post_kernel(yr_ref, xs_ref, z_ref, dskip_ref, ng_ref, y_in_ref, y_ref):
    del y_in_ref
    y = yr_ref[...] + dskip_ref[...] * xs_ref[...]
    y_ref[...] = _ssd_gate_norm(y, z_ref[...], ng_ref[...]).astype(y_ref.dtype)


def ssd_sample_post(y_raw, xs, z, y, p):
    full = lambda shape: pl.BlockSpec(shape, lambda i: (0,) * len(shape))
    return pl.pallas_call(
        _ssd_sample_post_kernel,
        grid=(1,),
        in_specs=[
            full((N_SAMPLE, D_INNER)), full((N_SAMPLE, D_INNER)),
            pl.BlockSpec((N_SAMPLE, D_INNER), lambda i: (SAMPLE_BLOCK, 0)),
            full((1, D_INNER)), full((1, D_INNER)),
            pl.BlockSpec(memory_space=pl.ANY),
        ],
        out_specs=pl.BlockSpec((N_SAMPLE, D_INNER), lambda i: (SAMPLE_BLOCK, 0)),
        out_shape=jax.ShapeDtypeStruct((N_ROWS, D_INNER), BF16),
        input_output_aliases={5: 0},
        compiler_params=_cparams("arbitrary"),
        name="ssd_sample_post",
    )(y_raw, xs, z, p["d_skip"], p["norm_g"], y)


def _to_time_major(a):
    return jnp.swapaxes(a, 0, 1).reshape(a.shape[0] * a.shape[1], a.shape[2])


def _to_batch_major(a, t):
    return jnp.swapaxes(a.reshape(t, DEC_BATCH, a.shape[1]), 0, 1)


def _row(v):
    return v.reshape(1, -1).astype(F32)


def _pad_lanes(v):
    return jnp.pad(v.reshape(1, -1).astype(F32), ((0, 0), (0, LANES - v.shape[-1])))


def kernel(x_prompt, x_sample, state_lru_h, state_lru_conv, state_ssd, state_ssd_conv, cache_mem_k, cache_mem_v, mem_prompt, norm_mix, norm_mem, norm_memkv, norm_ffn, norm_final, lru_w_in, lru_conv_w, lru_conv_b, lru_w_a, lru_b_a, lru_w_x, lru_b_x, lru_lam, lru_w_out, ssd_w_in, ssd_conv_w, ssd_conv_b, ssd_dt_bias, ssd_a_log, ssd_d, ssd_norm_g, ssd_w_out, mem_w_q, mem_w_k, mem_w_v, mem_w_o, ffn_w1, ffn_w3, ffn_w2, moe_router, moe_w1, moe_w3, moe_w2):
    bf = lambda w: w.astype(BF16)
    x = jnp.concatenate([x_prompt.reshape(N_PROMPT, D_MODEL), _to_time_major(x_sample)], axis=0)
    mem = mem_prompt.reshape(BATCH * N_MEM, D_MODEL)
    head_of_col = jnp.arange(D_INNER, dtype=jnp.int32) // SSD_HEADDIM
    expand = (jnp.arange(LANES, dtype=jnp.int32)[:, None] == head_of_col[None, :]).astype(BF16)

    p_lru_h, p_lru_conv, p_ssd, p_ssd_conv, p_mk, p_mv = [], [], [], [], [], []
    s_lru_h, s_lru_conv, s_ssd, s_ssd_conv = [], [], [], []
    hist = CONV_W - 1
    for i in range(DEPTH):
        j = i // 2
        mk, mv = norm_matmul(mem, norm_memkv[i], [bf(mem_w_k[i]), bf(mem_w_v[i])], [F32, F32])
        p_mk.append(mk.reshape(BATCH, N_MEM, MEM_HEADS, MEM_HD))
        p_mv.append(mv.reshape(BATCH, N_MEM, MEM_HEADS, MEM_HD))

        if i % 2 == 0:
            p = dict(conv_w=lru_conv_w[j], conv_b=_row(lru_conv_b[j]), w_a=bf(lru_w_a[j]), b_a=_row(lru_b_a[j]),
                     w_x=bf(lru_w_x[j]), b_x=_row(lru_b_x[j]), lam=_row(lru_lam[j]))
            (proj,) = norm_matmul(x, norm_mix[i], [bf(lru_w_in[j])], [F32])
            y, h_p, c_p = lru_prompt(proj, p)
            y, h_s, c_s = lru_sample(proj, y, _to_time_major(state_lru_conv[j]), state_lru_h[j], p)
            p_lru_h.append(h_p.reshape(BATCH, D_RNN))
            p_lru_conv.append(c_p[:, HALO - hist :, :])
            s_lru_h.append(h_s)
            s_lru_conv.append(_to_batch_major(c_s, hist))
            x = matmul_residual(y, bf(lru_w_out[j]), x)
        else:
            w_in = ssd_w_in[j]
            w_z = bf(w_in[:, :D_INNER])
            w_xbc = bf(w_in[:, D_INNER : D_INNER + SSD_CONV_DIM])
            w_dt = bf(jnp.pad(w_in[:, D_INNER + SSD_CONV_DIM :], ((0, 0), (0, LANES - SSD_HEADS))))
            p = dict(conv_w=ssd_conv_w[j], conv_b=_row(ssd_conv_b[j]), dt_bias=_pad_lanes(ssd_dt_bias[j]),
                     a_log=_pad_lanes(ssd_a_log[j]), d_skip=_row(jnp.repeat(ssd_d[j], SSD_HEADDIM)),
                     norm_g=_row(ssd_norm_g[j]), expand=expand)
            z, xbc, dt = norm_matmul(x, norm_mix[i], [w_z, w_xbc, w_dt], [F32, F32, F32], tm=256)
            y, st_p, c_p = ssd_prompt(z, xbc, dt, p)
            xs_s, xdt_s, dec_s, bm_s, cm_s, c_s = ssd_sample_pre(xbc, dt, _to_time_major(state_ssd_conv[j]), p)
            y_raw, st_s = ssd_recur(
                _to_batch_major(xdt_s, DEC_SEQ), _to_batch_major(dec_s, DEC_SEQ),
                _to_batch_major(bm_s, DEC_SEQ), _to_batch_major(cm_s, DEC_SEQ),
                state_ssd[j].reshape(DEC_BATCH, D_INNER, SSD_STATE))
            y = ssd_sample_post(_to_time_major(y_raw), xs_s, z, y, p)
            p_ssd.append(st_p.reshape(BATCH, SSD_HEADS, SSD_HEADDIM, SSD_STATE))
            p_ssd_conv.append(c_p[:, HALO - hist :, :])
            s_ssd.append(st_s.reshape(DEC_BATCH, SSD_HEADS, SSD_HEADDIM, SSD_STATE))
            s_ssd_conv.append(_to_batch_major(c_s, hist))
            x = matmul_residual(y, bf(ssd_w_out[j]), x)

        (qp,) = norm_matmul(x, norm_mem[i], [bf(mem_w_q[i])], [BF16])
        o = attn_prompt(qp, mk.reshape(BATCH, N_MEM, D_MODEL), mv.reshape(BATCH, N_MEM, D_MODEL))
        o_s = attn_sample(
            _to_batch_major(qp[N_PROMPT:], DEC_SEQ),
            cache_mem_k[i].reshape(DEC_BATCH, N_MEM, D_MODEL), cache_mem_v[i].reshape(DEC_BATCH, N_MEM, D_MODEL))
        o = lax.dynamic_update_slice(o, _to_time_major(o_s).astype(BF16), (N_PROMPT, 0))
        x = matmul_residual(o, bf(mem_w_o[i]), x)

        if i % 2 == 0:
            x = ffn(x, norm_ffn[i], bf(ffn_w1[j]), bf(ffn_w3[j]), bf(ffn_w2[j]), tf=D_FF // 2)
        else:
            h, gate = router(x, norm_ffn[i], moe_router[j])
            for ex in range(N_EXPERTS):
                x = expert_accumulate(h, gate, x, bf(moe_w1[j, ex]), bf(moe_w3[j, ex]), bf(moe_w2[j, ex]), ex,
                                      tf=D_FF_EXPERT // 2)

    y_prompt = rmsnorm_rows(x, norm_final, 0, N_PROMPT // ROW_TILE).reshape(BATCH, SEQ, D_MODEL)
    y_sample = _to_batch_major(rmsnorm_rows(x, norm_final, N_PROMPT // ROW_TILE, N_SAMPLE // ROW_TILE), DEC_SEQ)
    return (y_prompt, y_sample,
            jnp.stack(p_lru_h), jnp.stack(p_lru_conv), jnp.stack(p_ssd), jnp.stack(p_ssd_conv),
            jnp.stack(p_mk), jnp.stack(p_mv),
            jnp.stack(s_lru_h), jnp.stack(s_lru_conv), jnp.stack(s_ssd), jnp.stack(s_ssd_conv))
```

```python
import functools
import math

import jax
import jax.numpy as jnp
from jax import lax
from jax.experimental import pallas as pl
from jax.experimental.pallas import tpu as pltpu

F32 = jnp.float32
BF16 = jnp.bfloat16

D_MODEL = 1024
BATCH = 8
SEQ = 2048
DEPTH = 4
DEC_BATCH = 128
DEC_SEQ = 4
CONV_W = 4
EPS = 1e-6
D_RNN = D_MODEL
LRU_BLOCKS = 8
LRU_BW = D_RNN // LRU_BLOCKS
LRU_C = 8.0
D_INNER = 2 * D_MODEL
SSD_HEADDIM = 64
SSD_HEADS = D_INNER // SSD_HEADDIM
SSD_GROUPS = 4
SSD_GROUP_W = D_INNER // SSD_GROUPS
SSD_STATE = 128
SSD_GN = SSD_GROUPS * SSD_STATE
SSD_CONV_DIM = D_INNER + 2 * SSD_GN
SSD_CHUNK = 128
N_MEM = 256
MEM_HEADS = 4
MEM_HD = D_MODEL // MEM_HEADS
D_FF = 2816
N_EXPERTS = 8
TOP_K = 2
D_FF_EXPERT = 3584

LANES = 128
SUBLANES = 8
VMEM_LIMIT_BYTES = 56 * 1024 * 1024

N_PROMPT = BATCH * SEQ
N_SAMPLE = DEC_BATCH * DEC_SEQ
N_ROWS = N_PROMPT + N_SAMPLE
ROW_TILE = 512
SAMPLE_BLOCK = N_PROMPT // N_SAMPLE
HALO = SUBLANES
MOE_TILE = 1024
MOE_FF_TILE = 512


def _cparams(*sem):
    return pltpu.CompilerParams(dimension_semantics=sem, vmem_limit_bytes=VMEM_LIMIT_BYTES)


def _rms(x, g):
    return x * lax.rsqrt(jnp.mean(x * x, axis=-1, keepdims=True) + EPS) * g


def _sigmoid(x):
    return 1.0 / (1.0 + jnp.exp(-x))


def _softplus(x):
    return jnp.maximum(x, 0.0) + jnp.log1p(jnp.exp(-jnp.abs(x)))


def _split3(x):
    a = x.astype(BF16)
    r = x - a.astype(F32)
    b = r.astype(BF16)
    c = (r - b.astype(F32)).astype(BF16)
    return a, b, c


def _dot(a, b):
    return jnp.dot(a, b, preferred_element_type=F32)


def _dot_f32_lhs(x, m):
    a, b, c = _split3(x)
    return _dot(a, m) + _dot(b, m) + _dot(c, m)


def _dot_f32_rhs(m, x):
    a, b, c = _split3(x)
    return _dot(m, a) + _dot(m, b) + _dot(m, c)


def _norm_matmul_kernel(n_w, x_ref, g_ref, *refs):
    h = _rms(x_ref[...], g_ref[...]).astype(BF16)
    for w_ref, o_ref in zip(refs[:n_w], refs[n_w:]):
        o_ref[...] = _dot(h, w_ref[...]).astype(o_ref.dtype)


def norm_matmul(x, g, ws, out_dtypes, tm=ROW_TILE):
    rows, k = x.shape
    in_specs = [pl.BlockSpec((tm, k), lambda i: (i, 0)), pl.BlockSpec((1, k), lambda i: (0, 0))]
    in_specs += [pl.BlockSpec(w.shape, lambda i: (0, 0)) for w in ws]
    out_specs = [pl.BlockSpec((tm, w.shape[1]), lambda i: (i, 0)) for w in ws]
    out_shape = [jax.ShapeDtypeStruct((rows, w.shape[1]), dt) for w, dt in zip(ws, out_dtypes)]
    return pl.pallas_call(
        functools.partial(_norm_matmul_kernel, len(ws)),
        grid=(rows // tm,),
        in_specs=in_specs,
        out_specs=out_specs,
        out_shape=out_shape,
        compiler_params=_cparams("parallel"),
        name="norm_matmul",
    )(x, g.reshape(1, k), *ws)


def _matmul_residual_kernel(n_p, yp_ref, ys_ref, w_ref, r_ref, o_ref):
    i = pl.program_id(0)

    @pl.when(i < n_p)
    def _():
        o_ref[...] = r_ref[...] + _dot(yp_ref[...].astype(BF16), w_ref[...])

    @pl.when(i >= n_p)
    def _():
        o_ref[...] = r_ref[...] + _dot(ys_ref[...].astype(BF16), w_ref[...])


def matmul_residual(y_prompt, y_sample, w, res, tm=ROW_TILE):
    k = y_prompt.shape[1]
    n = w.shape[1]
    n_p = y_prompt.shape[0] // tm
    n_s = y_sample.shape[0] // tm
    return pl.pallas_call(
        functools.partial(_matmul_residual_kernel, n_p),
        grid=(n_p + n_s,),
        in_specs=[
            pl.BlockSpec((tm, k), lambda i: (jnp.minimum(i, n_p - 1), 0)),
            pl.BlockSpec((tm, k), lambda i: (jnp.maximum(i - n_p, 0), 0)),
            pl.BlockSpec((k, n), lambda i: (0, 0)),
            pl.BlockSpec((tm, n), lambda i: (i, 0)),
        ],
        out_specs=pl.BlockSpec((tm, n), lambda i: (i, 0)),
        out_shape=jax.ShapeDtypeStruct(res.shape, F32),
        compiler_params=_cparams("arbitrary"),
        name="matmul_residual",
    )(y_prompt, y_sample, w, res)


def _rmsnorm_kernel(x_ref, g_ref, o_ref):
    o_ref[...] = _rms(x_ref[...], g_ref[...])


def rmsnorm_rows(x, g, first_block, n_blocks, tm=ROW_TILE):
    k = x.shape[1]
    return pl.pallas_call(
        _rmsnorm_kernel,
        grid=(n_blocks,),
        in_specs=[pl.BlockSpec((tm, k), lambda i: (i + first_block, 0)), pl.BlockSpec((1, k), lambda i: (0, 0))],
        out_specs=pl.BlockSpec((tm, k), lambda i: (i, 0)),
        out_shape=jax.ShapeDtypeStruct((n_blocks * tm, k), F32),
        compiler_params=_cparams("parallel"),
        name="final_norm",
    )(x, g.reshape(1, k))


def _swiglu_partial(h, w1_ref, w3_ref, w2_ref):
    a = _dot(h, w1_ref[...])
    b = _dot(h, w3_ref[...])
    return _dot((a * _sigmoid(a) * b).astype(BF16), w2_ref[...])


def _ffn_kernel(x_ref, g_ref, w1_ref, w3_ref, w2_ref, o_ref, h_scr, acc_scr):
    j = pl.program_id(1)

    @pl.when(j == 0)
    def _():
        h_scr[...] = _rms(x_ref[...], g_ref[...]).astype(BF16)
        acc_scr[...] = jnp.zeros_like(acc_scr)

    acc_scr[...] += _swiglu_partial(h_scr[...], w1_ref, w3_ref, w2_ref)

    @pl.when(j == pl.num_programs(1) - 1)
    def _():
        o_ref[...] = x_ref[...] + acc_scr[...]


def ffn(x, g, w1, w3, w2, tf, tm=ROW_TILE):
    rows, k = x.shape
    f = w1.shape[1]
    return pl.pallas_call(
        _ffn_kernel,
        grid=(rows // tm, f // tf),
        in_specs=[
            pl.BlockSpec((tm, k), lambda i, j: (i, 0)),
            pl.BlockSpec((1, k), lambda i, j: (0, 0)),
            pl.BlockSpec((k, tf), lambda i, j: (0, j)),
            pl.BlockSpec((k, tf), lambda i, j: (0, j)),
            pl.BlockSpec((tf, k), lambda i, j: (j, 0)),
        ],
        out_specs=pl.BlockSpec((tm, k), lambda i, j: (i, 0)),
        out_shape=jax.ShapeDtypeStruct((rows, k), F32),
        scratch_shapes=[pltpu.VMEM((tm, k), BF16), pltpu.VMEM((tm, k), F32)],
        compiler_params=_cparams("parallel", "arbitrary"),
        name="ffn",
    )(x, g.reshape(1, k), w1, w3, w2)


def _router_kernel(tm, x_ref, g_ref, wr_ref, gate_ref, rank_ref, exp_ref, cnt_ref, carry):
    @pl.when(pl.program_id(0) == 0)
    def _():
        carry[...] = jnp.zeros_like(carry)

    h = _rms(x_ref[...], g_ref[...])
    h1 = h.astype(BF16)
    h2 = (h - h1.astype(F32)).astype(BF16)
    w = wr_ref[...]
    w1 = w.astype(BF16)
    w2 = (w - w1.astype(F32)).astype(BF16)
    logits = _dot(h1, w1) + _dot(h1, w2) + _dot(h2, w1)
    lane = lax.broadcasted_iota(jnp.int32, logits.shape, 1).astype(F32)
    neg = jnp.float32(-jnp.inf)
    logits = jnp.where(lane < N_EXPERTS, logits, neg)
    m1 = jnp.max(logits, axis=-1, keepdims=True)
    i1 = jnp.min(jnp.where(logits == m1, lane, float(LANES)), axis=-1, keepdims=True)
    rest = jnp.where(lane == i1, neg, logits)
    m2 = jnp.max(rest, axis=-1, keepdims=True)
    i2 = jnp.min(jnp.where(rest == m2, lane, float(LANES)), axis=-1, keepdims=True)
    e2 = jnp.exp(m2 - m1)
    den = 1.0 + e2
    gate_ref[...] = jnp.where(lane == 0.0, 1.0 / den, 0.0) + jnp.where(lane == 1.0, e2 / den, 0.0)

    oh1_t = jnp.where(lane == i1, 1.0, 0.0).T
    oh2_t = jnp.where(lane == i2, 1.0, 0.0).T
    oh_t = oh1_t + oh2_t
    src = lax.broadcasted_iota(jnp.int32, (tm, tm), 0)
    dst = lax.broadcasted_iota(jnp.int32, (tm, tm), 1)
    earlier = jnp.where(src < dst, 1.0, 0.0).astype(BF16)
    before = _dot(oh_t.astype(BF16), earlier)
    base = jnp.concatenate([carry[...]] * (tm // LANES), axis=1) + before
    expert_id = lax.broadcasted_iota(jnp.int32, (LANES, tm), 0).astype(F32)
    col_sum = lambda a: jnp.sum(a, axis=0, keepdims=True)
    rank_ref[...] = jnp.concatenate([col_sum(oh1_t * base), col_sum(oh2_t * base)], axis=1).astype(jnp.int32)
    exp_ref[...] = jnp.concatenate([col_sum(oh1_t * expert_id), col_sum(oh2_t * expert_id)], axis=1).astype(jnp.int32)
    carry[...] += jnp.broadcast_to(jnp.sum(oh_t, axis=1, keepdims=True), carry.shape)
    cnt_ref[...] = carry[...]


def router(x, g, w_router, tm=ROW_TILE):
    rows, k = x.shape
    n_t = rows // tm
    wr = jnp.pad(w_router, ((0, 0), (0, LANES - N_EXPERTS)))
    return pl.pallas_call(
        functools.partial(_router_kernel, tm),
        grid=(n_t,),
        in_specs=[
            pl.BlockSpec((tm, k), lambda i: (i, 0)),
            pl.BlockSpec((1, k), lambda i: (0, 0)),
            pl.BlockSpec((k, LANES), lambda i: (0, 0)),
        ],
        out_specs=[
            pl.BlockSpec((tm, LANES), lambda i: (i, 0)),
            pl.BlockSpec((None, 1, TOP_K * tm), lambda i: (i, 0, 0)),
            pl.BlockSpec((None, 1, TOP_K * tm), lambda i: (i, 0, 0)),
            pl.BlockSpec((LANES, LANES), lambda i: (0, 0)),
        ],
        out_shape=[
            jax.ShapeDtypeStruct((rows, LANES), F32),
            jax.ShapeDtypeStruct((n_t, 1, TOP_K * tm), jnp.int32),
            jax.ShapeDtypeStruct((n_t, 1, TOP_K * tm), jnp.int32),
            jax.ShapeDtypeStruct((LANES, LANES), F32),
        ],
        scratch_shapes=[pltpu.VMEM((LANES, LANES), F32)],
        compiler_params=_cparams("arbitrary"),
        name="router",
    )(x, g.reshape(1, k), wr)


def _row_copy(src_ref, src_row, dst_ref, dst_row, sem):
    return pltpu.make_async_copy(src_ref.at[pl.ds(src_row, 1)], dst_ref.at[pl.ds(dst_row, 1)], sem)


def _dispatch_kernel(tm, pos_ref, x_ref, xs_in_ref, xs_ref, sem):
    del xs_in_ref

    def start(r, c):
        for choice in range(TOP_K):
            _row_copy(x_ref, r, xs_ref, pos_ref[0, choice * tm + r], sem).start()
        return c

    lax.fori_loop(0, tm, start, 0, unroll=8)
    for _ in range(TOP_K):
        pltpu.make_async_copy(x_ref, xs_ref.at[pl.ds(0, tm)], sem).wait()


def dispatch(x, pos, xs_zero, tm=ROW_TILE):
    rows, k = x.shape
    return pl.pallas_call(
        functools.partial(_dispatch_kernel, tm),
        grid=(rows // tm,),
        in_specs=[
            pl.BlockSpec((None, 1, TOP_K * tm), lambda i: (i, 0, 0), memory_space=pltpu.SMEM),
            pl.BlockSpec((tm, k), lambda i: (i, 0)),
            pl.BlockSpec(memory_space=pl.ANY),
        ],
        out_specs=pl.BlockSpec(memory_space=pl.ANY),
        out_shape=jax.ShapeDtypeStruct(xs_zero.shape, F32),
        scratch_shapes=[pltpu.SemaphoreType.DMA(())],
        input_output_aliases={2: 0},
        compiler_params=_cparams("arbitrary"),
        name="moe_dispatch",
    )(pos, x, xs_zero)


def _moe_kernel(texp_ref, tvalid_ref, x_ref, g_ref, w1_ref, w3_ref, w2_ref, o_ref, h_scr, acc_scr):
    del texp_ref
    j = pl.program_id(1)
    last = pl.num_programs(1) - 1
    valid = tvalid_ref[pl.program_id(0)]

    @pl.when(valid > 0)
    def _():
        @pl.when(j == 0)
        def _():
            h_scr[...] = _rms(x_ref[...], g_ref[...]).astype(BF16)
            acc_scr[...] = jnp.zeros_like(acc_scr)

        h = h_scr[...]
        a = _dot(h, w1_ref[...].astype(BF16))
        b = _dot(h, w3_ref[...].astype(BF16))
        acc_scr[...] += _dot((a * _sigmoid(a) * b).astype(BF16), w2_ref[...].astype(BF16))

        @pl.when(j == last)
        def _():
            o_ref[...] = acc_scr[...]

    @pl.when(jnp.logical_and(valid == 0, j == last))
    def _():
        o_ref[...] = jnp.zeros_like(o_ref)


def moe_experts(xs, g, w1, w3, w2, layer, tile_expert, tile_valid, tm, tf):
    rows, k = xs.shape
    f = w1.shape[-1]
    n_f = f // tf

    def jf(i, j, tvalid):
        return jnp.where(tvalid[i] > 0, j, n_f - 1)

    grid_spec = pltpu.PrefetchScalarGridSpec(
        num_scalar_prefetch=2,
        grid=(rows // tm, n_f),
        in_specs=[
            pl.BlockSpec((tm, k), lambda i, j, texp, tvalid: (i, 0)),
            pl.BlockSpec((1, k), lambda i, j, texp, tvalid: (0, 0)),
            pl.BlockSpec((None, None, k, tf), lambda i, j, texp, tvalid: (layer, texp[i], 0, jf(i, j, tvalid))),
            pl.BlockSpec((None, None, k, tf), lambda i, j, texp, tvalid: (layer, texp[i], 0, jf(i, j, tvalid))),
            pl.BlockSpec((None, None, tf, k), lambda i, j, texp, tvalid: (layer, texp[i], jf(i, j, tvalid), 0)),
        ],
        out_specs=pl.BlockSpec((tm, k), lambda i, j, texp, tvalid: (i, 0)),
        scratch_shapes=[pltpu.VMEM((tm, k), BF16), pltpu.VMEM((tm, k), F32)],
    )
    return pl.pallas_call(
        _moe_kernel,
        grid_spec=grid_spec,
        out_shape=jax.ShapeDtypeStruct((rows, k), F32),
        compiler_params=_cparams("arbitrary", "arbitrary"),
        name="moe_experts",
    )(tile_expert, tile_valid, xs, g.reshape(1, k), w1, w3, w2)


def _combine_kernel(tm, pos_ref, x_ref, gate_ref, ys_ref, o_ref, y1_scr, y2_scr, sem):
    bufs = (y1_scr, y2_scr)

    def start(r, c):
        for choice in range(TOP_K):
            _row_copy(ys_ref, pos_ref[0, choice * tm + r], bufs[choice], r, sem).start()
        return c

    lax.fori_loop(0, tm, start, 0, unroll=8)
    for choice in range(TOP_K):
        pltpu.make_async_copy(ys_ref.at[pl.ds(0, tm)], bufs[choice], sem).wait()
    g = gate_ref[...]
    o_ref[...] = x_ref[...] + (g[:, 0:1] * y1_scr[...] + g[:, 1:2] * y2_scr[...])


def combine(x, gate, pos, ys, tm=ROW_TILE):
    rows, k = x.shape
    return pl.pallas_call(
        functools.partial(_combine_kernel, tm),
        grid=(rows // tm,),
        in_specs=[
            pl.BlockSpec((None, 1, TOP_K * tm), lambda i: (i, 0, 0), memory_space=pltpu.SMEM),
            pl.BlockSpec((tm, k), lambda i: (i, 0)),
            pl.BlockSpec((tm, LANES), lambda i: (i, 0)),
            pl.BlockSpec(memory_space=pl.ANY),
        ],
        out_specs=pl.BlockSpec((tm, k), lambda i: (i, 0)),
        out_shape=jax.ShapeDtypeStruct((rows, k), F32),
        scratch_shapes=[pltpu.VMEM((tm, k), F32), pltpu.VMEM((tm, k), F32), pltpu.SemaphoreType.DMA(())],
        compiler_params=_cparams("arbitrary"),
        name="moe_combine",
    )(pos, x, gate, ys)


def moe(x, g, w_router, w1, w3, w2, layer, tm_e=MOE_TILE, tf=MOE_FF_TILE):
    rows, k = x.shape
    n_tiles = -(-(TOP_K * rows + N_EXPERTS * (tm_e - 1)) // tm_e)
    gate, rank, expert, cnt = router(x, g, w_router)
    counts = cnt[:N_EXPERTS, 0].astype(jnp.int32)
    padded = ((counts + tm_e - 1) // tm_e) * tm_e
    ends = jnp.cumsum(padded)
    offs = ends - padded
    pos = rank
    for e in range(N_EXPERTS):
        pos = pos + jnp.where(expert == e, offs[e], 0)
    tile_start = jnp.arange(n_tiles, dtype=jnp.int32) * tm_e
    tile_expert = jnp.minimum(jnp.sum(ends[None, :] <= tile_start[:, None], axis=1), N_EXPERTS - 1).astype(jnp.int32)
    tile_valid = jnp.clip(counts[tile_expert] - (tile_start - offs[tile_expert]), 0, tm_e).astype(jnp.int32)
    xs = dispatch(x, pos, jnp.zeros((n_tiles * tm_e, k), F32))
    ys = moe_experts(xs, g, w1, w3, w2, layer, tile_expert, tile_valid, tm_e, tf)
    return combine(x, gate, pos, ys)


def _conv_taps(xpad, base, rows, step, cw, cb):
    y = xpad[base - 3 * step : base - 3 * step + rows, :] * cw[0:1]
    for k in range(1, CONV_W):
        lo = base - (CONV_W - 1 - k) * step
        y = y + xpad[lo : lo + rows, :] * cw[k : k + 1]
    return y + cb


def _lru_gates(xc, wa_ref, ba_ref, wx_ref, bx_ref, lam_ref):
    xcb = xc.astype(BF16)
    r_parts, i_parts = [], []
    for k in range(LRU_BLOCKS):
        blk = xcb[:, k * LRU_BW : (k + 1) * LRU_BW]
        r_parts.append(_dot(blk, wa_ref[k]))
        i_parts.append(_dot(blk, wx_ref[k]))
    r = _sigmoid(jnp.concatenate(r_parts, axis=1) + ba_ref[...])
    ig = _sigmoid(jnp.concatenate(i_parts, axis=1) + bx_ref[...])
    log_a = (-LRU_C * r) * _softplus(-lam_ref[...])
    a = jnp.exp(log_a)
    one_minus_a2 = -jnp.tanh(log_a) * (a * a + 1.0)
    u = jnp.sqrt(one_minus_a2) * (ig * xc)
    return a, u


def _lru_prompt_kernel(tt, proj_ref, cw_ref, cb_ref, wa_ref, ba_ref, wx_ref, bx_ref, lam_ref,
                       y_ref, hlast_ref, convn_ref, xpad, a_scr, u_scr, hs_scr, h_scr):
    t = pl.program_id(1)

    @pl.when(t == 0)
    def _():
        xpad[0:HALO, :] = jnp.zeros((HALO, D_RNN), F32)
        h_scr[...] = jnp.zeros_like(h_scr)

    @pl.when(t > 0)
    def _():
        xpad[0:HALO, :] = xpad[tt : tt + HALO, :]

    xpad[HALO : HALO + tt, :] = proj_ref[:, D_RNN:]
    xc = _conv_taps(xpad, HALO, tt, 1, cw_ref[...], cb_ref[...])
    a, u = _lru_gates(xc, wa_ref, ba_ref, wx_ref, bx_ref, lam_ref)
    a_scr[...] = a
    u_scr[...] = u

    def body(i, h):
        h = a_scr[pl.ds(i, 1), :] * h + u_scr[pl.ds(i, 1), :]
        hs_scr[pl.ds(i, 1), :] = h
        return h

    h_scr[...] = lax.fori_loop(0, tt, body, h_scr[...], unroll=8)
    y_ref[...] = hs_scr[...] * jax.nn.gelu(proj_ref[:, :D_RNN])

    @pl.when(t == pl.num_programs(1) - 1)
    def _():
        hlast_ref[...] = h_scr[...]
        convn_ref[...] = xpad[tt : tt + HALO, :]


def lru_prompt(proj, p, tt=256):
    n_t = SEQ // tt
    wspec = lambda shape: pl.BlockSpec(shape, lambda b, t: (0,) * len(shape))
    return pl.pallas_call(
        functools.partial(_lru_prompt_kernel, tt),
        grid=(BATCH, n_t),
        in_specs=[
            pl.BlockSpec((tt, 2 * D_RNN), lambda b, t: (b * n_t + t, 0)),
            wspec((CONV_W, D_RNN)), wspec((1, D_RNN)),
            wspec((LRU_BLOCKS, LRU_BW, LRU_BW)), wspec((1, D_RNN)),
            wspec((LRU_BLOCKS, LRU_BW, LRU_BW)), wspec((1, D_RNN)),
            wspec((1, D_RNN)),
        ],
        out_specs=[
            pl.BlockSpec((tt, D_RNN), lambda b, t: (b * n_t + t, 0)),
            pl.BlockSpec((None, 1, D_RNN), lambda b, t: (b, 0, 0)),
            pl.BlockSpec((None, HALO, D_RNN), lambda b, t: (b, 0, 0)),
        ],
        out_shape=[
            jax.ShapeDtypeStruct((N_PROMPT, D_RNN), F32),
            jax.ShapeDtypeStruct((BATCH, 1, D_RNN), F32),
            jax.ShapeDtypeStruct((BATCH, HALO, D_RNN), F32),
        ],
        scratch_shapes=[
            pltpu.VMEM((HALO + tt, D_RNN), F32),
            pltpu.VMEM((tt, D_RNN), F32),
            pltpu.VMEM((tt, D_RNN), F32),
            pltpu.VMEM((tt, D_RNN), F32),
            pltpu.VMEM((1, D_RNN), F32),
        ],
        compiler_params=_cparams("parallel", "arbitrary"),
        name="lru_prompt",
    )(proj, p["conv_w"], p["conv_b"], p["w_a"], p["b_a"], p["w_x"], p["b_x"], p["lam"])


def _lru_sample_kernel(proj_ref, convp_ref, hprev_ref, cw_ref, cb_ref, wa_ref, ba_ref, wx_ref, bx_ref, lam_ref,
                       y_ref, hlast_ref, convn_ref, xpad):
    hist = (CONV_W - 1) * DEC_BATCH
    xpad[0:hist, :] = convp_ref[...]
    xpad[hist:, :] = proj_ref[:, D_RNN:]
    xc = _conv_taps(xpad, hist, N_SAMPLE, DEC_BATCH, cw_ref[...], cb_ref[...])
    a, u = _lru_gates(xc, wa_ref, ba_ref, wx_ref, bx_ref, lam_ref)
    gate = jax.nn.gelu(proj_ref[:, :D_RNN])
    h = hprev_ref[...]
    for t in range(DEC_SEQ):
        rows = slice(t * DEC_BATCH, (t + 1) * DEC_BATCH)
        h = a[rows] * h + u[rows]
        y_ref[rows, :] = h * gate[rows]
    hlast_ref[...] = h
    convn_ref[...] = xpad[N_SAMPLE:, :]


def lru_sample(proj, conv_prev, h_prev, p):
    hist = (CONV_W - 1) * DEC_BATCH
    full = lambda shape: pl.BlockSpec(shape, lambda i: (0,) * len(shape))
    return pl.pallas_call(
        _lru_sample_kernel,
        grid=(1,),
        in_specs=[
            pl.BlockSpec((N_SAMPLE, 2 * D_RNN), lambda i: (SAMPLE_BLOCK, 0)),
            full((hist, D_RNN)), full((DEC_BATCH, D_RNN)),
            full((CONV_W, D_RNN)), full((1, D_RNN)),
            full((LRU_BLOCKS, LRU_BW, LRU_BW)), full((1, D_RNN)),
            full((LRU_BLOCKS, LRU_BW, LRU_BW)), full((1, D_RNN)),
            full((1, D_RNN)),
        ],
        out_specs=[
            full((N_SAMPLE, D_RNN)),
            full((DEC_BATCH, D_RNN)),
            full((hist, D_RNN)),
        ],
        out_shape=[
            jax.ShapeDtypeStruct((N_SAMPLE, D_RNN), F32),
            jax.ShapeDtypeStruct((DEC_BATCH, D_RNN), F32),
            jax.ShapeDtypeStruct((hist, D_RNN), F32),
        ],
        scratch_shapes=[pltpu.VMEM((hist + N_SAMPLE, D_RNN), F32)],
        compiler_params=_cparams("arbitrary"),
        name="lru_sample",
    )(proj, conv_prev, h_prev, p["conv_w"], p["conv_b"], p["w_a"], p["b_a"], p["w_x"], p["b_x"], p["lam"])


def _attend(q, k, v):
    outs = []
    for h in range(MEM_HEADS):
        hs = slice(h * MEM_HD, (h + 1) * MEM_HD)
        s = lax.dot_general(q[:, hs], k[:, hs], (((1,), (1,)), ((), ())), preferred_element_type=F32)
        s = s * (MEM_HD ** -0.5)
        e = jnp.exp(s - jnp.max(s, axis=-1, keepdims=True))
        p = e / jnp.sum(e, axis=-1, keepdims=True)
        outs.append(_dot(p.astype(BF16), v[:, hs]))
    return jnp.concatenate(outs, axis=1)


def _attn_prompt_kernel(q_ref, k_ref, v_ref, o_ref):
    o_ref[...] = _attend(q_ref[...], k_ref[...].astype(BF16), v_ref[...].astype(BF16)).astype(o_ref.dtype)


def attn_prompt(q, k, v, tt=512):
    n_t = SEQ // tt
    return pl.pallas_call(
        _attn_prompt_kernel,
        grid=(BATCH, n_t),
        in_specs=[
            pl.BlockSpec((tt, D_MODEL), lambda b, t: (b * n_t + t, 0)),
            pl.BlockSpec((None, N_MEM, D_MODEL), lambda b, t: (b, 0, 0)),
            pl.BlockSpec((None, N_MEM, D_MODEL), lambda b, t: (b, 0, 0)),
        ],
        out_specs=pl.BlockSpec((tt, D_MODEL), lambda b, t: (b * n_t + t, 0)),
        out_shape=jax.ShapeDtypeStruct((N_PROMPT, D_MODEL), BF16),
        compiler_params=_cparams("parallel", "arbitrary"),
        name="attn_prompt",
    )(q, k, v)


def _attn_sample_kernel(bb, q_ref, k_ref, v_ref, o_ref):
    for i in range(bb):
        o_ref[i] = _attend(q_ref[i], k_ref[i].astype(BF16), v_ref[i].astype(BF16)).astype(o_ref.dtype)


def attn_sample(q, k, v, layer, bb=4):
    return pl.pallas_call(
        functools.partial(_attn_sample_kernel, bb),
        grid=(DEC_BATCH // bb,),
        in_specs=[
            pl.BlockSpec((bb, DEC_SEQ, D_MODEL), lambda i: (i, 0, 0)),
            pl.BlockSpec((None, bb, N_MEM, D_MODEL), lambda i: (layer, i, 0, 0)),
            pl.BlockSpec((None, bb, N_MEM, D_MODEL), lambda i: (layer, i, 0, 0)),
        ],
        out_specs=pl.BlockSpec((bb, DEC_SEQ, D_MODEL), lambda i: (i, 0, 0)),
        out_shape=jax.ShapeDtypeStruct((DEC_BATCH, DEC_SEQ, D_MODEL), F32),
        compiler_params=_cparams("parallel"),
        name="attn_sample",
    )(q, k, v)


def _ssd_gate_norm(y, z, ng):
    y = y * (z * _sigmoid(z))
    outs = []
    for g in range(SSD_GROUPS):
        yg = y[:, g * SSD_GROUP_W : (g + 1) * SSD_GROUP_W]
        outs.append(yg * lax.rsqrt(jnp.mean(yg * yg, axis=-1, keepdims=True) + EPS))
    return jnp.concatenate(outs, axis=1) * ng


def _ssd_prompt_kernel(q, z_ref, xbc_ref, dt_ref, cw_ref, cb_ref, dtb_ref, alog_ref, dskip_ref, ng_ref, e_ref,
                       y_ref, st_ref, convn_ref, xpad, s_t, y_scr):
    t = pl.program_id(1)

    @pl.when(t == 0)
    def _():
        xpad[0:HALO, :] = jnp.zeros((HALO, SSD_CONV_DIM), F32)
        s_t[...] = jnp.zeros_like(s_t)

    @pl.when(t > 0)
    def _():
        xpad[0:HALO, :] = xpad[q : q + HALO, :]

    xpad[HALO : HALO + q, :] = xbc_ref[...]
    xc = _conv_taps(xpad, HALO, q, 1, cw_ref[...], cb_ref[...])
    xbc = xc * _sigmoid(xc)
    xs = xbc[:, :D_INNER]

    dt = _softplus(dt_ref[...] + dtb_ref[...])
    adt = -jnp.exp(alog_ref[...]) * dt
    row_i = lax.broadcasted_iota(jnp.int32, (q, q), 0)
    col_i = lax.broadcasted_iota(jnp.int32, (q, q), 1)
    tril = row_i >= col_i
    a_cs = _dot_f32_rhs(jnp.where(tril, 1.0, 0.0).astype(BF16), adt)
    a_cs_t = a_cs.T
    a_end = a_cs[q - 1 : q, :]
    expand = e_ref[...]
    dt_x = _dot_f32_lhs(dt, expand)
    ecs_x = _dot_f32_lhs(jnp.exp(a_cs), expand)
    dst_x = _dot_f32_lhs(jnp.exp(a_end - a_cs), expand)
    x_dt = xs * dt_x
    xb = x_dt.astype(BF16)
    xd = (x_dt * dst_x).astype(BF16)

    for g in range(SSD_GROUPS):
        gc = slice(g * SSD_GROUP_W, (g + 1) * SSD_GROUP_W)
        bg = xbc[:, D_INNER + g * SSD_STATE : D_INNER + (g + 1) * SSD_STATE]
        cg = xbc[:, D_INNER + SSD_GN + g * SSD_STATE : D_INNER + SSD_GN + (g + 1) * SSD_STATE].astype(BF16)
        cb_mat = lax.dot_general(cg, bg.astype(BF16), (((1,), (1,)), ((), ())), preferred_element_type=F32)
        sg = s_t[:, gc]
        y_scr[:, gc] = _dot(cg, sg.astype(BF16)) * ecs_x[:, gc]
        s_t[:, gc] = ecs_x[q - 1 : q, gc] * sg + _dot(bg.T.astype(BF16), xd[:, gc])
        for e in range(SSD_HEADS // SSD_GROUPS):
            h = g * (SSD_HEADS // SSD_GROUPS) + e
            hc = slice(h * SSD_HEADDIM, (h + 1) * SSD_HEADDIM)
            seg = a_cs[:, h : h + 1] - a_cs_t[h : h + 1, :]
            decay = jnp.where(tril, jnp.exp(jnp.minimum(seg, 0.0)), 0.0)
            y_scr[:, hc] += _dot((cb_mat * decay).astype(BF16), xb[:, hc])

    y = y_scr[...] + dskip_ref[...] * xs
    y_ref[...] = _ssd_gate_norm(y, z_ref[...], ng_ref[...]).astype(y_ref.dtype)

    @pl.when(t == pl.num_programs(1) - 1)
    def _():
        for j in range(D_INNER // LANES):
            st_ref[j * LANES : (j + 1) * LANES, :] = s_t[:, j * LANES : (j + 1) * LANES].T
        convn_ref[...] = xpad[q : q + HALO, :]


def ssd_prompt(z, xbc, dt, p, q=SSD_CHUNK):
    n_t = SEQ // q
    rows = lambda w: pl.BlockSpec((q, w), lambda b, t: (b * n_t + t, 0))
    wspec = lambda shape: pl.BlockSpec(shape, lambda b, t: (0,) * len(shape))
    return pl.pallas_call(
        functools.partial(_ssd_prompt_kernel, q),
        grid=(BATCH, n_t),
        in_specs=[
            rows(D_INNER), rows(SSD_CONV_DIM), rows(LANES),
            wspec((CONV_W, SSD_CONV_DIM)), wspec((1, SSD_CONV_DIM)),
            wspec((1, LANES)), wspec((1, LANES)), wspec((1, D_INNER)), wspec((1, D_INNER)),
            wspec((LANES, D_INNER)),
        ],
        out_specs=[
            rows(D_INNER),
            pl.BlockSpec((None, D_INNER, SSD_STATE), lambda b, t: (b, 0, 0)),
            pl.BlockSpec((None, HALO, SSD_CONV_DIM), lambda b, t: (b, 0, 0)),
        ],
        out_shape=[
            jax.ShapeDtypeStruct((N_PROMPT, D_INNER), BF16),
            jax.ShapeDtypeStruct((BATCH, D_INNER, SSD_STATE), F32),
            jax.ShapeDtypeStruct((BATCH, HALO, SSD_CONV_DIM), F32),
        ],
        scratch_shapes=[
            pltpu.VMEM((HALO + q, SSD_CONV_DIM), F32),
            pltpu.VMEM((SSD_STATE, D_INNER), F32),
            pltpu.VMEM((q, D_INNER), F32),
        ],
        compiler_params=_cparams("parallel", "arbitrary"),
        name="ssd_prompt",
    )(z, xbc, dt, p["conv_w"], p["conv_b"], p["dt_bias"], p["a_log"], p["d_skip"], p["norm_g"], p["expand"])


def _ssd_sample_pre_kernel(xbc_ref, dt_ref, convp_ref, cw_ref, cb_ref, dtb_ref, alog_ref, e_ref,
                           xs_ref, xdt_ref, dec_ref, bm_ref, cm_ref, convn_ref, xpad):
    hist = (CONV_W - 1) * DEC_BATCH
    xpad[0:hist, :] = convp_ref[...]
    xpad[hist:, :] = xbc_ref[...]
    xc = _conv_taps(xpad, hist, N_SAMPLE, DEC_BATCH, cw_ref[...], cb_ref[...])
    xbc = xc * _sigmoid(xc)
    xs = xbc[:, :D_INNER]
    dt = _softplus(dt_ref[...] + dtb_ref[...])
    adt = -jnp.exp(alog_ref[...]) * dt
    expand = e_ref[...]
    xs_ref[...] = xs
    xdt_ref[...] = xs * _dot_f32_lhs(dt, expand)
    dec_ref[...] = _dot_f32_lhs(jnp.exp(adt), expand)
    bm_ref[...] = xbc[:, D_INNER : D_INNER + SSD_GN]
    cm_ref[...] = xbc[:, D_INNER + SSD_GN :]
    convn_ref[...] = xpad[N_SAMPLE:, :]


def ssd_sample_pre(xbc, dt, conv_prev, p):
    hist = (CONV_W - 1) * DEC_BATCH
    full = lambda shape: pl.BlockSpec(shape, lambda i: (0,) * len(shape))
    out_w = [D_INNER, D_INNER, D_INNER, SSD_GN, SSD_GN]
    return pl.pallas_call(
        _ssd_sample_pre_kernel,
        grid=(1,),
        in_specs=[
            pl.BlockSpec((N_SAMPLE, SSD_CONV_DIM), lambda i: (SAMPLE_BLOCK, 0)),
            pl.BlockSpec((N_SAMPLE, LANES), lambda i: (SAMPLE_BLOCK, 0)),
            full((hist, SSD_CONV_DIM)),
            full((CONV_W, SSD_CONV_DIM)), full((1, SSD_CONV_DIM)),
            full((1, LANES)), full((1, LANES)), full((LANES, D_INNER)),
        ],
        out_specs=[full((N_SAMPLE, w)) for w in out_w] + [full((hist, SSD_CONV_DIM))],
        out_shape=[jax.ShapeDtypeStruct((N_SAMPLE, w), F32) for w in out_w]
        + [jax.ShapeDtypeStruct((hist, SSD_CONV_DIM), F32)],
        scratch_shapes=[pltpu.VMEM((hist + N_SAMPLE, SSD_CONV_DIM), F32)],
        compiler_params=_cparams("arbitrary"),
        name="ssd_sample_pre",
    )(xbc, dt, conv_prev, p["conv_w"], p["conv_b"], p["dt_bias"], p["a_log"], p["expand"])


def _ssd_recur_kernel(xdt_ref, dec_ref, bm_ref, cm_ref, st_in_ref, y_ref, st_out_ref):
    xdt = xdt_ref[...]
    dec = dec_ref[...]
    heads_per_block = LANES // SSD_HEADDIM
    for g in range(SSD_GROUPS):
        b_t = bm_ref[:, g * SSD_STATE : (g + 1) * SSD_STATE].T
        c_t = cm_ref[:, g * SSD_STATE : (g + 1) * SSD_STATE].T
        b_cols = [jnp.broadcast_to(b_t[:, t : t + 1], (SSD_STATE, LANES)) for t in range(DEC_SEQ)]
        c_cols = [jnp.broadcast_to(c_t[:, t : t + 1], (SSD_STATE, LANES)) for t in range(DEC_SEQ)]
        for jb in range(SSD_GROUP_W // LANES):
            j = g * (SSD_GROUP_W // LANES) + jb
            cols = slice(j * LANES, (j + 1) * LANES)
            s = st_in_ref[cols, :].T
            for t in range(DEC_SEQ):
                s = dec[t : t + 1, cols] * s + b_cols[t] * xdt[t : t + 1, cols]
                y_ref[t : t + 1, cols] = jnp.sum(c_cols[t] * s, axis=0, keepdims=True)
            st_out_ref[cols, :] = s.T
    del heads_per_block


def ssd_recur(xdt, dec, bm, cm, state, layer):
    seq = lambda w: pl.BlockSpec((None, DEC_SEQ, w), lambda b: (b, 0, 0))
    st = pl.BlockSpec((None, D_INNER, SSD_STATE), lambda b: (b, 0, 0))
    st_in = pl.BlockSpec((None, None, D_INNER, SSD_STATE), lambda b: (layer, b, 0, 0))
    return pl.pallas_call(
        _ssd_recur_kernel,
        grid=(DEC_BATCH,),
        in_specs=[seq(D_INNER), seq(D_INNER), seq(SSD_GN), seq(SSD_GN), st_in],
        out_specs=[seq(D_INNER), st],
        out_shape=[
            jax.ShapeDtypeStruct((DEC_BATCH, DEC_SEQ, D_INNER), F32),
            jax.ShapeDtypeStruct((DEC_BATCH, D_INNER, SSD_STATE), F32),
        ],
        compiler_params=_cparams("parallel"),
        name="ssd_recur",
    )(xdt, dec, bm, cm, state)


def _ssd_sample_post_kernel(yr_ref, xs_ref, z_ref, dskip_ref, ng_ref, y_ref):
    y = yr_ref[...] + dskip_ref[...] * xs_ref[...]
    y_ref[...] = _ssd_gate_norm(y, z_ref[...], ng_ref[...]).astype(y_ref.dtype)


def ssd_sample_post(y_raw, xs, z, p):
    full = lambda shape: pl.BlockSpec(shape, lambda i: (0,) * len(shape))
    return pl.pallas_call(
        _ssd_sample_post_kernel,
        grid=(1,),
        in_specs=[
            full((N_SAMPLE, D_INNER)), full((N_SAMPLE, D_INNER)),
            pl.BlockSpec((N_SAMPLE, D_INNER), lambda i: (SAMPLE_BLOCK, 0)),
            full((1, D_INNER)), full((1, D_INNER)),
        ],
        out_specs=full((N_SAMPLE, D_INNER)),
        out_shape=jax.ShapeDtypeStruct((N_SAMPLE, D_INNER), BF16),
        compiler_params=_cparams("arbitrary"),
        name="ssd_sample_post",
    )(y_raw, xs, z, p["d_skip"], p["norm_g"])


def _to_time_major(a):
    return jnp.swapaxes(a, 0, 1).reshape(a.shape[0] * a.shape[1], a.shape[2])


def _to_batch_major(a, t):
    return jnp.swapaxes(a.reshape(t, DEC_BATCH, a.shape[1]), 0, 1)


def _row(v):
    return v.reshape(1, -1).astype(F32)


def _pad_lanes(v):
    return jnp.pad(v.reshape(1, -1).astype(F32), ((0, 0), (0, LANES - v.shape[-1])))


def kernel(x_prompt, x_sample, state_lru_h, state_lru_conv, state_ssd, state_ssd_conv, cache_mem_k, cache_mem_v, mem_prompt, norm_mix, norm_mem, norm_memkv, norm_ffn, norm_final, lru_w_in, lru_conv_w, lru_conv_b, lru_w_a, lru_b_a, lru_w_x, lru_b_x, lru_lam, lru_w_out, ssd_w_in, ssd_conv_w, ssd_conv_b, ssd_dt_bias, ssd_a_log, ssd_d, ssd_norm_g, ssd_w_out, mem_w_q, mem_w_k, mem_w_v, mem_w_o, ffn_w1, ffn_w3, ffn_w2, moe_router, moe_w1, moe_w3, moe_w2):
    bf = lambda w: w.astype(BF16)
    x = jnp.concatenate([x_prompt.reshape(N_PROMPT, D_MODEL), _to_time_major(x_sample)], axis=0)
    mem = mem_prompt.reshape(BATCH * N_MEM, D_MODEL)
    head_of_col = jnp.arange(D_INNER, dtype=jnp.int32) // SSD_HEADDIM
    expand = (jnp.arange(LANES, dtype=jnp.int32)[:, None] == head_of_col[None, :]).astype(BF16)

    p_lru_h, p_lru_conv, p_ssd, p_ssd_conv, p_mk, p_mv = [], [], [], [], [], []
    s_lru_h, s_lru_conv, s_ssd, s_ssd_conv = [], [], [], []
    hist = CONV_W - 1
    for i in range(DEPTH):
        j = i // 2
        mk, mv = norm_matmul(mem, norm_memkv[i], [bf(mem_w_k[i]), bf(mem_w_v[i])], [F32, F32])
        p_mk.append(mk.reshape(BATCH, N_MEM, MEM_HEADS, MEM_HD))
        p_mv.append(mv.reshape(BATCH, N_MEM, MEM_HEADS, MEM_HD))

        if i % 2 == 0:
            p = dict(conv_w=lru_conv_w[j], conv_b=_row(lru_conv_b[j]), w_a=bf(lru_w_a[j]), b_a=_row(lru_b_a[j]),
                     w_x=bf(lru_w_x[j]), b_x=_row(lru_b_x[j]), lam=_row(lru_lam[j]))
            (proj,) = norm_matmul(x, norm_mix[i], [bf(lru_w_in[j])], [F32])
            y_p, h_p, c_p = lru_prompt(proj, p)
            y_s, h_s, c_s = lru_sample(proj, _to_time_major(state_lru_conv[j]), state_lru_h[j], p)
            p_lru_h.append(h_p.reshape(BATCH, D_RNN))
            p_lru_conv.append(c_p[:, HALO - hist :, :])
            s_lru_h.append(h_s)
            s_lru_conv.append(_to_batch_major(c_s, hist))
            x = matmul_residual(y_p, y_s, bf(lru_w_out[j]), x)
        else:
            w_in = ssd_w_in[j]
            w_z = bf(w_in[:, :D_INNER])
            w_xbc = bf(w_in[:, D_INNER : D_INNER + SSD_CONV_DIM])
            w_dt = bf(jnp.pad(w_in[:, D_INNER + SSD_CONV_DIM :], ((0, 0), (0, LANES - SSD_HEADS))))
            p = dict(conv_w=ssd_conv_w[j], conv_b=_row(ssd_conv_b[j]), dt_bias=_pad_lanes(ssd_dt_bias[j]),
                     a_log=_pad_lanes(ssd_a_log[j]), d_skip=_row(jnp.repeat(ssd_d[j], SSD_HEADDIM)),
                     norm_g=_row(ssd_norm_g[j]), expand=expand)
            z, xbc, dt = norm_matmul(x, norm_mix[i], [w_z, w_xbc, w_dt], [F32, F32, F32], tm=256)
            y_p, st_p, c_p = ssd_prompt(z, xbc, dt, p)
            xs_s, xdt_s, dec_s, bm_s, cm_s, c_s = ssd_sample_pre(xbc, dt, _to_time_major(state_ssd_conv[j]), p)
            y_raw, st_s = ssd_recur(
                _to_batch_major(xdt_s, DEC_SEQ), _to_batch_major(dec_s, DEC_SEQ),
                _to_batch_major(bm_s, DEC_SEQ), _to_batch_major(cm_s, DEC_SEQ),
                state_ssd.reshape(-1, DEC_BATCH, D_INNER, SSD_STATE), j)
            y_s = ssd_sample_post(_to_time_major(y_raw), xs_s, z, p)
            p_ssd.append(st_p.reshape(BATCH, SSD_HEADS, SSD_HEADDIM, SSD_STATE))
            p_ssd_conv.append(c_p[:, HALO - hist :, :])
            s_ssd.append(st_s.reshape(DEC_BATCH, SSD_HEADS, SSD_HEADDIM, SSD_STATE))
            s_ssd_conv.append(_to_batch_major(c_s, hist))
            x = matmul_residual(y_p, y_s, bf(ssd_w_out[j]), x)

        (qp,) = norm_matmul(x, norm_mem[i], [bf(mem_w_q[i])], [BF16])
        o = attn_prompt(qp, mk.reshape(BATCH, N_MEM, D_MODEL), mv.reshape(BATCH, N_MEM, D_MODEL))
        o_s = attn_sample(
            _to_batch_major(qp[N_PROMPT:], DEC_SEQ),
            cache_mem_k.reshape(DEPTH, DEC_BATCH, N_MEM, D_MODEL), cache_mem_v.reshape(DEPTH, DEC_BATCH, N_MEM, D_MODEL), i)
        x = matmul_residual(o, _to_time_major(o_s), bf(mem_w_o[i]), x)

        if i % 2 == 0:
            x = ffn(x, norm_ffn[i], bf(ffn_w1[j]), bf(ffn_w3[j]), bf(ffn_w2[j]), tf=D_FF // 2)
        else:
            x = moe(x, norm_ffn[i], moe_router[j], moe_w1, moe_w3, moe_w2, j)

    y_prompt = rmsnorm_rows(x, norm_final, 0, N_PROMPT // ROW_TILE).reshape(BATCH, SEQ, D_MODEL)
    y_sample = _to_batch_major(rmsnorm_rows(x, norm_final, N_PROMPT // ROW_TILE, N_SAMPLE // ROW_TILE), DEC_SEQ)
    return (y_prompt, y_sample,
            jnp.stack(p_lru_h), jnp.stack(p_lru_conv), jnp.stack(p_ssd), jnp.stack(p_ssd_conv),
            jnp.stack(p_mk), jnp.stack(p_mv),
            jnp.stack(s_lru_h), jnp.stack(s_lru_conv), jnp.stack(s_ssd), jnp.stack(s_ssd_conv))
```

```python
import functools
import math

import jax
import jax.numpy as jnp
from jax import lax
from jax.experimental import pallas as pl
from jax.experimental.pallas import tpu as pltpu

F32 = jnp.float32
BF16 = jnp.bfloat16

D_MODEL = 1024
BATCH = 8
SEQ = 2048
DEPTH = 4
DEC_BATCH = 128
DEC_SEQ = 4
CONV_W = 4
EPS = 1e-6
D_RNN = D_MODEL
LRU_BLOCKS = 8
LRU_BW = D_RNN // LRU_BLOCKS
LRU_C = 8.0
D_INNER = 2 * D_MODEL
SSD_HEADDIM = 64
SSD_HEADS = D_INNER // SSD_HEADDIM
SSD_GROUPS = 4
SSD_GROUP_W = D_INNER // SSD_GROUPS
SSD_STATE = 128
SSD_GN = SSD_GROUPS * SSD_STATE
SSD_CONV_DIM = D_INNER + 2 * SSD_GN
SSD_CHUNK = 128
N_MEM = 256
MEM_HEADS = 4
MEM_HD = D_MODEL // MEM_HEADS
D_FF = 2816
N_EXPERTS = 8
TOP_K = 2
D_FF_EXPERT = 3584

LANES = 128
SUBLANES = 8
VMEM_LIMIT_BYTES = 56 * 1024 * 1024

N_PROMPT = BATCH * SEQ
N_SAMPLE = DEC_BATCH * DEC_SEQ
N_ROWS = N_PROMPT + N_SAMPLE
ROW_TILE = 512
SAMPLE_BLOCK = N_PROMPT // N_SAMPLE
HALO = SUBLANES
MOE_TILE = 1024
MOE_FF_TILE = 512


def _cparams(*sem):
    return pltpu.CompilerParams(dimension_semantics=sem, vmem_limit_bytes=VMEM_LIMIT_BYTES)


def _rms(x, g):
    return x * lax.rsqrt(jnp.mean(x * x, axis=-1, keepdims=True) + EPS) * g


def _sigmoid(x):
    return 1.0 / (1.0 + jnp.exp(-x))


def _silu(x):
    h = 0.5 * x
    return h + h * jnp.tanh(h)


def _softplus(x):
    return jnp.maximum(x, 0.0) + jnp.log1p(jnp.exp(-jnp.abs(x)))


def _split3(x):
    a = x.astype(BF16)
    r = x - a.astype(F32)
    b = r.astype(BF16)
    c = (r - b.astype(F32)).astype(BF16)
    return a, b, c


def _dot(a, b):
    return jnp.dot(a, b, preferred_element_type=F32)


def _dot_f32_lhs(x, m):
    a, b, c = _split3(x)
    return _dot(a, m) + _dot(b, m) + _dot(c, m)


def _expand_heads(x, expand):
    hi = x.astype(BF16)
    lo = (x - hi.astype(F32)).astype(BF16)
    return _dot(hi, expand) + _dot(lo, expand)


def _dot_f32_rhs(m, x):
    a, b, c = _split3(x)
    return _dot(m, a) + _dot(m, b) + _dot(m, c)


def _norm_matmul_kernel(n_w, x_ref, g_ref, *refs):
    h = _rms(x_ref[...], g_ref[...]).astype(BF16)
    for w_ref, o_ref in zip(refs[:n_w], refs[n_w:]):
        o_ref[...] = _dot(h, w_ref[...]).astype(o_ref.dtype)


def norm_matmul(x, g, ws, out_dtypes, tm=ROW_TILE):
    rows, k = x.shape
    in_specs = [pl.BlockSpec((tm, k), lambda i: (i, 0)), pl.BlockSpec((1, k), lambda i: (0, 0))]
    in_specs += [pl.BlockSpec(w.shape, lambda i: (0, 0)) for w in ws]
    out_specs = [pl.BlockSpec((tm, w.shape[1]), lambda i: (i, 0)) for w in ws]
    out_shape = [jax.ShapeDtypeStruct((rows, w.shape[1]), dt) for w, dt in zip(ws, out_dtypes)]
    return pl.pallas_call(
        functools.partial(_norm_matmul_kernel, len(ws)),
        grid=(rows // tm,),
        in_specs=in_specs,
        out_specs=out_specs,
        out_shape=out_shape,
        compiler_params=_cparams("parallel"),
        name="norm_matmul",
    )(x, g.reshape(1, k), *ws)


def _concat_rows_kernel(n_p, a_ref, b_ref, o_ref):
    i = pl.program_id(0)

    @pl.when(i < n_p)
    def _():
        o_ref[...] = a_ref[...]

    @pl.when(i >= n_p)
    def _():
        o_ref[...] = b_ref[...]


def concat_rows(a, b, tm=ROW_TILE):
    k = a.shape[1]
    n_p = a.shape[0] // tm
    n_s = b.shape[0] // tm
    return pl.pallas_call(
        functools.partial(_concat_rows_kernel, n_p),
        grid=(n_p + n_s,),
        in_specs=[
            pl.BlockSpec((tm, k), lambda i: (jnp.minimum(i, n_p - 1), 0)),
            pl.BlockSpec((tm, k), lambda i: (jnp.maximum(i - n_p, 0), 0)),
        ],
        out_specs=pl.BlockSpec((tm, k), lambda i: (i, 0)),
        out_shape=jax.ShapeDtypeStruct((a.shape[0] + b.shape[0], k), a.dtype),
        compiler_params=_cparams("arbitrary"),
        name="concat_rows",
    )(a, b)


def _mem_kv_kernel(bt, m_ref, g_ref, wk_ref, wv_ref, k2_ref, v2_ref, k4_ref, v4_ref):
    h = _rms(m_ref[...], g_ref[...]).astype(BF16)
    for w_ref, o2_ref, o4_ref in ((wk_ref, k2_ref, k4_ref), (wv_ref, v2_ref, v4_ref)):
        r = _dot(h, w_ref[...].astype(BF16))
        o2_ref[...] = r
        o4_ref[...] = r.reshape(bt, N_MEM, MEM_HEADS, MEM_HD)


def mem_kv(mem, g, w_k, w_v, bt=2):
    rows, k = mem.shape
    tm = bt * N_MEM
    flat = lambda: pl.BlockSpec((None, tm, k), lambda l, i: (l, i, 0))
    heads = lambda: pl.BlockSpec((None, bt, N_MEM, MEM_HEADS, MEM_HD), lambda l, i: (l, i, 0, 0, 0))
    wspec = lambda: pl.BlockSpec((None, k, k), lambda l, i: (l, 0, 0))
    flat_shape = jax.ShapeDtypeStruct((DEPTH, rows, k), F32)
    heads_shape = jax.ShapeDtypeStruct((DEPTH, BATCH, N_MEM, MEM_HEADS, MEM_HD), F32)
    return pl.pallas_call(
        functools.partial(_mem_kv_kernel, bt),
        grid=(DEPTH, rows // tm),
        in_specs=[
            pl.BlockSpec((tm, k), lambda l, i: (i, 0)),
            pl.BlockSpec((None, 1, k), lambda l, i: (l, 0, 0)),
            wspec(), wspec(),
        ],
        out_specs=[flat(), flat(), heads(), heads()],
        out_shape=[flat_shape, flat_shape, heads_shape, heads_shape],
        compiler_params=_cparams("arbitrary", "arbitrary"),
        name="mem_kv",
    )(mem, g.reshape(DEPTH, 1, k), w_k, w_v)


def _matmul_residual_kernel(n_p, yp_ref, ys_ref, w_ref, r_ref, o_ref):
    i = pl.program_id(0)

    @pl.when(i < n_p)
    def _():
        o_ref[...] = r_ref[...] + _dot(yp_ref[...].astype(BF16), w_ref[...])

    @pl.when(i >= n_p)
    def _():
        o_ref[...] = r_ref[...] + _dot(ys_ref[...].astype(BF16), w_ref[...])


def matmul_residual(y_prompt, y_sample, w, res, tm=ROW_TILE):
    k = y_prompt.shape[1]
    n = w.shape[1]
    n_p = y_prompt.shape[0] // tm
    n_s = y_sample.shape[0] // tm
    return pl.pallas_call(
        functools.partial(_matmul_residual_kernel, n_p),
        grid=(n_p + n_s,),
        in_specs=[
            pl.BlockSpec((tm, k), lambda i: (jnp.minimum(i, n_p - 1), 0)),
            pl.BlockSpec((tm, k), lambda i: (jnp.maximum(i - n_p, 0), 0)),
            pl.BlockSpec((k, n), lambda i: (0, 0)),
            pl.BlockSpec((tm, n), lambda i: (i, 0)),
        ],
        out_specs=pl.BlockSpec((tm, n), lambda i: (i, 0)),
        out_shape=jax.ShapeDtypeStruct(res.shape, F32),
        compiler_params=_cparams("arbitrary"),
        name="matmul_residual",
    )(y_prompt, y_sample, w, res)


def _rmsnorm_kernel(x_ref, g_ref, o_ref):
    o_ref[...] = _rms(x_ref[...], g_ref[...])


def rmsnorm_rows(x, g, first_block, n_blocks, tm=ROW_TILE):
    k = x.shape[1]
    return pl.pallas_call(
        _rmsnorm_kernel,
        grid=(n_blocks,),
        in_specs=[pl.BlockSpec((tm, k), lambda i: (i + first_block, 0)), pl.BlockSpec((1, k), lambda i: (0, 0))],
        out_specs=pl.BlockSpec((tm, k), lambda i: (i, 0)),
        out_shape=jax.ShapeDtypeStruct((n_blocks * tm, k), F32),
        compiler_params=_cparams("parallel"),
        name="final_norm",
    )(x, g.reshape(1, k))


def _swiglu_partial(h, w1_ref, w3_ref, w2_ref):
    a = _dot(h, w1_ref[...])
    b = _dot(h, w3_ref[...])
    return _dot((a * _sigmoid(a) * b).astype(BF16), w2_ref[...])


def _ffn_kernel(x_ref, g_ref, w1_ref, w3_ref, w2_ref, o_ref, h_scr, acc_scr):
    j = pl.program_id(1)

    @pl.when(j == 0)
    def _():
        h_scr[...] = _rms(x_ref[...], g_ref[...]).astype(BF16)
        acc_scr[...] = jnp.zeros_like(acc_scr)

    acc_scr[...] += _swiglu_partial(h_scr[...], w1_ref, w3_ref, w2_ref)

    @pl.when(j == pl.num_programs(1) - 1)
    def _():
        o_ref[...] = x_ref[...] + acc_scr[...]


def ffn(x, g, w1, w3, w2, tf, tm=ROW_TILE):
    rows, k = x.shape
    f = w1.shape[1]
    return pl.pallas_call(
        _ffn_kernel,
        grid=(rows // tm, f // tf),
        in_specs=[
            pl.BlockSpec((tm, k), lambda i, j: (i, 0)),
            pl.BlockSpec((1, k), lambda i, j: (0, 0)),
            pl.BlockSpec((k, tf), lambda i, j: (0, j)),
            pl.BlockSpec((k, tf), lambda i, j: (0, j)),
            pl.BlockSpec((tf, k), lambda i, j: (j, 0)),
        ],
        out_specs=pl.BlockSpec((tm, k), lambda i, j: (i, 0)),
        out_shape=jax.ShapeDtypeStruct((rows, k), F32),
        scratch_shapes=[pltpu.VMEM((tm, k), BF16), pltpu.VMEM((tm, k), F32)],
        compiler_params=_cparams("parallel", "arbitrary"),
        name="ffn",
    )(x, g.reshape(1, k), w1, w3, w2)


def _router_kernel(tm, x_ref, g_ref, wr_ref, gate_ref, rank_ref, exp_ref, cnt_ref, carry):
    @pl.when(pl.program_id(0) == 0)
    def _():
        carry[...] = jnp.zeros_like(carry)

    h = _rms(x_ref[...], g_ref[...])
    h1 = h.astype(BF16)
    h2 = (h - h1.astype(F32)).astype(BF16)
    w = wr_ref[...]
    w1 = w.astype(BF16)
    w2 = (w - w1.astype(F32)).astype(BF16)
    logits = _dot(h1, w1) + _dot(h1, w2) + _dot(h2, w1)
    lane = lax.broadcasted_iota(jnp.int32, logits.shape, 1).astype(F32)
    neg = jnp.float32(-jnp.inf)
    logits = jnp.where(lane < N_EXPERTS, logits, neg)
    m1 = jnp.max(logits, axis=-1, keepdims=True)
    i1 = jnp.min(jnp.where(logits == m1, lane, float(LANES)), axis=-1, keepdims=True)
    rest = jnp.where(lane == i1, neg, logits)
    m2 = jnp.max(rest, axis=-1, keepdims=True)
    i2 = jnp.min(jnp.where(rest == m2, lane, float(LANES)), axis=-1, keepdims=True)
    e2 = jnp.exp(m2 - m1)
    den = 1.0 + e2
    gate_ref[...] = jnp.where(lane == 0.0, 1.0 / den, 0.0) + jnp.where(lane == 1.0, e2 / den, 0.0)

    oh1_t = jnp.where(lane == i1, 1.0, 0.0).T
    oh2_t = jnp.where(lane == i2, 1.0, 0.0).T
    oh_t = oh1_t + oh2_t
    src = lax.broadcasted_iota(jnp.int32, (tm, tm), 0)
    dst = lax.broadcasted_iota(jnp.int32, (tm, tm), 1)
    earlier = jnp.where(src < dst, 1.0, 0.0).astype(BF16)
    before = _dot(oh_t.astype(BF16), earlier)
    base = jnp.concatenate([carry[...]] * (tm // LANES), axis=1) + before
    expert_id = lax.broadcasted_iota(jnp.int32, (LANES, tm), 0).astype(F32)
    col_sum = lambda a: jnp.sum(a, axis=0, keepdims=True)
    rank_ref[...] = jnp.concatenate([col_sum(oh1_t * base), col_sum(oh2_t * base)], axis=1).astype(jnp.int32)
    exp_ref[...] = jnp.concatenate([col_sum(oh1_t * expert_id), col_sum(oh2_t * expert_id)], axis=1).astype(jnp.int32)
    carry[...] += jnp.broadcast_to(jnp.sum(oh_t, axis=1, keepdims=True), carry.shape)
    cnt_ref[...] = carry[...]


def router(x, g, w_router, tm=ROW_TILE):
    rows, k = x.shape
    n_t = rows // tm
    wr = jnp.pad(w_router, ((0, 0), (0, LANES - N_EXPERTS)))
    return pl.pallas_call(
        functools.partial(_router_kernel, tm),
        grid=(n_t,),
        in_specs=[
            pl.BlockSpec((tm, k), lambda i: (i, 0)),
            pl.BlockSpec((1, k), lambda i: (0, 0)),
            pl.BlockSpec((k, LANES), lambda i: (0, 0)),
        ],
        out_specs=[
            pl.BlockSpec((tm, LANES), lambda i: (i, 0)),
            pl.BlockSpec((None, 1, TOP_K * tm), lambda i: (i, 0, 0)),
            pl.BlockSpec((None, 1, TOP_K * tm), lambda i: (i, 0, 0)),
            pl.BlockSpec((LANES, LANES), lambda i: (0, 0)),
        ],
        out_shape=[
            jax.ShapeDtypeStruct((rows, LANES), F32),
            jax.ShapeDtypeStruct((n_t, 1, TOP_K * tm), jnp.int32),
            jax.ShapeDtypeStruct((n_t, 1, TOP_K * tm), jnp.int32),
            jax.ShapeDtypeStruct((LANES, LANES), F32),
        ],
        scratch_shapes=[pltpu.VMEM((LANES, LANES), F32)],
        compiler_params=_cparams("arbitrary"),
        name="router",
    )(x, g.reshape(1, k), wr)


def _row_copy(src_ref, src_row, dst_ref, dst_row, sem):
    return pltpu.make_async_copy(src_ref.at[pl.ds(src_row, 1)], dst_ref.at[pl.ds(dst_row, 1)], sem)


def _dispatch_kernel(tm, pos_ref, x_ref, xs_in_ref, xs_ref, sem):
    del xs_in_ref

    def start(r, c):
        for choice in range(TOP_K):
            _row_copy(x_ref, r, xs_ref, pos_ref[0, choice * tm + r], sem).start(priority=choice)
        return c

    lax.fori_loop(0, tm, start, 0, unroll=8)
    for _ in range(TOP_K):
        pltpu.make_async_copy(x_ref, xs_ref.at[pl.ds(0, tm)], sem).wait()


def dispatch(x, pos, xs_zero, tm=ROW_TILE):
    rows, k = x.shape
    return pl.pallas_call(
        functools.partial(_dispatch_kernel, tm),
        grid=(rows // tm,),
        in_specs=[
            pl.BlockSpec((None, 1, TOP_K * tm), lambda i: (i, 0, 0), memory_space=pltpu.SMEM),
            pl.BlockSpec((tm, k), lambda i: (i, 0)),
            pl.BlockSpec(memory_space=pl.ANY),
        ],
        out_specs=pl.BlockSpec(memory_space=pl.ANY),
        out_shape=jax.ShapeDtypeStruct(xs_zero.shape, F32),
        scratch_shapes=[pltpu.SemaphoreType.DMA(())],
        input_output_aliases={2: 0},
        compiler_params=_cparams("arbitrary"),
        name="moe_dispatch",
    )(pos, x, xs_zero)


def _moe_kernel(texp_ref, tvalid_ref, x_ref, g_ref, w1_ref, w3_ref, w2_ref, o_ref, h_scr, acc_scr):
    del texp_ref
    j = pl.program_id(1)
    last = pl.num_programs(1) - 1
    valid = tvalid_ref[pl.program_id(0)]

    @pl.when(valid > 0)
    def _():
        @pl.when(j == 0)
        def _():
            h_scr[...] = _rms(x_ref[...], g_ref[...]).astype(BF16)
            acc_scr[...] = jnp.zeros_like(acc_scr)

        w1 = w1_ref[...].astype(BF16)
        w3 = w3_ref[...].astype(BF16)
        w2 = w2_ref[...].astype(BF16)
        half = x_ref.shape[0] // 2
        for lo in (0, half):

            @pl.when(valid > lo)
            def _():
                h = h_scr[lo : lo + half, :]
                a = _dot(h, w1)
                b = _dot(h, w3)
                acc_scr[lo : lo + half, :] += _dot((a * _sigmoid(a) * b).astype(BF16), w2)

        @pl.when(j == last)
        def _():
            o_ref[...] = acc_scr[...]

    @pl.when(jnp.logical_and(valid == 0, j == last))
    def _():
        o_ref[...] = jnp.zeros_like(o_ref)


def moe_experts(xs, g, w1, w3, w2, layer, tile_expert, tile_valid, tm, tf):
    rows, k = xs.shape
    f = w1.shape[-1]
    n_f = f // tf

    def jf(i, j, tvalid):
        return jnp.where(tvalid[i] > 0, j, n_f - 1)

    grid_spec = pltpu.PrefetchScalarGridSpec(
        num_scalar_prefetch=2,
        grid=(rows // tm, n_f),
        in_specs=[
            pl.BlockSpec((tm, k), lambda i, j, texp, tvalid: (i, 0)),
            pl.BlockSpec((1, k), lambda i, j, texp, tvalid: (0, 0)),
            pl.BlockSpec((None, None, k, tf), lambda i, j, texp, tvalid: (layer, texp[i], 0, jf(i, j, tvalid))),
            pl.BlockSpec((None, None, k, tf), lambda i, j, texp, tvalid: (layer, texp[i], 0, jf(i, j, tvalid))),
            pl.BlockSpec((None, None, tf, k), lambda i, j, texp, tvalid: (layer, texp[i], jf(i, j, tvalid), 0)),
        ],
        out_specs=pl.BlockSpec((tm, k), lambda i, j, texp, tvalid: (i, 0)),
        scratch_shapes=[pltpu.VMEM((tm, k), BF16), pltpu.VMEM((tm, k), F32)],
    )
    return pl.pallas_call(
        _moe_kernel,
        grid_spec=grid_spec,
        out_shape=jax.ShapeDtypeStruct((rows, k), F32),
        compiler_params=_cparams("arbitrary", "arbitrary"),
        name="moe_experts",
    )(tile_expert, tile_valid, xs, g.reshape(1, k), w1, w3, w2)


def _combine_kernel(tm, pos_ref, x_ref, gate_ref, ys_ref, o_ref, y1_scr, y2_scr, sem):
    bufs = (y1_scr, y2_scr)

    def start(r, c):
        for choice in range(TOP_K):
            _row_copy(ys_ref, pos_ref[0, choice * tm + r], bufs[choice], r, sem).start(priority=choice)
        return c

    lax.fori_loop(0, tm, start, 0, unroll=8)
    for choice in range(TOP_K):
        pltpu.make_async_copy(ys_ref.at[pl.ds(0, tm)], bufs[choice], sem).wait()
    g = gate_ref[...]
    o_ref[...] = x_ref[...] + (g[:, 0:1] * y1_scr[...] + g[:, 1:2] * y2_scr[...])


def combine(x, gate, pos, ys, tm=ROW_TILE):
    rows, k = x.shape
    return pl.pallas_call(
        functools.partial(_combine_kernel, tm),
        grid=(rows // tm,),
        in_specs=[
            pl.BlockSpec((None, 1, TOP_K * tm), lambda i: (i, 0, 0), memory_space=pltpu.SMEM),
            pl.BlockSpec((tm, k), lambda i: (i, 0)),
            pl.BlockSpec((tm, LANES), lambda i: (i, 0)),
            pl.BlockSpec(memory_space=pl.ANY),
        ],
        out_specs=pl.BlockSpec((tm, k), lambda i: (i, 0)),
        out_shape=jax.ShapeDtypeStruct((rows, k), F32),
        scratch_shapes=[pltpu.VMEM((tm, k), F32), pltpu.VMEM((tm, k), F32), pltpu.SemaphoreType.DMA(())],
        compiler_params=_cparams("arbitrary"),
        name="moe_combine",
    )(pos, x, gate, ys)


def moe(x, g, w_router, w1, w3, w2, layer, tm_e=MOE_TILE, tf=MOE_FF_TILE):
    rows, k = x.shape
    n_tiles = -(-(TOP_K * rows + N_EXPERTS * (tm_e - 1)) // tm_e)
    gate, rank, expert, cnt = router(x, g, w_router)
    counts = cnt[:N_EXPERTS, 0].astype(jnp.int32)
    padded = ((counts + tm_e - 1) // tm_e) * tm_e
    ends = jnp.cumsum(padded)
    offs = ends - padded
    pos = rank
    for e in range(N_EXPERTS):
        pos = pos + jnp.where(expert == e, offs[e], 0)
    tile_start = jnp.arange(n_tiles, dtype=jnp.int32) * tm_e
    tile_expert = jnp.minimum(jnp.sum(ends[None, :] <= tile_start[:, None], axis=1), N_EXPERTS - 1).astype(jnp.int32)
    tile_valid = jnp.clip(counts[tile_expert] - (tile_start - offs[tile_expert]), 0, tm_e).astype(jnp.int32)
    xs = dispatch(x, pos, jnp.zeros((n_tiles * tm_e, k), F32))
    ys = moe_experts(xs, g, w1, w3, w2, layer, tile_expert, tile_valid, tm_e, tf)
    return combine(x, gate, pos, ys)


def _conv_rows(xpad, cw, cb):
    ext = xpad[...]
    y = None
    for k in range(CONV_W):
        back = CONV_W - 1 - k
        src = ext if back == 0 else pltpu.roll(ext, back, axis=0)
        term = src[HALO:] * cw[k : k + 1]
        y = term if y is None else y + term
    return y + cb


def _conv_taps(xpad, base, rows, step, cw, cb):
    y = xpad[base - 3 * step : base - 3 * step + rows, :] * cw[0:1]
    for k in range(1, CONV_W):
        lo = base - (CONV_W - 1 - k) * step
        y = y + xpad[lo : lo + rows, :] * cw[k : k + 1]
    return y + cb


def _lru_gates(xc, wa_ref, ba_ref, wx_ref, bx_ref, lam_ref):
    xcb = xc.astype(BF16)
    r_parts, i_parts = [], []
    for k in range(LRU_BLOCKS):
        blk = xcb[:, k * LRU_BW : (k + 1) * LRU_BW]
        r_parts.append(_dot(blk, wa_ref[k]))
        i_parts.append(_dot(blk, wx_ref[k]))
    r = _sigmoid(jnp.concatenate(r_parts, axis=1) + ba_ref[...])
    ig = _sigmoid(jnp.concatenate(i_parts, axis=1) + bx_ref[...])
    log_a = (-LRU_C * r) * _softplus(-lam_ref[...])
    a = jnp.exp(log_a)
    one_minus_a2 = -jnp.tanh(log_a) * (a * a + 1.0)
    u = jnp.sqrt(one_minus_a2) * (ig * xc)
    return a, u


def _lru_prompt_kernel(tt, proj_ref, cw_ref, cb_ref, wa_ref, ba_ref, wx_ref, bx_ref, lam_ref,
                       y_ref, hlast_ref, convn_ref, xpad, a_scr, u_scr, hs_scr, h_scr):
    t = pl.program_id(1)

    @pl.when(t == 0)
    def _():
        xpad[0:HALO, :] = jnp.zeros((HALO, D_RNN), F32)
        h_scr[...] = jnp.zeros_like(h_scr)

    @pl.when(t > 0)
    def _():
        xpad[0:HALO, :] = xpad[tt : tt + HALO, :]

    xpad[HALO : HALO + tt, :] = proj_ref[:, D_RNN:]
    xc = _conv_rows(xpad, cw_ref[...], cb_ref[...])
    a, u = _lru_gates(xc, wa_ref, ba_ref, wx_ref, bx_ref, lam_ref)
    a_scr[...] = a
    u_scr[...] = u

    def body(i, h):
        h = a_scr[pl.ds(i, 1), :] * h + u_scr[pl.ds(i, 1), :]
        hs_scr[pl.ds(i, 1), :] = h
        return h

    h_scr[...] = lax.fori_loop(0, tt, body, h_scr[...], unroll=8)
    y_ref[...] = hs_scr[...] * jax.nn.gelu(proj_ref[:, :D_RNN])

    @pl.when(t == pl.num_programs(1) - 1)
    def _():
        hlast_ref[...] = h_scr[...]
        convn_ref[...] = xpad[tt : tt + HALO, :]


def lru_prompt(proj, p, tt=256):
    n_t = SEQ // tt
    wspec = lambda shape: pl.BlockSpec(shape, lambda b, t: (0,) * len(shape))
    return pl.pallas_call(
        functools.partial(_lru_prompt_kernel, tt),
        grid=(BATCH, n_t),
        in_specs=[
            pl.BlockSpec((tt, 2 * D_RNN), lambda b, t: (b * n_t + t, 0)),
            wspec((CONV_W, D_RNN)), wspec((1, D_RNN)),
            wspec((LRU_BLOCKS, LRU_BW, LRU_BW)), wspec((1, D_RNN)),
            wspec((LRU_BLOCKS, LRU_BW, LRU_BW)), wspec((1, D_RNN)),
            wspec((1, D_RNN)),
        ],
        out_specs=[
            pl.BlockSpec((tt, D_RNN), lambda b, t: (b * n_t + t, 0)),
            pl.BlockSpec((None, 1, D_RNN), lambda b, t: (b, 0, 0)),
            pl.BlockSpec((None, HALO, D_RNN), lambda b, t: (b, 0, 0)),
        ],
        out_shape=[
            jax.ShapeDtypeStruct((N_PROMPT, D_RNN), F32),
            jax.ShapeDtypeStruct((BATCH, 1, D_RNN), F32),
            jax.ShapeDtypeStruct((BATCH, HALO, D_RNN), F32),
        ],
        scratch_shapes=[
            pltpu.VMEM((HALO + tt, D_RNN), F32),
            pltpu.VMEM((tt, D_RNN), F32),
            pltpu.VMEM((tt, D_RNN), F32),
            pltpu.VMEM((tt, D_RNN), F32),
            pltpu.VMEM((1, D_RNN), F32),
        ],
        compiler_params=_cparams("parallel", "arbitrary"),
        name="lru_prompt",
    )(proj, p["conv_w"], p["conv_b"], p["w_a"], p["b_a"], p["w_x"], p["b_x"], p["lam"])


def _lru_sample_kernel(proj_ref, convp_ref, hprev_ref, cw_ref, cb_ref, wa_ref, ba_ref, wx_ref, bx_ref, lam_ref,
                       y_ref, hlast_ref, convn_ref, xpad):
    hist = (CONV_W - 1) * DEC_BATCH
    xpad[0:hist, :] = convp_ref[...]
    xpad[hist:, :] = proj_ref[:, D_RNN:]
    xc = _conv_taps(xpad, hist, N_SAMPLE, DEC_BATCH, cw_ref[...], cb_ref[...])
    a, u = _lru_gates(xc, wa_ref, ba_ref, wx_ref, bx_ref, lam_ref)
    gate = jax.nn.gelu(proj_ref[:, :D_RNN])
    h = hprev_ref[...]
    for t in range(DEC_SEQ):
        rows = slice(t * DEC_BATCH, (t + 1) * DEC_BATCH)
        h = a[rows] * h + u[rows]
        y_ref[rows, :] = h * gate[rows]
    hlast_ref[...] = h
    convn_ref[...] = xpad[N_SAMPLE:, :]


def lru_sample(proj, conv_prev, h_prev, p):
    hist = (CONV_W - 1) * DEC_BATCH
    full = lambda shape: pl.BlockSpec(shape, lambda i: (0,) * len(shape))
    return pl.pallas_call(
        _lru_sample_kernel,
        grid=(1,),
        in_specs=[
            pl.BlockSpec((N_SAMPLE, 2 * D_RNN), lambda i: (SAMPLE_BLOCK, 0)),
            full((hist, D_RNN)), full((DEC_BATCH, D_RNN)),
            full((CONV_W, D_RNN)), full((1, D_RNN)),
            full((LRU_BLOCKS, LRU_BW, LRU_BW)), full((1, D_RNN)),
            full((LRU_BLOCKS, LRU_BW, LRU_BW)), full((1, D_RNN)),
            full((1, D_RNN)),
        ],
        out_specs=[
            full((N_SAMPLE, D_RNN)),
            full((DEC_BATCH, D_RNN)),
            full((hist, D_RNN)),
        ],
        out_shape=[
            jax.ShapeDtypeStruct((N_SAMPLE, D_RNN), F32),
            jax.ShapeDtypeStruct((DEC_BATCH, D_RNN), F32),
            jax.ShapeDtypeStruct((hist, D_RNN), F32),
        ],
        scratch_shapes=[pltpu.VMEM((hist + N_SAMPLE, D_RNN), F32)],
        compiler_params=_cparams("arbitrary"),
        name="lru_sample",
    )(proj, conv_prev, h_prev, p["conv_w"], p["conv_b"], p["w_a"], p["b_a"], p["w_x"], p["b_x"], p["lam"])


def _attend(q, k, v):
    outs = []
    for h in range(MEM_HEADS):
        hs = slice(h * MEM_HD, (h + 1) * MEM_HD)
        s = lax.dot_general(q[:, hs], k[:, hs], (((1,), (1,)), ((), ())), preferred_element_type=F32)
        s = s * (MEM_HD ** -0.5)
        e = jnp.exp(s - jnp.max(s, axis=-1, keepdims=True))
        p = e / jnp.sum(e, axis=-1, keepdims=True)
        outs.append(_dot(p.astype(BF16), v[:, hs]))
    return jnp.concatenate(outs, axis=1)


def _attn_prompt_kernel(q_ref, k_ref, v_ref, o_ref):
    o_ref[...] = _attend(q_ref[...], k_ref[...].astype(BF16), v_ref[...].astype(BF16)).astype(o_ref.dtype)


def attn_prompt(q, k, v, layer, tt=512):
    n_t = SEQ // tt
    return pl.pallas_call(
        _attn_prompt_kernel,
        grid=(BATCH, n_t),
        in_specs=[
            pl.BlockSpec((tt, D_MODEL), lambda b, t: (b * n_t + t, 0)),
            pl.BlockSpec((None, N_MEM, D_MODEL), lambda b, t: (layer, b, 0)),
            pl.BlockSpec((None, N_MEM, D_MODEL), lambda b, t: (layer, b, 0)),
        ],
        out_specs=pl.BlockSpec((tt, D_MODEL), lambda b, t: (b * n_t + t, 0)),
        out_shape=jax.ShapeDtypeStruct((N_PROMPT, D_MODEL), BF16),
        compiler_params=_cparams("parallel", "arbitrary"),
        name="attn_prompt",
    )(q, k, v)


def _attn_sample_kernel(bb, q_ref, k_ref, v_ref, o_ref):
    rows = MEM_HEADS * DEC_SEQ
    cols = N_MEM * MEM_HEADS
    row_head = lax.broadcasted_iota(jnp.int32, (rows, cols), 0) // DEC_SEQ
    col_head = lax.broadcasted_iota(jnp.int32, (rows, cols), 1) % MEM_HEADS
    same_head = row_head == col_head
    for i in range(bb):
        k2 = k_ref[i].reshape(cols, MEM_HD).astype(BF16)
        v2 = v_ref[i].reshape(cols, MEM_HD).astype(BF16)
        q = q_ref[i]
        qh = jnp.concatenate([q[:, h * MEM_HD : (h + 1) * MEM_HD] for h in range(MEM_HEADS)], axis=0)
        s = lax.dot_general(qh, k2, (((1,), (1,)), ((), ())), preferred_element_type=F32) * (MEM_HD ** -0.5)
        s = jnp.where(same_head, s, -jnp.inf)
        e = jnp.exp(s - jnp.max(s, axis=-1, keepdims=True))
        p = e / jnp.sum(e, axis=-1, keepdims=True)
        oh = _dot(p.astype(BF16), v2)
        o_ref[i] = jnp.concatenate([oh[h * DEC_SEQ : (h + 1) * DEC_SEQ] for h in range(MEM_HEADS)], axis=1)


def attn_sample(q, k, v, layer, bb=4):
    return pl.pallas_call(
        functools.partial(_attn_sample_kernel, bb),
        grid=(DEC_BATCH // bb,),
        in_specs=[
            pl.BlockSpec((bb, DEC_SEQ, D_MODEL), lambda i: (i, 0, 0)),
            pl.BlockSpec((None, bb, N_MEM, MEM_HEADS, MEM_HD), lambda i: (layer, i, 0, 0, 0)),
            pl.BlockSpec((None, bb, N_MEM, MEM_HEADS, MEM_HD), lambda i: (layer, i, 0, 0, 0)),
        ],
        out_specs=pl.BlockSpec((bb, DEC_SEQ, D_MODEL), lambda i: (i, 0, 0)),
        out_shape=jax.ShapeDtypeStruct((DEC_BATCH, DEC_SEQ, D_MODEL), F32),
        compiler_params=_cparams("parallel"),
        name="attn_sample",
    )(q, k, v)


def _ssd_gate_norm(y, z, ng):
    y = y * _silu(z)
    outs = []
    for g in range(SSD_GROUPS):
        yg = y[:, g * SSD_GROUP_W : (g + 1) * SSD_GROUP_W]
        outs.append(yg * lax.rsqrt(jnp.mean(yg * yg, axis=-1, keepdims=True) + EPS))
    return jnp.concatenate(outs, axis=1) * ng


def _ssd_prompt_kernel(q, z_ref, xbc_ref, dt_ref, cw_ref, cb_ref, dtb_ref, alog_ref, dskip_ref, ng_ref, e_ref,
                       y_ref, st_ref, convn_ref, xpad, s_t, y_scr):
    t = pl.program_id(1)

    @pl.when(t == 0)
    def _():
        xpad[0:HALO, :] = jnp.zeros((HALO, SSD_CONV_DIM), F32)
        s_t[...] = jnp.zeros_like(s_t)

    @pl.when(t > 0)
    def _():
        xpad[0:HALO, :] = xpad[q : q + HALO, :]

    xpad[HALO : HALO + q, :] = xbc_ref[...]
    xbc = _silu(_conv_rows(xpad, cw_ref[...], cb_ref[...]))
    xs = xbc[:, :D_INNER]

    dt = _softplus(dt_ref[...] + dtb_ref[...])
    adt = -jnp.exp(alog_ref[...]) * dt
    row_i = lax.broadcasted_iota(jnp.int32, (q, q), 0)
    col_i = lax.broadcasted_iota(jnp.int32, (q, q), 1)
    tril = row_i >= col_i
    a_cs = _dot_f32_rhs(jnp.where(tril, 1.0, 0.0).astype(BF16), adt)
    a_cs_t = a_cs.T
    a_end = a_cs[q - 1 : q, :]
    expand = e_ref[...]
    ecs_x = _expand_heads(jnp.exp(a_cs), expand)
    xb = (xs * _expand_heads(dt, expand)).astype(BF16)
    xd = (xs * _expand_heads(dt * jnp.exp(a_end - a_cs), expand)).astype(BF16)

    for g in range(SSD_GROUPS):
        gc = slice(g * SSD_GROUP_W, (g + 1) * SSD_GROUP_W)
        bg = xbc[:, D_INNER + g * SSD_STATE : D_INNER + (g + 1) * SSD_STATE]
        cg = xbc[:, D_INNER + SSD_GN + g * SSD_STATE : D_INNER + SSD_GN + (g + 1) * SSD_STATE].astype(BF16)
        cb_mat = lax.dot_general(cg, bg.astype(BF16), (((1,), (1,)), ((), ())), preferred_element_type=F32)
        sg = s_t[:, gc]
        y_scr[:, gc] = _dot(cg, sg.astype(BF16)) * ecs_x[:, gc]
        s_t[:, gc] = ecs_x[q - 1 : q, gc] * sg + _dot(bg.T.astype(BF16), xd[:, gc])
        for e in range(SSD_HEADS // SSD_GROUPS):
            h = g * (SSD_HEADS // SSD_GROUPS) + e
            hc = slice(h * SSD_HEADDIM, (h + 1) * SSD_HEADDIM)
            seg = a_cs[:, h : h + 1] - a_cs_t[h : h + 1, :]
            decay = jnp.where(tril, jnp.exp(jnp.minimum(seg, 0.0)), 0.0)
            y_scr[:, hc] += _dot((cb_mat * decay).astype(BF16), xb[:, hc])

    y = y_scr[...] + dskip_ref[...] * xs
    y_ref[...] = _ssd_gate_norm(y, z_ref[...], ng_ref[...]).astype(y_ref.dtype)

    @pl.when(t == pl.num_programs(1) - 1)
    def _():
        for j in range(D_INNER // LANES):
            st_ref[j * LANES : (j + 1) * LANES, :] = s_t[:, j * LANES : (j + 1) * LANES].T
        convn_ref[...] = xpad[q : q + HALO, :]


def ssd_prompt(z, xbc, dt, p, q=SSD_CHUNK):
    n_t = SEQ // q
    rows = lambda w: pl.BlockSpec((q, w), lambda b, t: (b * n_t + t, 0))
    wspec = lambda shape: pl.BlockSpec(shape, lambda b, t: (0,) * len(shape))
    return pl.pallas_call(
        functools.partial(_ssd_prompt_kernel, q),
        grid=(BATCH, n_t),
        in_specs=[
            rows(D_INNER), rows(SSD_CONV_DIM), rows(LANES),
            wspec((CONV_W, SSD_CONV_DIM)), wspec((1, SSD_CONV_DIM)),
            wspec((1, LANES)), wspec((1, LANES)), wspec((1, D_INNER)), wspec((1, D_INNER)),
            wspec((LANES, D_INNER)),
        ],
        out_specs=[
            rows(D_INNER),
            pl.BlockSpec((None, D_INNER, SSD_STATE), lambda b, t: (b, 0, 0)),
            pl.BlockSpec((None, HALO, SSD_CONV_DIM), lambda b, t: (b, 0, 0)),
        ],
        out_shape=[
            jax.ShapeDtypeStruct((N_PROMPT, D_INNER), BF16),
            jax.ShapeDtypeStruct((BATCH, D_INNER, SSD_STATE), F32),
            jax.ShapeDtypeStruct((BATCH, HALO, SSD_CONV_DIM), F32),
        ],
        scratch_shapes=[
            pltpu.VMEM((HALO + q, SSD_CONV_DIM), F32),
            pltpu.VMEM((SSD_STATE, D_INNER), F32),
            pltpu.VMEM((q, D_INNER), F32),
        ],
        compiler_params=_cparams("parallel", "arbitrary"),
        name="ssd_prompt",
    )(z, xbc, dt, p["conv_w"], p["conv_b"], p["dt_bias"], p["a_log"], p["d_skip"], p["norm_g"], p["expand"])


def _ssd_sample_pre_kernel(xbc_ref, dt_ref, convp_ref, cw_ref, cb_ref, dtb_ref, alog_ref, e_ref,
                           xs_ref, xdt_ref, dec_ref, bm_ref, cm_ref, convn_ref, xpad):
    hist = (CONV_W - 1) * DEC_BATCH
    xpad[0:hist, :] = convp_ref[...]
    xpad[hist:, :] = xbc_ref[...]
    xc = _conv_taps(xpad, hist, N_SAMPLE, DEC_BATCH, cw_ref[...], cb_ref[...])
    xbc = xc * _sigmoid(xc)
    xs = xbc[:, :D_INNER]
    dt = _softplus(dt_ref[...] + dtb_ref[...])
    adt = -jnp.exp(alog_ref[...]) * dt
    expand = e_ref[...]
    xs_ref[...] = xs
    xdt_ref[...] = xs * _dot_f32_lhs(dt, expand)
    dec_ref[...] = _dot_f32_lhs(jnp.exp(adt), expand)
    bm_ref[...] = xbc[:, D_INNER : D_INNER + SSD_GN]
    cm_ref[...] = xbc[:, D_INNER + SSD_GN :]
    convn_ref[...] = xpad[N_SAMPLE:, :]


def ssd_sample_pre(xbc, dt, conv_prev, p):
    hist = (CONV_W - 1) * DEC_BATCH
    full = lambda shape: pl.BlockSpec(shape, lambda i: (0,) * len(shape))
    out_w = [D_INNER, D_INNER, D_INNER, SSD_GN, SSD_GN]
    return pl.pallas_call(
        _ssd_sample_pre_kernel,
        grid=(1,),
        in_specs=[
            pl.BlockSpec((N_SAMPLE, SSD_CONV_DIM), lambda i: (SAMPLE_BLOCK, 0)),
            pl.BlockSpec((N_SAMPLE, LANES), lambda i: (SAMPLE_BLOCK, 0)),
            full((hist, SSD_CONV_DIM)),
            full((CONV_W, SSD_CONV_DIM)), full((1, SSD_CONV_DIM)),
            full((1, LANES)), full((1, LANES)), full((LANES, D_INNER)),
        ],
        out_specs=[full((N_SAMPLE, w)) for w in out_w] + [full((hist, SSD_CONV_DIM))],
        out_shape=[jax.ShapeDtypeStruct((N_SAMPLE, w), F32) for w in out_w]
        + [jax.ShapeDtypeStruct((hist, SSD_CONV_DIM), F32)],
        scratch_shapes=[pltpu.VMEM((hist + N_SAMPLE, SSD_CONV_DIM), F32)],
        compiler_params=_cparams("arbitrary"),
        name="ssd_sample_pre",
    )(xbc, dt, conv_prev, p["conv_w"], p["conv_b"], p["dt_bias"], p["a_log"], p["expand"])


def _ssd_recur_kernel(bb, xdt_ref, dec_ref, bm_ref, cm_ref, st_in_ref, y_ref, st_out_ref):
    for i in range(bb):
        xdt = xdt_ref[i]
        dec = dec_ref[i]
        for g in range(SSD_GROUPS):
            b_t = bm_ref[i, :, g * SSD_STATE : (g + 1) * SSD_STATE].T
            c_t = cm_ref[i, :, g * SSD_STATE : (g + 1) * SSD_STATE].T
            b_cols = [jnp.broadcast_to(b_t[:, t : t + 1], (SSD_STATE, LANES)) for t in range(DEC_SEQ)]
            c_cols = [jnp.broadcast_to(c_t[:, t : t + 1], (SSD_STATE, LANES)) for t in range(DEC_SEQ)]
            for jb in range(SSD_GROUP_W // LANES):
                j = g * (SSD_GROUP_W // LANES) + jb
                cols = slice(j * LANES, (j + 1) * LANES)
                s = st_in_ref[i, cols, :].T
                for t in range(DEC_SEQ):
                    s = dec[t : t + 1, cols] * s + b_cols[t] * xdt[t : t + 1, cols]
                    y_ref[i, t : t + 1, cols] = jnp.sum(c_cols[t] * s, axis=0, keepdims=True)
                st_out_ref[i, cols, :] = s.T


def ssd_recur(xdt, dec, bm, cm, state, layer, bb=2):
    seq = lambda w: pl.BlockSpec((bb, DEC_SEQ, w), lambda b: (b, 0, 0))
    st = pl.BlockSpec((bb, D_INNER, SSD_STATE), lambda b: (b, 0, 0))
    st_in = pl.BlockSpec((None, bb, D_INNER, SSD_STATE), lambda b: (layer, b, 0, 0))
    return pl.pallas_call(
        functools.partial(_ssd_recur_kernel, bb),
        grid=(DEC_BATCH // bb,),
        in_specs=[seq(D_INNER), seq(D_INNER), seq(SSD_GN), seq(SSD_GN), st_in],
        out_specs=[seq(D_INNER), st],
        out_shape=[
            jax.ShapeDtypeStruct((DEC_BATCH, DEC_SEQ, D_INNER), F32),
            jax.ShapeDtypeStruct((DEC_BATCH, D_INNER, SSD_STATE), F32),
        ],
        compiler_params=_cparams("parallel"),
        name="ssd_recur",
    )(xdt, dec, bm, cm, state)


def _ssd_sample_post_kernel(yr_ref, xs_ref, z_ref, dskip_ref, ng_ref, y_ref):
    y = yr_ref[...] + dskip_ref[...] * xs_ref[...]
    y_ref[...] = _ssd_gate_norm(y, z_ref[...], ng_ref[...]).astype(y_ref.dtype)


def ssd_sample_post(y_raw, xs, z, p):
    full = lambda shape: pl.BlockSpec(shape, lambda i: (0,) * len(shape))
    return pl.pallas_call(
        _ssd_sample_post_kernel,
        grid=(1,),
        in_specs=[
            full((N_SAMPLE, D_INNER)), full((N_SAMPLE, D_INNER)),
            pl.BlockSpec((N_SAMPLE, D_INNER), lambda i: (SAMPLE_BLOCK, 0)),
            full((1, D_INNER)), full((1, D_INNER)),
        ],
        out_specs=full((N_SAMPLE, D_INNER)),
        out_shape=jax.ShapeDtypeStruct((N_SAMPLE, D_INNER), BF16),
        compiler_params=_cparams("arbitrary"),
        name="ssd_sample_post",
    )(y_raw, xs, z, p["d_skip"], p["norm_g"])


def _to_time_major(a):
    return jnp.swapaxes(a, 0, 1).reshape(a.shape[0] * a.shape[1], a.shape[2])


def _to_batch_major(a, t):
    return jnp.swapaxes(a.reshape(t, DEC_BATCH, a.shape[1]), 0, 1)


def _row(v):
    return v.reshape(1, -1).astype(F32)


def _pad_lanes(v):
    return jnp.pad(v.reshape(1, -1).astype(F32), ((0, 0), (0, LANES - v.shape[-1])))


def kernel(x_prompt, x_sample, state_lru_h, state_lru_conv, state_ssd, state_ssd_conv, cache_mem_k, cache_mem_v, mem_prompt, norm_mix, norm_mem, norm_memkv, norm_ffn, norm_final, lru_w_in, lru_conv_w, lru_conv_b, lru_w_a, lru_b_a, lru_w_x, lru_b_x, lru_lam, lru_w_out, ssd_w_in, ssd_conv_w, ssd_conv_b, ssd_dt_bias, ssd_a_log, ssd_d, ssd_norm_g, ssd_w_out, mem_w_q, mem_w_k, mem_w_v, mem_w_o, ffn_w1, ffn_w3, ffn_w2, moe_router, moe_w1, moe_w3, moe_w2):
    bf = lambda w: w.astype(BF16)
    x = concat_rows(x_prompt.reshape(N_PROMPT, D_MODEL), _to_time_major(x_sample))
    mem = mem_prompt.reshape(BATCH * N_MEM, D_MODEL)
    head_of_col = jnp.arange(D_INNER, dtype=jnp.int32) // SSD_HEADDIM
    expand = (jnp.arange(LANES, dtype=jnp.int32)[:, None] == head_of_col[None, :]).astype(BF16)

    p_lru_h, p_lru_conv, p_ssd, p_ssd_conv = [], [], [], []
    s_lru_h, s_lru_conv, s_ssd, s_ssd_conv = [], [], [], []
    hist = CONV_W - 1
    mk, mv, p_mk, p_mv = mem_kv(mem, norm_memkv, mem_w_k, mem_w_v)
    for i in range(DEPTH):
        j = i // 2
        if i % 2 == 0:
            p = dict(conv_w=lru_conv_w[j], conv_b=_row(lru_conv_b[j]), w_a=bf(lru_w_a[j]), b_a=_row(lru_b_a[j]),
                     w_x=bf(lru_w_x[j]), b_x=_row(lru_b_x[j]), lam=_row(lru_lam[j]))
            (proj,) = norm_matmul(x, norm_mix[i], [bf(lru_w_in[j])], [F32])
            y_p, h_p, c_p = lru_prompt(proj, p)
            y_s, h_s, c_s = lru_sample(proj, _to_time_major(state_lru_conv[j]), state_lru_h[j], p)
            p_lru_h.append(h_p.reshape(BATCH, D_RNN))
            p_lru_conv.append(c_p[:, HALO - hist :, :])
            s_lru_h.append(h_s)
            s_lru_conv.append(_to_batch_major(c_s, hist))
            x = matmul_residual(y_p, y_s, bf(lru_w_out[j]), x)
        else:
            w_in = ssd_w_in[j]
            w_z = bf(w_in[:, :D_INNER])
            w_xbc = bf(w_in[:, D_INNER : D_INNER + SSD_CONV_DIM])
            w_dt = bf(jnp.pad(w_in[:, D_INNER + SSD_CONV_DIM :], ((0, 0), (0, LANES - SSD_HEADS))))
            p = dict(conv_w=ssd_conv_w[j], conv_b=_row(ssd_conv_b[j]), dt_bias=_pad_lanes(ssd_dt_bias[j]),
                     a_log=_pad_lanes(ssd_a_log[j]), d_skip=_row(jnp.repeat(ssd_d[j], SSD_HEADDIM)),
                     norm_g=_row(ssd_norm_g[j]), expand=expand)
            z, xbc, dt = norm_matmul(x, norm_mix[i], [w_z, w_xbc, w_dt], [F32, F32, F32], tm=256)
            y_p, st_p, c_p = ssd_prompt(z, xbc, dt, p)
            xs_s, xdt_s, dec_s, bm_s, cm_s, c_s = ssd_sample_pre(xbc, dt, _to_time_major(state_ssd_conv[j]), p)
            y_raw, st_s = ssd_recur(
                _to_batch_major(xdt_s, DEC_SEQ), _to_batch_major(dec_s, DEC_SEQ),
                _to_batch_major(bm_s, DEC_SEQ), _to_batch_major(cm_s, DEC_SEQ),
                state_ssd.reshape(-1, DEC_BATCH, D_INNER, SSD_STATE), j)
            y_s = ssd_sample_post(_to_time_major(y_raw), xs_s, z, p)
            p_ssd.append(st_p.reshape(BATCH, SSD_HEADS, SSD_HEADDIM, SSD_STATE))
            p_ssd_conv.append(c_p[:, HALO - hist :, :])
            s_ssd.append(st_s.reshape(DEC_BATCH, SSD_HEADS, SSD_HEADDIM, SSD_STATE))
            s_ssd_conv.append(_to_batch_major(c_s, hist))
            x = matmul_residual(y_p, y_s, bf(ssd_w_out[j]), x)

        (qp,) = norm_matmul(x, norm_mem[i], [bf(mem_w_q[i])], [BF16])
        o = attn_prompt(qp, mk, mv, i)
        o_s = attn_sample(
            _to_batch_major(qp[N_PROMPT:], DEC_SEQ),
            cache_mem_k, cache_mem_v, i)
        x = matmul_residual(o, _to_time_major(o_s), bf(mem_w_o[i]), x)

        if i % 2 == 0:
            x = ffn(x, norm_ffn[i], bf(ffn_w1[j]), bf(ffn_w3[j]), bf(ffn_w2[j]), tf=D_FF // 2)
        else:
            x = moe(x, norm_ffn[i], moe_router[j], moe_w1, moe_w3, moe_w2, j)

    y_prompt = rmsnorm_rows(x, norm_final, 0, N_PROMPT // ROW_TILE).reshape(BATCH, SEQ, D_MODEL)
    y_sample = _to_batch_major(rmsnorm_rows(x, norm_final, N_PROMPT // ROW_TILE, N_SAMPLE // ROW_TILE), DEC_SEQ)
    return (y_prompt, y_sample,
            jnp.stack(p_lru_h), jnp.stack(p_lru_conv), jnp.stack(p_ssd), jnp.stack(p_ssd_conv),
            p_mk, p_mv,
            jnp.stack(s_lru_h), jnp.stack(s_lru_conv), jnp.stack(s_ssd), jnp.stack(s_ssd_conv))
```

```python
import functools
import math

import jax
import jax.numpy as jnp
from jax import lax
from jax.experimental import pallas as pl
from jax.experimental.pallas import tpu as pltpu

F32 = jnp.float32
BF16 = jnp.bfloat16

D_MODEL = 1024
BATCH = 8
SEQ = 2048
DEPTH = 4
DEC_BATCH = 128
DEC_SEQ = 4
CONV_W = 4
EPS = 1e-6
D_RNN = D_MODEL
LRU_BLOCKS = 8
LRU_BW = D_RNN // LRU_BLOCKS
LRU_C = 8.0
D_INNER = 2 * D_MODEL
SSD_HEADDIM = 64
SSD_HEADS = D_INNER // SSD_HEADDIM
SSD_GROUPS = 4
SSD_GROUP_W = D_INNER // SSD_GROUPS
SSD_STATE = 128
SSD_GN = SSD_GROUPS * SSD_STATE
SSD_CONV_DIM = D_INNER + 2 * SSD_GN
SSD_CHUNK = 128
N_MEM = 256
MEM_HEADS = 4
MEM_HD = D_MODEL // MEM_HEADS
D_FF = 2816
N_EXPERTS = 8
TOP_K = 2
D_FF_EXPERT = 3584

LANES = 128
SUBLANES = 8
VMEM_LIMIT_BYTES = 56 * 1024 * 1024

N_PROMPT = BATCH * SEQ
N_SAMPLE = DEC_BATCH * DEC_SEQ
N_ROWS = N_PROMPT + N_SAMPLE
ROW_TILE = 512
SAMPLE_BLOCK = N_PROMPT // N_SAMPLE
HALO = SUBLANES
MOE_TILE = 1024
MOE_FF_TILE = 512


def _cparams(*sem):
    return pltpu.CompilerParams(dimension_semantics=sem, vmem_limit_bytes=VMEM_LIMIT_BYTES)


def _rms(x, g):
    return x * lax.rsqrt(jnp.mean(x * x, axis=-1, keepdims=True) + EPS) * g


def _sigmoid(x):
    return 1.0 / (1.0 + jnp.exp(-x))


def _silu(x):
    h = 0.5 * x
    return h + h * jnp.tanh(h)


def _softplus(x):
    return jnp.maximum(x, 0.0) + jnp.log1p(jnp.exp(-jnp.abs(x)))


def _split3(x):
    a = x.astype(BF16)
    r = x - a.astype(F32)
    b = r.astype(BF16)
    c = (r - b.astype(F32)).astype(BF16)
    return a, b, c


def _dot(a, b):
    return jnp.dot(a, b, preferred_element_type=F32)


def _dot_f32_lhs(x, m):
    a, b, c = _split3(x)
    return _dot(a, m) + _dot(b, m) + _dot(c, m)


def _expand_heads(x, expand):
    hi = x.astype(BF16)
    lo = (x - hi.astype(F32)).astype(BF16)
    return _dot(hi, expand) + _dot(lo, expand)


def _dot_f32_rhs(m, x):
    a, b, c = _split3(x)
    return _dot(m, a) + _dot(m, b) + _dot(m, c)


def _norm_matmul_kernel(n_w, x_ref, g_ref, *refs):
    h = _rms(x_ref[...], g_ref[...]).astype(BF16)
    for w_ref, o_ref in zip(refs[:n_w], refs[n_w:]):
        o_ref[...] = _dot(h, w_ref[...]).astype(o_ref.dtype)


def norm_matmul(x, g, ws, out_dtypes, tm=ROW_TILE):
    rows, k = x.shape
    in_specs = [pl.BlockSpec((tm, k), lambda i: (i, 0)), pl.BlockSpec((1, k), lambda i: (0, 0))]
    in_specs += [pl.BlockSpec(w.shape, lambda i: (0, 0)) for w in ws]
    out_specs = [pl.BlockSpec((tm, w.shape[1]), lambda i: (i, 0)) for w in ws]
    out_shape = [jax.ShapeDtypeStruct((rows, w.shape[1]), dt) for w, dt in zip(ws, out_dtypes)]
    return pl.pallas_call(
        functools.partial(_norm_matmul_kernel, len(ws)),
        grid=(rows // tm,),
        in_specs=in_specs,
        out_specs=out_specs,
        out_shape=out_shape,
        compiler_params=_cparams("parallel"),
        name="norm_matmul",
    )(x, g.reshape(1, k), *ws)


def _concat_rows_kernel(n_p, a_ref, b_ref, o_ref):
    i = pl.program_id(0)

    @pl.when(i < n_p)
    def _():
        o_ref[...] = a_ref[...]

    @pl.when(i >= n_p)
    def _():
        o_ref[...] = b_ref[...]


def concat_rows(a, b, tm=ROW_TILE):
    k = a.shape[1]
    n_p = a.shape[0] // tm
    n_s = b.shape[0] // tm
    return pl.pallas_call(
        functools.partial(_concat_rows_kernel, n_p),
        grid=(n_p + n_s,),
        in_specs=[
            pl.BlockSpec((tm, k), lambda i: (jnp.minimum(i, n_p - 1), 0)),
            pl.BlockSpec((tm, k), lambda i: (jnp.maximum(i - n_p, 0), 0)),
        ],
        out_specs=pl.BlockSpec((tm, k), lambda i: (i, 0)),
        out_shape=jax.ShapeDtypeStruct((a.shape[0] + b.shape[0], k), a.dtype),
        compiler_params=_cparams("arbitrary"),
        name="concat_rows",
    )(a, b)


def _mem_kv_kernel(bt, m_ref, g_ref, wk_ref, wv_ref, k2_ref, v2_ref, k4_ref, v4_ref):
    h = _rms(m_ref[...], g_ref[...]).astype(BF16)
    for w_ref, o2_ref, o4_ref in ((wk_ref, k2_ref, k4_ref), (wv_ref, v2_ref, v4_ref)):
        r = _dot(h, w_ref[...].astype(BF16))
        o2_ref[...] = r
        o4_ref[...] = r.reshape(bt, N_MEM, MEM_HEADS, MEM_HD)


def mem_kv(mem, g, w_k, w_v, bt=2):
    rows, k = mem.shape
    tm = bt * N_MEM
    flat = lambda: pl.BlockSpec((None, tm, k), lambda l, i: (l, i, 0))
    heads = lambda: pl.BlockSpec((None, bt, N_MEM, MEM_HEADS, MEM_HD), lambda l, i: (l, i, 0, 0, 0))
    wspec = lambda: pl.BlockSpec((None, k, k), lambda l, i: (l, 0, 0))
    flat_shape = jax.ShapeDtypeStruct((DEPTH, rows, k), F32)
    heads_shape = jax.ShapeDtypeStruct((DEPTH, BATCH, N_MEM, MEM_HEADS, MEM_HD), F32)
    return pl.pallas_call(
        functools.partial(_mem_kv_kernel, bt),
        grid=(DEPTH, rows // tm),
        in_specs=[
            pl.BlockSpec((tm, k), lambda l, i: (i, 0)),
            pl.BlockSpec((None, 1, k), lambda l, i: (l, 0, 0)),
            wspec(), wspec(),
        ],
        out_specs=[flat(), flat(), heads(), heads()],
        out_shape=[flat_shape, flat_shape, heads_shape, heads_shape],
        compiler_params=_cparams("arbitrary", "arbitrary"),
        name="mem_kv",
    )(mem, g.reshape(DEPTH, 1, k), w_k, w_v)


def _matmul_residual_kernel(n_p, yp_ref, ys_ref, w_ref, r_ref, o_ref):
    i = pl.program_id(0)

    @pl.when(i < n_p)
    def _():
        o_ref[...] = r_ref[...] + _dot(yp_ref[...].astype(BF16), w_ref[...])

    @pl.when(i >= n_p)
    def _():
        o_ref[...] = r_ref[...] + _dot(ys_ref[...].astype(BF16), w_ref[...])


def matmul_residual(y_prompt, y_sample, w, res, tm=ROW_TILE):
    k = y_prompt.shape[1]
    n = w.shape[1]
    n_p = y_prompt.shape[0] // tm
    n_s = y_sample.shape[0] // tm
    return pl.pallas_call(
        functools.partial(_matmul_residual_kernel, n_p),
        grid=(n_p + n_s,),
        in_specs=[
            pl.BlockSpec((tm, k), lambda i: (jnp.minimum(i, n_p - 1), 0)),
            pl.BlockSpec((tm, k), lambda i: (jnp.maximum(i - n_p, 0), 0)),
            pl.BlockSpec((k, n), lambda i: (0, 0)),
            pl.BlockSpec((tm, n), lambda i: (i, 0)),
        ],
        out_specs=pl.BlockSpec((tm, n), lambda i: (i, 0)),
        out_shape=jax.ShapeDtypeStruct(res.shape, F32),
        compiler_params=_cparams("arbitrary"),
        name="matmul_residual",
    )(y_prompt, y_sample, w, res)


def _rmsnorm_kernel(x_ref, g_ref, o_ref):
    o_ref[...] = _rms(x_ref[...], g_ref[...])


def rmsnorm_rows(x, g, first_block, n_blocks, tm=ROW_TILE):
    k = x.shape[1]
    return pl.pallas_call(
        _rmsnorm_kernel,
        grid=(n_blocks,),
        in_specs=[pl.BlockSpec((tm, k), lambda i: (i + first_block, 0)), pl.BlockSpec((1, k), lambda i: (0, 0))],
        out_specs=pl.BlockSpec((tm, k), lambda i: (i, 0)),
        out_shape=jax.ShapeDtypeStruct((n_blocks * tm, k), F32),
        compiler_params=_cparams("parallel"),
        name="final_norm",
    )(x, g.reshape(1, k))


def _swiglu_partial(h, w1_ref, w3_ref, w2_ref):
    a = _dot(h, w1_ref[...])
    b = _dot(h, w3_ref[...])
    return _dot((a * _sigmoid(a) * b).astype(BF16), w2_ref[...])


def _ffn_kernel(x_ref, g_ref, w1_ref, w3_ref, w2_ref, o_ref, h_scr, acc_scr):
    j = pl.program_id(1)

    @pl.when(j == 0)
    def _():
        h_scr[...] = _rms(x_ref[...], g_ref[...]).astype(BF16)
        acc_scr[...] = jnp.zeros_like(acc_scr)

    acc_scr[...] += _swiglu_partial(h_scr[...], w1_ref, w3_ref, w2_ref)

    @pl.when(j == pl.num_programs(1) - 1)
    def _():
        o_ref[...] = x_ref[...] + acc_scr[...]


def ffn(x, g, w1, w3, w2, tf, tm=ROW_TILE):
    rows, k = x.shape
    f = w1.shape[1]
    return pl.pallas_call(
        _ffn_kernel,
        grid=(rows // tm, f // tf),
        in_specs=[
            pl.BlockSpec((tm, k), lambda i, j: (i, 0)),
            pl.BlockSpec((1, k), lambda i, j: (0, 0)),
            pl.BlockSpec((k, tf), lambda i, j: (0, j)),
            pl.BlockSpec((k, tf), lambda i, j: (0, j)),
            pl.BlockSpec((tf, k), lambda i, j: (j, 0)),
        ],
        out_specs=pl.BlockSpec((tm, k), lambda i, j: (i, 0)),
        out_shape=jax.ShapeDtypeStruct((rows, k), F32),
        scratch_shapes=[pltpu.VMEM((tm, k), BF16), pltpu.VMEM((tm, k), F32)],
        compiler_params=_cparams("parallel", "arbitrary"),
        name="ffn",
    )(x, g.reshape(1, k), w1, w3, w2)


def _router_kernel(tm, x_ref, g_ref, wr_ref, gate_ref, rank_ref, exp_ref, cnt_ref, carry):
    @pl.when(pl.program_id(0) == 0)
    def _():
        carry[...] = jnp.zeros_like(carry)

    h = _rms(x_ref[...], g_ref[...])
    h1 = h.astype(BF16)
    h2 = (h - h1.astype(F32)).astype(BF16)
    w = wr_ref[...]
    w1 = w.astype(BF16)
    w2 = (w - w1.astype(F32)).astype(BF16)
    logits = _dot(h1, w1) + _dot(h1, w2) + _dot(h2, w1)
    lane = lax.broadcasted_iota(jnp.int32, logits.shape, 1).astype(F32)
    neg = jnp.float32(-jnp.inf)
    logits = jnp.where(lane < N_EXPERTS, logits, neg)
    m1 = jnp.max(logits, axis=-1, keepdims=True)
    i1 = jnp.min(jnp.where(logits == m1, lane, float(LANES)), axis=-1, keepdims=True)
    rest = jnp.where(lane == i1, neg, logits)
    m2 = jnp.max(rest, axis=-1, keepdims=True)
    i2 = jnp.min(jnp.where(rest == m2, lane, float(LANES)), axis=-1, keepdims=True)
    e2 = jnp.exp(m2 - m1)
    den = 1.0 + e2
    gate_ref[...] = jnp.where(lane == 0.0, 1.0 / den, 0.0) + jnp.where(lane == 1.0, e2 / den, 0.0)

    oh1_t = jnp.where(lane == i1, 1.0, 0.0).T
    oh2_t = jnp.where(lane == i2, 1.0, 0.0).T
    oh_t = oh1_t + oh2_t
    src = lax.broadcasted_iota(jnp.int32, (tm, tm), 0)
    dst = lax.broadcasted_iota(jnp.int32, (tm, tm), 1)
    earlier = jnp.where(src < dst, 1.0, 0.0).astype(BF16)
    before = _dot(oh_t.astype(BF16), earlier)
    base = jnp.concatenate([carry[...]] * (tm // LANES), axis=1) + before
    expert_id = lax.broadcasted_iota(jnp.int32, (LANES, tm), 0).astype(F32)
    col_sum = lambda a: jnp.sum(a, axis=0, keepdims=True)
    rank_ref[...] = jnp.concatenate([col_sum(oh1_t * base), col_sum(oh2_t * base)], axis=1).astype(jnp.int32)
    exp_ref[...] = jnp.concatenate([col_sum(oh1_t * expert_id), col_sum(oh2_t * expert_id)], axis=1).astype(jnp.int32)
    carry[...] += jnp.broadcast_to(jnp.sum(oh_t, axis=1, keepdims=True), carry.shape)
    cnt_ref[...] = carry[...]


def router(x, g, w_router, tm=ROW_TILE):
    rows, k = x.shape
    n_t = rows // tm
    wr = jnp.pad(w_router, ((0, 0), (0, LANES - N_EXPERTS)))
    return pl.pallas_call(
        functools.partial(_router_kernel, tm),
        grid=(n_t,),
        in_specs=[
            pl.BlockSpec((tm, k), lambda i: (i, 0)),
            pl.BlockSpec((1, k), lambda i: (0, 0)),
            pl.BlockSpec((k, LANES), lambda i: (0, 0)),
        ],
        out_specs=[
            pl.BlockSpec((tm, LANES), lambda i: (i, 0)),
            pl.BlockSpec((None, 1, TOP_K * tm), lambda i: (i, 0, 0)),
            pl.BlockSpec((None, 1, TOP_K * tm), lambda i: (i, 0, 0)),
            pl.BlockSpec((LANES, LANES), lambda i: (0, 0)),
        ],
        out_shape=[
            jax.ShapeDtypeStruct((rows, LANES), F32),
            jax.ShapeDtypeStruct((n_t, 1, TOP_K * tm), jnp.int32),
            jax.ShapeDtypeStruct((n_t, 1, TOP_K * tm), jnp.int32),
            jax.ShapeDtypeStruct((LANES, LANES), F32),
        ],
        scratch_shapes=[pltpu.VMEM((LANES, LANES), F32)],
        compiler_params=_cparams("arbitrary"),
        name="router",
    )(x, g.reshape(1, k), wr)


def _as_tiles(a):
    return a.reshape(a.shape[0], SUBLANES, LANES)


def _as_rows(a):
    return a.reshape(a.shape[0], SUBLANES * LANES)


def _row_copy(src_ref, src_row, dst_ref, dst_row, sem):
    return pltpu.make_async_copy(src_ref.at[src_row], dst_ref.at[dst_row], sem)


def _dispatch_kernel(tm, pos_ref, x_ref, xs_in_ref, xs_ref, x_tiles, sem):
    del xs_in_ref
    x_tiles[...] = _as_tiles(x_ref[...])

    def start(r, c):
        for choice in range(TOP_K):
            _row_copy(x_tiles, r, xs_ref, pos_ref[0, choice * tm + r], sem).start(priority=choice)
        return c

    lax.fori_loop(0, tm, start, 0, unroll=8)
    for _ in range(TOP_K):
        pltpu.make_async_copy(x_tiles, xs_ref.at[pl.ds(0, tm)], sem).wait()


def dispatch(x, pos, xs_buf, tm=ROW_TILE):
    rows, k = x.shape
    return pl.pallas_call(
        functools.partial(_dispatch_kernel, tm),
        grid=(rows // tm,),
        in_specs=[
            pl.BlockSpec((None, 1, TOP_K * tm), lambda i: (i, 0, 0), memory_space=pltpu.SMEM),
            pl.BlockSpec((tm, k), lambda i: (i, 0)),
            pl.BlockSpec(memory_space=pl.ANY),
        ],
        out_specs=pl.BlockSpec(memory_space=pl.ANY),
        out_shape=jax.ShapeDtypeStruct(xs_buf.shape, F32),
        scratch_shapes=[pltpu.VMEM((tm, SUBLANES, LANES), F32), pltpu.SemaphoreType.DMA(())],
        input_output_aliases={2: 0},
        compiler_params=_cparams("arbitrary"),
        name="moe_dispatch",
    )(pos, x, xs_buf)


def _moe_kernel(texp_ref, tvalid_ref, x_ref, g_ref, w1_ref, w3_ref, w2_ref, o_ref, h_scr, acc_scr):
    del texp_ref
    j = pl.program_id(1)
    last = pl.num_programs(1) - 1
    valid = tvalid_ref[pl.program_id(0)]

    @pl.when(valid > 0)
    def _():
        @pl.when(j == 0)
        def _():
            h_scr[...] = _rms(_as_rows(x_ref[...]), g_ref[...]).astype(BF16)
            acc_scr[...] = jnp.zeros_like(acc_scr)

        w1 = w1_ref[...].astype(BF16)
        w3 = w3_ref[...].astype(BF16)
        w2 = w2_ref[...].astype(BF16)
        half = x_ref.shape[0] // 2
        for lo in (0, half):

            @pl.when(valid > lo)
            def _():
                h = h_scr[lo : lo + half, :]
                a = _dot(h, w1)
                b = _dot(h, w3)
                acc_scr[lo : lo + half, :] += _dot((a * _sigmoid(a) * b).astype(BF16), w2)

        @pl.when(j == last)
        def _():
            o_ref[...] = _as_tiles(acc_scr[...])

    @pl.when(jnp.logical_and(valid == 0, j == last))
    def _():
        o_ref[...] = jnp.zeros_like(o_ref)


def moe_experts(xs, g, w1, w3, w2, layer, tile_expert, tile_valid, tm, tf):
    rows = xs.shape[0]
    k = SUBLANES * LANES
    tiles = lambda: pl.BlockSpec((tm, SUBLANES, LANES), lambda i, j, texp, tvalid: (i, 0, 0))
    f = w1.shape[-1]
    n_f = f // tf

    def jf(i, j, tvalid):
        return jnp.where(tvalid[i] > 0, j, n_f - 1)

    grid_spec = pltpu.PrefetchScalarGridSpec(
        num_scalar_prefetch=2,
        grid=(rows // tm, n_f),
        in_specs=[
            tiles(),
            pl.BlockSpec((1, k), lambda i, j, texp, tvalid: (0, 0)),
            pl.BlockSpec((None, None, k, tf), lambda i, j, texp, tvalid: (layer, texp[i], 0, jf(i, j, tvalid))),
            pl.BlockSpec((None, None, k, tf), lambda i, j, texp, tvalid: (layer, texp[i], 0, jf(i, j, tvalid))),
            pl.BlockSpec((None, None, tf, k), lambda i, j, texp, tvalid: (layer, texp[i], jf(i, j, tvalid), 0)),
        ],
        out_specs=tiles(),
        scratch_shapes=[pltpu.VMEM((tm, k), BF16), pltpu.VMEM((tm, k), F32)],
    )
    return pl.pallas_call(
        _moe_kernel,
        grid_spec=grid_spec,
        out_shape=jax.ShapeDtypeStruct(xs.shape, F32),
        compiler_params=_cparams("arbitrary", "arbitrary"),
        name="moe_experts",
    )(tile_expert, tile_valid, xs, g.reshape(1, k), w1, w3, w2)


def _combine_kernel(tm, pos_ref, x_ref, gate_ref, ys_ref, o_ref, y1_scr, y2_scr, sem):
    bufs = (y1_scr, y2_scr)

    def start(r, c):
        for choice in range(TOP_K):
            _row_copy(ys_ref, pos_ref[0, choice * tm + r], bufs[choice], r, sem).start(priority=choice)
        return c

    lax.fori_loop(0, tm, start, 0, unroll=8)
    for choice in range(TOP_K):
        pltpu.make_async_copy(ys_ref.at[pl.ds(0, tm)], bufs[choice], sem).wait()
    g = gate_ref[...]
    o_ref[...] = x_ref[...] + (g[:, 0:1] * _as_rows(y1_scr[...]) + g[:, 1:2] * _as_rows(y2_scr[...]))


def combine(x, gate, pos, ys, tm=ROW_TILE):
    rows, k = x.shape
    return pl.pallas_call(
        functools.partial(_combine_kernel, tm),
        grid=(rows // tm,),
        in_specs=[
            pl.BlockSpec((None, 1, TOP_K * tm), lambda i: (i, 0, 0), memory_space=pltpu.SMEM),
            pl.BlockSpec((tm, k), lambda i: (i, 0)),
            pl.BlockSpec((tm, LANES), lambda i: (i, 0)),
            pl.BlockSpec(memory_space=pl.ANY),
        ],
        out_specs=pl.BlockSpec((tm, k), lambda i: (i, 0)),
        out_shape=jax.ShapeDtypeStruct((rows, k), F32),
        scratch_shapes=[
            pltpu.VMEM((tm, SUBLANES, LANES), F32),
            pltpu.VMEM((tm, SUBLANES, LANES), F32),
            pltpu.SemaphoreType.DMA(()),
        ],
        compiler_params=_cparams("arbitrary"),
        name="moe_combine",
    )(pos, x, gate, ys)


def moe(x, g, w_router, w1, w3, w2, layer, xs_buf=None, tm_e=MOE_TILE, tf=MOE_FF_TILE):
    rows, k = x.shape
    n_tiles = -(-(TOP_K * rows + N_EXPERTS * (tm_e - 1)) // tm_e)
    gate, rank, expert, cnt = router(x, g, w_router)
    counts = cnt[:N_EXPERTS, 0].astype(jnp.int32)
    padded = ((counts + tm_e - 1) // tm_e) * tm_e
    ends = jnp.cumsum(padded)
    offs = ends - padded
    pos = rank
    for e in range(N_EXPERTS):
        pos = pos + jnp.where(expert == e, offs[e], 0)
    tile_start = jnp.arange(n_tiles, dtype=jnp.int32) * tm_e
    tile_expert = jnp.minimum(jnp.sum(ends[None, :] <= tile_start[:, None], axis=1), N_EXPERTS - 1).astype(jnp.int32)
    tile_valid = jnp.clip(counts[tile_expert] - (tile_start - offs[tile_expert]), 0, tm_e).astype(jnp.int32)
    if xs_buf is None:
        xs_buf = jnp.zeros((n_tiles * tm_e, SUBLANES, LANES), F32)
    xs = dispatch(x, pos, xs_buf)
    ys = moe_experts(xs, g, w1, w3, w2, layer, tile_expert, tile_valid, tm_e, tf)
    return combine(x, gate, pos, ys), xs


def _conv_rows(xpad, cw, cb):
    ext = xpad[...]
    y = None
    for k in range(CONV_W):
        back = CONV_W - 1 - k
        src = ext if back == 0 else pltpu.roll(ext, back, axis=0)
        term = src[HALO:] * cw[k : k + 1]
        y = term if y is None else y + term
    return y + cb


def _conv_taps(xpad, base, rows, step, cw, cb):
    y = xpad[base - 3 * step : base - 3 * step + rows, :] * cw[0:1]
    for k in range(1, CONV_W):
        lo = base - (CONV_W - 1 - k) * step
        y = y + xpad[lo : lo + rows, :] * cw[k : k + 1]
    return y + cb


def _lru_gates(xc, wa_ref, ba_ref, wx_ref, bx_ref, lam_ref):
    xcb = xc.astype(BF16)
    r_parts, i_parts = [], []
    for k in range(LRU_BLOCKS):
        blk = xcb[:, k * LRU_BW : (k + 1) * LRU_BW]
        r_parts.append(_dot(blk, wa_ref[k]))
        i_parts.append(_dot(blk, wx_ref[k]))
    r = _sigmoid(jnp.concatenate(r_parts, axis=1) + ba_ref[...])
    ig = _sigmoid(jnp.concatenate(i_parts, axis=1) + bx_ref[...])
    log_a = (-LRU_C * r) * _softplus(-lam_ref[...])
    a = jnp.exp(log_a)
    one_minus_a2 = -jnp.tanh(log_a) * (a * a + 1.0)
    u = jnp.sqrt(one_minus_a2) * (ig * xc)
    return a, u


def _lru_prompt_kernel(tt, proj_ref, cw_ref, cb_ref, wa_ref, ba_ref, wx_ref, bx_ref, lam_ref,
                       y_ref, hlast_ref, convn_ref, xpad, a_scr, u_scr, hs_scr, h_scr):
    t = pl.program_id(1)

    @pl.when(t == 0)
    def _():
        xpad[0:HALO, :] = jnp.zeros((HALO, D_RNN), F32)
        h_scr[...] = jnp.zeros_like(h_scr)

    @pl.when(t > 0)
    def _():
        xpad[0:HALO, :] = xpad[tt : tt + HALO, :]

    xpad[HALO : HALO + tt, :] = proj_ref[:, D_RNN:]
    xc = _conv_rows(xpad, cw_ref[...], cb_ref[...])
    a, u = _lru_gates(xc, wa_ref, ba_ref, wx_ref, bx_ref, lam_ref)
    a_scr[...] = a
    u_scr[...] = u

    def body(i, h):
        h = a_scr[pl.ds(i, 1), :] * h + u_scr[pl.ds(i, 1), :]
        hs_scr[pl.ds(i, 1), :] = h
        return h

    h_scr[...] = lax.fori_loop(0, tt, body, h_scr[...], unroll=8)
    y_ref[...] = hs_scr[...] * jax.nn.gelu(proj_ref[:, :D_RNN])

    @pl.when(t == pl.num_programs(1) - 1)
    def _():
        hlast_ref[...] = h_scr[...]
        convn_ref[...] = xpad[tt : tt + HALO, :]


def lru_prompt(proj, p, tt=256):
    n_t = SEQ // tt
    wspec = lambda shape: pl.BlockSpec(shape, lambda b, t: (0,) * len(shape))
    return pl.pallas_call(
        functools.partial(_lru_prompt_kernel, tt),
        grid=(BATCH, n_t),
        in_specs=[
            pl.BlockSpec((tt, 2 * D_RNN), lambda b, t: (b * n_t + t, 0)),
            wspec((CONV_W, D_RNN)), wspec((1, D_RNN)),
            wspec((LRU_BLOCKS, LRU_BW, LRU_BW)), wspec((1, D_RNN)),
            wspec((LRU_BLOCKS, LRU_BW, LRU_BW)), wspec((1, D_RNN)),
            wspec((1, D_RNN)),
        ],
        out_specs=[
            pl.BlockSpec((tt, D_RNN), lambda b, t: (b * n_t + t, 0)),
            pl.BlockSpec((None, 1, D_RNN), lambda b, t: (b, 0, 0)),
            pl.BlockSpec((None, HALO, D_RNN), lambda b, t: (b, 0, 0)),
        ],
        out_shape=[
            jax.ShapeDtypeStruct((N_PROMPT, D_RNN), F32),
            jax.ShapeDtypeStruct((BATCH, 1, D_RNN), F32),
            jax.ShapeDtypeStruct((BATCH, HALO, D_RNN), F32),
        ],
        scratch_shapes=[
            pltpu.VMEM((HALO + tt, D_RNN), F32),
            pltpu.VMEM((tt, D_RNN), F32),
            pltpu.VMEM((tt, D_RNN), F32),
            pltpu.VMEM((tt, D_RNN), F32),
            pltpu.VMEM((1, D_RNN), F32),
        ],
        compiler_params=_cparams("parallel", "arbitrary"),
        name="lru_prompt",
    )(proj, p["conv_w"], p["conv_b"], p["w_a"], p["b_a"], p["w_x"], p["b_x"], p["lam"])


def _lru_sample_kernel(proj_ref, convp_ref, hprev_ref, cw_ref, cb_ref, wa_ref, ba_ref, wx_ref, bx_ref, lam_ref,
                       y_ref, hlast_ref, convn_ref, xpad):
    hist = (CONV_W - 1) * DEC_BATCH
    xpad[0:hist, :] = convp_ref[...]
    xpad[hist:, :] = proj_ref[:, D_RNN:]
    xc = _conv_taps(xpad, hist, N_SAMPLE, DEC_BATCH, cw_ref[...], cb_ref[...])
    a, u = _lru_gates(xc, wa_ref, ba_ref, wx_ref, bx_ref, lam_ref)
    gate = jax.nn.gelu(proj_ref[:, :D_RNN])
    h = hprev_ref[...]
    for t in range(DEC_SEQ):
        rows = slice(t * DEC_BATCH, (t + 1) * DEC_BATCH)
        h = a[rows] * h + u[rows]
        y_ref[rows, :] = h * gate[rows]
    hlast_ref[...] = h
    convn_ref[...] = xpad[N_SAMPLE:, :]


def lru_sample(proj, conv_prev, h_prev, p):
    hist = (CONV_W - 1) * DEC_BATCH
    full = lambda shape: pl.BlockSpec(shape, lambda i: (0,) * len(shape))
    return pl.pallas_call(
        _lru_sample_kernel,
        grid=(1,),
        in_specs=[
            pl.BlockSpec((N_SAMPLE, 2 * D_RNN), lambda i: (SAMPLE_BLOCK, 0)),
            full((hist, D_RNN)), full((DEC_BATCH, D_RNN)),
            full((CONV_W, D_RNN)), full((1, D_RNN)),
            full((LRU_BLOCKS, LRU_BW, LRU_BW)), full((1, D_RNN)),
            full((LRU_BLOCKS, LRU_BW, LRU_BW)), full((1, D_RNN)),
            full((1, D_RNN)),
        ],
        out_specs=[
            full((N_SAMPLE, D_RNN)),
            full((DEC_BATCH, D_RNN)),
            full((hist, D_RNN)),
        ],
        out_shape=[
            jax.ShapeDtypeStruct((N_SAMPLE, D_RNN), F32),
            jax.ShapeDtypeStruct((DEC_BATCH, D_RNN), F32),
            jax.ShapeDtypeStruct((hist, D_RNN), F32),
        ],
        scratch_shapes=[pltpu.VMEM((hist + N_SAMPLE, D_RNN), F32)],
        compiler_params=_cparams("arbitrary"),
        name="lru_sample",
    )(proj, conv_prev, h_prev, p["conv_w"], p["conv_b"], p["w_a"], p["b_a"], p["w_x"], p["b_x"], p["lam"])


def _attend(q, k, v):
    outs = []
    for h in range(MEM_HEADS):
        hs = slice(h * MEM_HD, (h + 1) * MEM_HD)
        s = lax.dot_general(q[:, hs], k[:, hs], (((1,), (1,)), ((), ())), preferred_element_type=F32)
        s = s * (MEM_HD ** -0.5)
        e = jnp.exp(s - jnp.max(s, axis=-1, keepdims=True))
        p = e / jnp.sum(e, axis=-1, keepdims=True)
        outs.append(_dot(p.astype(BF16), v[:, hs]))
    return jnp.concatenate(outs, axis=1)


def _attn_prompt_kernel(q_ref, k_ref, v_ref, o_ref):
    o_ref[...] = _attend(q_ref[...], k_ref[...].astype(BF16), v_ref[...].astype(BF16)).astype(o_ref.dtype)


def attn_prompt(q, k, v, layer, tt=512):
    n_t = SEQ // tt
    return pl.pallas_call(
        _attn_prompt_kernel,
        grid=(BATCH, n_t),
        in_specs=[
            pl.BlockSpec((tt, D_MODEL), lambda b, t: (b * n_t + t, 0)),
            pl.BlockSpec((None, N_MEM, D_MODEL), lambda b, t: (layer, b, 0)),
            pl.BlockSpec((None, N_MEM, D_MODEL), lambda b, t: (layer, b, 0)),
        ],
        out_specs=pl.BlockSpec((tt, D_MODEL), lambda b, t: (b * n_t + t, 0)),
        out_shape=jax.ShapeDtypeStruct((N_PROMPT, D_MODEL), BF16),
        compiler_params=_cparams("parallel", "arbitrary"),
        name="attn_prompt",
    )(q, k, v)


def _attn_sample_kernel(bb, q_ref, k_ref, v_ref, o_ref):
    rows = MEM_HEADS * DEC_SEQ
    cols = N_MEM * MEM_HEADS
    row_head = lax.broadcasted_iota(jnp.int32, (rows, cols), 0) // DEC_SEQ
    col_head = lax.broadcasted_iota(jnp.int32, (rows, cols), 1) % MEM_HEADS
    same_head = row_head == col_head
    for i in range(bb):
        k2 = k_ref[i].reshape(cols, MEM_HD).astype(BF16)
        v2 = v_ref[i].reshape(cols, MEM_HD).astype(BF16)
        q = q_ref[i]
        qh = jnp.concatenate([q[:, h * MEM_HD : (h + 1) * MEM_HD] for h in range(MEM_HEADS)], axis=0)
        s = lax.dot_general(qh, k2, (((1,), (1,)), ((), ())), preferred_element_type=F32) * (MEM_HD ** -0.5)
        s = jnp.where(same_head, s, -jnp.inf)
        e = jnp.exp(s - jnp.max(s, axis=-1, keepdims=True))
        p = e / jnp.sum(e, axis=-1, keepdims=True)
        oh = _dot(p.astype(BF16), v2)
        o_ref[i] = jnp.concatenate([oh[h * DEC_SEQ : (h + 1) * DEC_SEQ] for h in range(MEM_HEADS)], axis=1)


def attn_sample(q, k, v, layer, bb=4):
    return pl.pallas_call(
        functools.partial(_attn_sample_kernel, bb),
        grid=(DEC_BATCH // bb,),
        in_specs=[
            pl.BlockSpec((bb, DEC_SEQ, D_MODEL), lambda i: (i, 0, 0)),
            pl.BlockSpec((None, bb, N_MEM, MEM_HEADS, MEM_HD), lambda i: (layer, i, 0, 0, 0)),
            pl.BlockSpec((None, bb, N_MEM, MEM_HEADS, MEM_HD), lambda i: (layer, i, 0, 0, 0)),
        ],
        out_specs=pl.BlockSpec((bb, DEC_SEQ, D_MODEL), lambda i: (i, 0, 0)),
        out_shape=jax.ShapeDtypeStruct((DEC_BATCH, DEC_SEQ, D_MODEL), F32),
        compiler_params=_cparams("parallel"),
        name="attn_sample",
    )(q, k, v)


def _ssd_gate_norm(y, z, ng):
    y = y * _silu(z)
    outs = []
    for g in range(SSD_GROUPS):
        yg = y[:, g * SSD_GROUP_W : (g + 1) * SSD_GROUP_W]
        outs.append(yg * lax.rsqrt(jnp.mean(yg * yg, axis=-1, keepdims=True) + EPS))
    return jnp.concatenate(outs, axis=1) * ng


def _ssd_prompt_kernel(q, z_ref, xbc_ref, dt_ref, cw_ref, cb_ref, dtb_ref, alog_ref, dskip_ref, ng_ref, e_ref,
                       y_ref, st_ref, convn_ref, xpad, s_t, y_scr):
    t = pl.program_id(1)

    @pl.when(t == 0)
    def _():
        xpad[0:HALO, :] = jnp.zeros((HALO, SSD_CONV_DIM), F32)
        s_t[...] = jnp.zeros_like(s_t)

    @pl.when(t > 0)
    def _():
        xpad[0:HALO, :] = xpad[q : q + HALO, :]

    xpad[HALO : HALO + q, :] = xbc_ref[...]
    xbc = _silu(_conv_rows(xpad, cw_ref[...], cb_ref[...]))
    xs = xbc[:, :D_INNER]

    dt = _softplus(dt_ref[...] + dtb_ref[...])
    adt = -jnp.exp(alog_ref[...]) * dt
    row_i = lax.broadcasted_iota(jnp.int32, (q, q), 0)
    col_i = lax.broadcasted_iota(jnp.int32, (q, q), 1)
    tril = row_i >= col_i
    a_cs = _dot_f32_rhs(jnp.where(tril, 1.0, 0.0).astype(BF16), adt)
    a_cs_t = a_cs.T
    a_end = a_cs[q - 1 : q, :]
    expand = e_ref[...]
    ecs_x = _expand_heads(jnp.exp(a_cs), expand)
    xb = (xs * _expand_heads(dt, expand)).astype(BF16)
    xd = (xs * _expand_heads(dt * jnp.exp(a_end - a_cs), expand)).astype(BF16)

    for g in range(SSD_GROUPS):
        gc = slice(g * SSD_GROUP_W, (g + 1) * SSD_GROUP_W)
        bg = xbc[:, D_INNER + g * SSD_STATE : D_INNER + (g + 1) * SSD_STATE]
        cg = xbc[:, D_INNER + SSD_GN + g * SSD_STATE : D_INNER + SSD_GN + (g + 1) * SSD_STATE].astype(BF16)
        cb_mat = lax.dot_general(cg, bg.astype(BF16), (((1,), (1,)), ((), ())), preferred_element_type=F32)
        sg = s_t[:, gc]
        y_scr[:, gc] = _dot(cg, sg.astype(BF16)) * ecs_x[:, gc]
        s_t[:, gc] = ecs_x[q - 1 : q, gc] * sg + _dot(bg.T.astype(BF16), xd[:, gc])
        for e in range(SSD_HEADS // SSD_GROUPS):
            h = g * (SSD_HEADS // SSD_GROUPS) + e
            hc = slice(h * SSD_HEADDIM, (h + 1) * SSD_HEADDIM)
            seg = a_cs[:, h : h + 1] - a_cs_t[h : h + 1, :]
            decay = jnp.where(tril, jnp.exp(jnp.minimum(seg, 0.0)), 0.0)
            y_scr[:, hc] += _dot((cb_mat * decay).astype(BF16), xb[:, hc])

    y = y_scr[...] + dskip_ref[...] * xs
    y_ref[...] = _ssd_gate_norm(y, z_ref[...], ng_ref[...]).astype(y_ref.dtype)

    @pl.when(t == pl.num_programs(1) - 1)
    def _():
        for j in range(D_INNER // LANES):
            st_ref[j * LANES : (j + 1) * LANES, :] = s_t[:, j * LANES : (j + 1) * LANES].T
        convn_ref[...] = xpad[q : q + HALO, :]


def ssd_prompt(z, xbc, dt, p, q=SSD_CHUNK):
    n_t = SEQ // q
    rows = lambda w: pl.BlockSpec((q, w), lambda b, t: (b * n_t + t, 0))
    wspec = lambda shape: pl.BlockSpec(shape, lambda b, t: (0,) * len(shape))
    return pl.pallas_call(
        functools.partial(_ssd_prompt_kernel, q),
        grid=(BATCH, n_t),
        in_specs=[
            rows(D_INNER), rows(SSD_CONV_DIM), rows(LANES),
            wspec((CONV_W, SSD_CONV_DIM)), wspec((1, SSD_CONV_DIM)),
            wspec((1, LANES)), wspec((1, LANES)), wspec((1, D_INNER)), wspec((1, D_INNER)),
            wspec((LANES, D_INNER)),
        ],
        out_specs=[
            rows(D_INNER),
            pl.BlockSpec((None, D_INNER, SSD_STATE), lambda b, t: (b, 0, 0)),
            pl.BlockSpec((None, HALO, SSD_CONV_DIM), lambda b, t: (b, 0, 0)),
        ],
        out_shape=[
            jax.ShapeDtypeStruct((N_PROMPT, D_INNER), BF16),
            jax.ShapeDtypeStruct((BATCH, D_INNER, SSD_STATE), F32),
            jax.ShapeDtypeStruct((BATCH, HALO, SSD_CONV_DIM), F32),
        ],
        scratch_shapes=[
            pltpu.VMEM((HALO + q, SSD_CONV_DIM), F32),
            pltpu.VMEM((SSD_STATE, D_INNER), F32),
            pltpu.VMEM((q, D_INNER), F32),
        ],
        compiler_params=_cparams("parallel", "arbitrary"),
        name="ssd_prompt",
    )(z, xbc, dt, p["conv_w"], p["conv_b"], p["dt_bias"], p["a_log"], p["d_skip"], p["norm_g"], p["expand"])


def _ssd_sample_pre_kernel(xbc_ref, dt_ref, convp_ref, cw_ref, cb_ref, dtb_ref, alog_ref, e_ref,
                           xs_ref, xdt_ref, dec_ref, bm_ref, cm_ref, convn_ref, xpad):
    hist = (CONV_W - 1) * DEC_BATCH
    xpad[0:hist, :] = convp_ref[...]
    xpad[hist:, :] = xbc_ref[...]
    xc = _conv_taps(xpad, hist, N_SAMPLE, DEC_BATCH, cw_ref[...], cb_ref[...])
    xbc = xc * _sigmoid(xc)
    xs = xbc[:, :D_INNER]
    dt = _softplus(dt_ref[...] + dtb_ref[...])
    adt = -jnp.exp(alog_ref[...]) * dt
    expand = e_ref[...]
    xs_ref[...] = xs
    xdt_ref[...] = xs * _dot_f32_lhs(dt, expand)
    dec_ref[...] = _dot_f32_lhs(jnp.exp(adt), expand)
    bm_ref[...] = xbc[:, D_INNER : D_INNER + SSD_GN]
    cm_ref[...] = xbc[:, D_INNER + SSD_GN :]
    convn_ref[...] = xpad[N_SAMPLE:, :]


def ssd_sample_pre(xbc, dt, conv_prev, p):
    hist = (CONV_W - 1) * DEC_BATCH
    full = lambda shape: pl.BlockSpec(shape, lambda i: (0,) * len(shape))
    out_w = [D_INNER, D_INNER, D_INNER, SSD_GN, SSD_GN]
    return pl.pallas_call(
        _ssd_sample_pre_kernel,
        grid=(1,),
        in_specs=[
            pl.BlockSpec((N_SAMPLE, SSD_CONV_DIM), lambda i: (SAMPLE_BLOCK, 0)),
            pl.BlockSpec((N_SAMPLE, LANES), lambda i: (SAMPLE_BLOCK, 0)),
            full((hist, SSD_CONV_DIM)),
            full((CONV_W, SSD_CONV_DIM)), full((1, SSD_CONV_DIM)),
            full((1, LANES)), full((1, LANES)), full((LANES, D_INNER)),
        ],
        out_specs=[full((N_SAMPLE, w)) for w in out_w] + [full((hist, SSD_CONV_DIM))],
        out_shape=[jax.ShapeDtypeStruct((N_SAMPLE, w), F32) for w in out_w]
        + [jax.ShapeDtypeStruct((hist, SSD_CONV_DIM), F32)],
        scratch_shapes=[pltpu.VMEM((hist + N_SAMPLE, SSD_CONV_DIM), F32)],
        compiler_params=_cparams("arbitrary"),
        name="ssd_sample_pre",
    )(xbc, dt, conv_prev, p["conv_w"], p["conv_b"], p["dt_bias"], p["a_log"], p["expand"])


def _ssd_recur_kernel(bb, slot, xdt_ref, dec_ref, bm_ref, cm_ref, st_in_ref, *rest):
    y_ref, st_out_ref = rest[-2:]
    nt = (((1,), (1,)), ((), ()))
    tn = (((0,), (0,)), ((), ()))
    last = DEC_SEQ - 1
    ones = jnp.ones((SUBLANES, SSD_STATE), BF16)
    for i in range(bb):
        xdt = xdt_ref[i]
        dec = dec_ref[i]
        decay = [dec[0:1]]
        for t in range(1, DEC_SEQ):
            decay.append(decay[t - 1] * dec[t : t + 1])
        prop = {(s, s): xdt[s : s + 1] for s in range(DEC_SEQ)}
        for t in range(1, DEC_SEQ):
            for s in range(t):
                prop[(t, s)] = dec[t : t + 1] * prop[(t - 1, s)]
        d3 = jnp.concatenate(list(_split3(decay[last])) + [jnp.zeros((SUBLANES - 3, D_INNER), BF16)], axis=0)
        p_last = jnp.concatenate([prop[(last, s)] for s in range(DEC_SEQ)], axis=0).astype(BF16)
        for g in range(SSD_GROUPS):
            gc = slice(g * SSD_GROUP_W, (g + 1) * SSD_GROUP_W)
            bg = bm_ref[i, :, g * SSD_STATE : (g + 1) * SSD_STATE].astype(BF16)
            cg = cm_ref[i, :, g * SSD_STATE : (g + 1) * SSD_STATE].astype(BF16)
            sg = st_in_ref[i, gc, :]
            c_h = lax.dot_general(cg, sg.astype(BF16), nt, preferred_element_type=F32)
            c_b = lax.dot_general(cg, bg, nt, preferred_element_type=F32)
            for t in range(DEC_SEQ):
                y = decay[t][:, gc] * c_h[t : t + 1]
                for s in range(t + 1):
                    y = y + jnp.broadcast_to(c_b[t : t + 1, s : s + 1], (1, SSD_GROUP_W)) * prop[(t, s)][:, gc]
                y_ref[i, t : t + 1, gc] = y
            d_col = lax.dot_general(d3[:, gc], ones, tn, preferred_element_type=F32)
            st_out_ref[slot, i, gc, :] = d_col * sg + lax.dot_general(p_last[:, gc], bg, tn, preferred_element_type=F32)
    for other in range(st_out_ref.shape[0]):
        if other != slot:
            st_out_ref[other] = jnp.zeros(st_out_ref.shape[1:], F32)


def ssd_recur(xdt, dec, bm, cm, state, layer, new_states=None, bb=2):
    n_layers = state.shape[0]
    seq = lambda w: pl.BlockSpec((bb, DEC_SEQ, w), lambda b: (b, 0, 0))
    st_in = pl.BlockSpec((None, bb, D_INNER, SSD_STATE), lambda b: (layer, b, 0, 0))
    in_specs = [seq(D_INNER), seq(D_INNER), seq(SSD_GN), seq(SSD_GN), st_in]
    args = [xdt, dec, bm, cm, state]
    if new_states is None:
        slot = layer
        st_out = pl.BlockSpec((n_layers, bb, D_INNER, SSD_STATE), lambda b: (0, b, 0, 0))
        aliases = {}
    else:
        slot = 0
        st_out = pl.BlockSpec((1, bb, D_INNER, SSD_STATE), lambda b: (layer, b, 0, 0))
        in_specs.append(pl.BlockSpec(memory_space=pl.ANY))
        args.append(new_states)
        aliases = {len(args) - 1: 1}
    return pl.pallas_call(
        functools.partial(_ssd_recur_kernel, bb, slot),
        grid=(DEC_BATCH // bb,),
        in_specs=in_specs,
        out_specs=[seq(D_INNER), st_out],
        out_shape=[
            jax.ShapeDtypeStruct((DEC_BATCH, DEC_SEQ, D_INNER), F32),
            jax.ShapeDtypeStruct((n_layers, DEC_BATCH, D_INNER, SSD_STATE), F32),
        ],
        input_output_aliases=aliases,
        compiler_params=_cparams("arbitrary"),
        name="ssd_recur",
    )(*args)


def _ssd_sample_post_kernel(yr_ref, xs_ref, z_ref, dskip_ref, ng_ref, y_ref):
    y = yr_ref[...] + dskip_ref[...] * xs_ref[...]
    y_ref[...] = _ssd_gate_norm(y, z_ref[...], ng_ref[...]).astype(y_ref.dtype)


def ssd_sample_post(y_raw, xs, z, p):
    full = lambda shape: pl.BlockSpec(shape, lambda i: (0,) * len(shape))
    return pl.pallas_call(
        _ssd_sample_post_kernel,
        grid=(1,),
        in_specs=[
            full((N_SAMPLE, D_INNER)), full((N_SAMPLE, D_INNER)),
            pl.BlockSpec((N_SAMPLE, D_INNER), lambda i: (SAMPLE_BLOCK, 0)),
            full((1, D_INNER)), full((1, D_INNER)),
        ],
        out_specs=full((N_SAMPLE, D_INNER)),
        out_shape=jax.ShapeDtypeStruct((N_SAMPLE, D_INNER), BF16),
        compiler_params=_cparams("arbitrary"),
        name="ssd_sample_post",
    )(y_raw, xs, z, p["d_skip"], p["norm_g"])


def _to_time_major(a):
    return jnp.swapaxes(a, 0, 1).reshape(a.shape[0] * a.shape[1], a.shape[2])


def _to_batch_major(a, t):
    return jnp.swapaxes(a.reshape(t, DEC_BATCH, a.shape[1]), 0, 1)


def _row(v):
    return v.reshape(1, -1).astype(F32)


def _pad_lanes(v):
    return jnp.pad(v.reshape(1, -1).astype(F32), ((0, 0), (0, LANES - v.shape[-1])))


def kernel(x_prompt, x_sample, state_lru_h, state_lru_conv, state_ssd, state_ssd_conv, cache_mem_k, cache_mem_v, mem_prompt, norm_mix, norm_mem, norm_memkv, norm_ffn, norm_final, lru_w_in, lru_conv_w, lru_conv_b, lru_w_a, lru_b_a, lru_w_x, lru_b_x, lru_lam, lru_w_out, ssd_w_in, ssd_conv_w, ssd_conv_b, ssd_dt_bias, ssd_a_log, ssd_d, ssd_norm_g, ssd_w_out, mem_w_q, mem_w_k, mem_w_v, mem_w_o, ffn_w1, ffn_w3, ffn_w2, moe_router, moe_w1, moe_w3, moe_w2):
    bf = lambda w: w.astype(BF16)
    x = concat_rows(x_prompt.reshape(N_PROMPT, D_MODEL), _to_time_major(x_sample))
    mem = mem_prompt.reshape(BATCH * N_MEM, D_MODEL)
    head_of_col = jnp.arange(D_INNER, dtype=jnp.int32) // SSD_HEADDIM
    expand = (jnp.arange(LANES, dtype=jnp.int32)[:, None] == head_of_col[None, :]).astype(BF16)

    p_lru_h, p_lru_conv, p_ssd, p_ssd_conv = [], [], [], []
    s_lru_h, s_lru_conv, s_ssd_conv = [], [], []
    s_ssd = None
    xs_buf = None
    hist = CONV_W - 1
    mk, mv, p_mk, p_mv = mem_kv(mem, norm_memkv, mem_w_k, mem_w_v)
    for i in range(DEPTH):
        j = i // 2
        if i % 2 == 0:
            p = dict(conv_w=lru_conv_w[j], conv_b=_row(lru_conv_b[j]), w_a=bf(lru_w_a[j]), b_a=_row(lru_b_a[j]),
                     w_x=bf(lru_w_x[j]), b_x=_row(lru_b_x[j]), lam=_row(lru_lam[j]))
            (proj,) = norm_matmul(x, norm_mix[i], [bf(lru_w_in[j])], [F32])
            y_p, h_p, c_p = lru_prompt(proj, p)
            y_s, h_s, c_s = lru_sample(proj, _to_time_major(state_lru_conv[j]), state_lru_h[j], p)
            p_lru_h.append(h_p.reshape(BATCH, D_RNN))
            p_lru_conv.append(c_p[:, HALO - hist :, :])
            s_lru_h.append(h_s)
            s_lru_conv.append(_to_batch_major(c_s, hist))
            x = matmul_residual(y_p, y_s, bf(lru_w_out[j]), x)
        else:
            w_in = ssd_w_in[j]
            w_z = bf(w_in[:, :D_INNER])
            w_xbc = bf(w_in[:, D_INNER : D_INNER + SSD_CONV_DIM])
            w_dt = bf(jnp.pad(w_in[:, D_INNER + SSD_CONV_DIM :], ((0, 0), (0, LANES - SSD_HEADS))))
            p = dict(conv_w=ssd_conv_w[j], conv_b=_row(ssd_conv_b[j]), dt_bias=_pad_lanes(ssd_dt_bias[j]),
                     a_log=_pad_lanes(ssd_a_log[j]), d_skip=_row(jnp.repeat(ssd_d[j], SSD_HEADDIM)),
                     norm_g=_row(ssd_norm_g[j]), expand=expand)
            z, xbc, dt = norm_matmul(x, norm_mix[i], [w_z, w_xbc, w_dt], [F32, F32, F32], tm=256)
            y_p, st_p, c_p = ssd_prompt(z, xbc, dt, p)
            xs_s, xdt_s, dec_s, bm_s, cm_s, c_s = ssd_sample_pre(xbc, dt, _to_time_major(state_ssd_conv[j]), p)
            y_raw, s_ssd = ssd_recur(
                _to_batch_major(xdt_s, DEC_SEQ), _to_batch_major(dec_s, DEC_SEQ),
                _to_batch_major(bm_s, DEC_SEQ), _to_batch_major(cm_s, DEC_SEQ),
                state_ssd.reshape(-1, DEC_BATCH, D_INNER, SSD_STATE), j, s_ssd)
            y_s = ssd_sample_post(_to_time_major(y_raw), xs_s, z, p)
            p_ssd.append(st_p.reshape(BATCH, SSD_HEADS, SSD_HEADDIM, SSD_STATE))
            p_ssd_conv.append(c_p[:, HALO - hist :, :])
            s_ssd_conv.append(_to_batch_major(c_s, hist))
            x = matmul_residual(y_p, y_s, bf(ssd_w_out[j]), x)

        (qp,) = norm_matmul(x, norm_mem[i], [bf(mem_w_q[i])], [BF16])
        o = attn_prompt(qp, mk, mv, i)
        o_s = attn_sample(
            _to_batch_major(qp[N_PROMPT:], DEC_SEQ),
            cache_mem_k, cache_mem_v, i)
        x = matmul_residual(o, _to_time_major(o_s), bf(mem_w_o[i]), x)

        if i % 2 == 0:
            x = ffn(x, norm_ffn[i], bf(ffn_w1[j]), bf(ffn_w3[j]), bf(ffn_w2[j]), tf=D_FF // 2)
        else:
            x, xs_buf = moe(x, norm_ffn[i], moe_router[j], moe_w1, moe_w3, moe_w2, j, xs_buf)

    y_prompt = rmsnorm_rows(x, norm_final, 0, N_PROMPT // ROW_TILE).reshape(BATCH, SEQ, D_MODEL)
    y_sample = _to_batch_major(rmsnorm_rows(x, norm_final, N_PROMPT // ROW_TILE, N_SAMPLE // ROW_TILE), DEC_SEQ)
    return (y_prompt, y_sample,
            jnp.stack(p_lru_h), jnp.stack(p_lru_conv), jnp.stack(p_ssd), jnp.stack(p_ssd_conv),
            p_mk, p_mv,
            jnp.stack(s_lru_h), jnp.stack(s_lru_conv), s_ssd.reshape(state_ssd.shape), jnp.stack(s_ssd_conv))
```

```python
import functools
import math

import jax
import jax.numpy as jnp
from jax import lax
from jax.experimental import pallas as pl
from jax.experimental.pallas import tpu as pltpu

F32 = jnp.float32
BF16 = jnp.bfloat16

D_MODEL = 1024
BATCH = 8
SEQ = 2048
DEPTH = 4
DEC_BATCH = 128
DEC_SEQ = 4
CONV_W = 4
EPS = 1e-6
D_RNN = D_MODEL
LRU_BLOCKS = 8
LRU_BW = D_RNN // LRU_BLOCKS
LRU_C = 8.0
D_INNER = 2 * D_MODEL
SSD_HEADDIM = 64
SSD_HEADS = D_INNER // SSD_HEADDIM
SSD_GROUPS = 4
SSD_GROUP_W = D_INNER // SSD_GROUPS
SSD_STATE = 128
SSD_GN = SSD_GROUPS * SSD_STATE
SSD_CONV_DIM = D_INNER + 2 * SSD_GN
SSD_CHUNK = 128
N_MEM = 256
MEM_HEADS = 4
MEM_HD = D_MODEL // MEM_HEADS
D_FF = 2816
N_EXPERTS = 8
TOP_K = 2
D_FF_EXPERT = 3584

LANES = 128
SUBLANES = 8
VMEM_LIMIT_BYTES = 56 * 1024 * 1024

N_PROMPT = BATCH * SEQ
N_SAMPLE = DEC_BATCH * DEC_SEQ
N_ROWS = N_PROMPT + N_SAMPLE
ROW_TILE = 512
SAMPLE_BLOCK = N_PROMPT // N_SAMPLE
HALO = SUBLANES
MOE_TILE = 1024
MOE_FF_TILE = 512


def _cparams(*sem):
    return pltpu.CompilerParams(dimension_semantics=sem, vmem_limit_bytes=VMEM_LIMIT_BYTES)


def _rms(x, g):
    return x * lax.rsqrt(jnp.mean(x * x, axis=-1, keepdims=True) + EPS) * g


def _sigmoid(x):
    return 1.0 / (1.0 + jnp.exp(-x))


def _silu(x):
    h = 0.5 * x
    return h + h * jnp.tanh(h)


def _softplus(x):
    return jnp.maximum(x, 0.0) + jnp.log1p(jnp.exp(-jnp.abs(x)))


def _split3(x):
    a = x.astype(BF16)
    r = x - a.astype(F32)
    b = r.astype(BF16)
    c = (r - b.astype(F32)).astype(BF16)
    return a, b, c


def _dot(a, b):
    return jnp.dot(a, b, preferred_element_type=F32)


def _dot_f32_lhs(x, m):
    a, b, c = _split3(x)
    return _dot(a, m) + _dot(b, m) + _dot(c, m)


def _expand_heads(x, expand):
    hi = x.astype(BF16)
    lo = (x - hi.astype(F32)).astype(BF16)
    return _dot(hi, expand) + _dot(lo, expand)


def _dot_f32_rhs(m, x):
    a, b, c = _split3(x)
    return _dot(m, a) + _dot(m, b) + _dot(m, c)


def _norm_matmul_kernel(n_w, x_ref, g_ref, *refs):
    h = _rms(x_ref[...], g_ref[...]).astype(BF16)
    for w_ref, o_ref in zip(refs[:n_w], refs[n_w:]):
        o_ref[...] = _dot(h, w_ref[...]).astype(o_ref.dtype)


def norm_matmul(x, g, ws, out_dtypes, tm=ROW_TILE):
    rows, k = x.shape
    in_specs = [pl.BlockSpec((tm, k), lambda i: (i, 0)), pl.BlockSpec((1, k), lambda i: (0, 0))]
    in_specs += [pl.BlockSpec(w.shape, lambda i: (0, 0)) for w in ws]
    out_specs = [pl.BlockSpec((tm, w.shape[1]), lambda i: (i, 0)) for w in ws]
    out_shape = [jax.ShapeDtypeStruct((rows, w.shape[1]), dt) for w, dt in zip(ws, out_dtypes)]
    return pl.pallas_call(
        functools.partial(_norm_matmul_kernel, len(ws)),
        grid=(rows // tm,),
        in_specs=in_specs,
        out_specs=out_specs,
        out_shape=out_shape,
        compiler_params=_cparams("parallel"),
        name="norm_matmul",
    )(x, g.reshape(1, k), *ws)


def _concat_rows_kernel(n_p, a_ref, b_ref, o_ref):
    i = pl.program_id(0)

    @pl.when(i < n_p)
    def _():
        o_ref[...] = a_ref[...]

    @pl.when(i >= n_p)
    def _():
        o_ref[...] = b_ref[...]


def concat_rows(a, b, tm=ROW_TILE):
    k = a.shape[1]
    n_p = a.shape[0] // tm
    n_s = b.shape[0] // tm
    return pl.pallas_call(
        functools.partial(_concat_rows_kernel, n_p),
        grid=(n_p + n_s,),
        in_specs=[
            pl.BlockSpec((tm, k), lambda i: (jnp.minimum(i, n_p - 1), 0)),
            pl.BlockSpec((tm, k), lambda i: (jnp.maximum(i - n_p, 0), 0)),
        ],
        out_specs=pl.BlockSpec((tm, k), lambda i: (i, 0)),
        out_shape=jax.ShapeDtypeStruct((a.shape[0] + b.shape[0], k), a.dtype),
        compiler_params=_cparams("arbitrary"),
        name="concat_rows",
    )(a, b)


def _mem_kv_kernel(bt, m_ref, g_ref, wk_ref, wv_ref, k2_ref, v2_ref, k4_ref, v4_ref):
    h = _rms(m_ref[...], g_ref[...]).astype(BF16)
    for w_ref, o2_ref, o4_ref in ((wk_ref, k2_ref, k4_ref), (wv_ref, v2_ref, v4_ref)):
        r = _dot(h, w_ref[...].astype(BF16))
        o2_ref[...] = r
        o4_ref[...] = r.reshape(bt, N_MEM, MEM_HEADS, MEM_HD)


def mem_kv(mem, g, w_k, w_v, bt=2):
    rows, k = mem.shape
    tm = bt * N_MEM
    flat = lambda: pl.BlockSpec((None, tm, k), lambda l, i: (l, i, 0))
    heads = lambda: pl.BlockSpec((None, bt, N_MEM, MEM_HEADS, MEM_HD), lambda l, i: (l, i, 0, 0, 0))
    wspec = lambda: pl.BlockSpec((None, k, k), lambda l, i: (l, 0, 0))
    flat_shape = jax.ShapeDtypeStruct((DEPTH, rows, k), F32)
    heads_shape = jax.ShapeDtypeStruct((DEPTH, BATCH, N_MEM, MEM_HEADS, MEM_HD), F32)
    return pl.pallas_call(
        functools.partial(_mem_kv_kernel, bt),
        grid=(DEPTH, rows // tm),
        in_specs=[
            pl.BlockSpec((tm, k), lambda l, i: (i, 0)),
            pl.BlockSpec((None, 1, k), lambda l, i: (l, 0, 0)),
            wspec(), wspec(),
        ],
        out_specs=[flat(), flat(), heads(), heads()],
        out_shape=[flat_shape, flat_shape, heads_shape, heads_shape],
        compiler_params=_cparams("arbitrary", "arbitrary"),
        name="mem_kv",
    )(mem, g.reshape(DEPTH, 1, k), w_k, w_v)


def _two_part_specs(a_prompt, a_sample, tm):
    n_p = a_prompt.shape[0] // tm
    specs = [
        pl.BlockSpec((tm, a_prompt.shape[1]), lambda i, *_: (jnp.minimum(i, n_p - 1), 0)),
        pl.BlockSpec((tm, a_sample.shape[1]), lambda i, *_: (jnp.maximum(i - n_p, 0), 0)),
    ]
    return specs, n_p


def _on_row_part(n_p, prompt_ref, sample_ref, fn):
    i = pl.program_id(0)

    @pl.when(i < n_p)
    def _():
        fn(prompt_ref[...])

    @pl.when(i >= n_p)
    def _():
        fn(sample_ref[...])


def _out_proj_q_kernel(n_p, yp_ref, ys_ref, w_ref, r_ref, g_ref, wq_ref, o_ref, q_ref):
    def finish(y):
        x_new = r_ref[...] + _dot(y.astype(BF16), w_ref[...])
        o_ref[...] = x_new
        q_ref[...] = _dot(_rms(x_new, g_ref[...]).astype(BF16), wq_ref[...]).astype(q_ref.dtype)

    _on_row_part(n_p, yp_ref, ys_ref, finish)


def out_proj_q(y_prompt, y_sample, w, res, g, w_q, tm=ROW_TILE):
    k, n = w.shape
    y_specs, n_p = _two_part_specs(y_prompt, y_sample, tm)
    full = lambda shape: pl.BlockSpec(shape, lambda i: (0, 0))
    rows = lambda: pl.BlockSpec((tm, n), lambda i: (i, 0))
    return pl.pallas_call(
        functools.partial(_out_proj_q_kernel, n_p),
        grid=(res.shape[0] // tm,),
        in_specs=y_specs + [full((k, n)), rows(), full((1, n)), full((n, n))],
        out_specs=[rows(), rows()],
        out_shape=[jax.ShapeDtypeStruct(res.shape, F32), jax.ShapeDtypeStruct(res.shape, BF16)],
        compiler_params=_cparams("arbitrary"),
        name="out_proj_q",
    )(y_prompt, y_sample, w, res, g.reshape(1, n), w_q)


def _rmsnorm_kernel(x_ref, g_ref, o_ref):
    o_ref[...] = _rms(x_ref[...], g_ref[...])


def rmsnorm_rows(x, g, first_block, n_blocks, tm=ROW_TILE):
    k = x.shape[1]
    return pl.pallas_call(
        _rmsnorm_kernel,
        grid=(n_blocks,),
        in_specs=[pl.BlockSpec((tm, k), lambda i: (i + first_block, 0)), pl.BlockSpec((1, k), lambda i: (0, 0))],
        out_specs=pl.BlockSpec((tm, k), lambda i: (i, 0)),
        out_shape=jax.ShapeDtypeStruct((n_blocks * tm, k), F32),
        compiler_params=_cparams("parallel"),
        name="final_norm",
    )(x, g.reshape(1, k))


def _swiglu_partial(h, w1_ref, w3_ref, w2_ref):
    a = _dot(h, w1_ref[...])
    b = _dot(h, w3_ref[...])
    return _dot((a * _sigmoid(a) * b).astype(BF16), w2_ref[...])


def _ffn_kernel(n_p, x_ref, op_ref, os_ref, wo_ref, g_ref, w1_ref, w3_ref, w2_ref, o_ref, xn_scr, h_scr, acc_scr):
    j = pl.program_id(1)

    @pl.when(j == 0)
    def _():
        def start(o):
            xn = x_ref[...] + _dot(o.astype(BF16), wo_ref[...])
            xn_scr[...] = xn
            h_scr[...] = _rms(xn, g_ref[...]).astype(BF16)

        _on_row_part(n_p, op_ref, os_ref, start)
        acc_scr[...] = jnp.zeros_like(acc_scr)

    acc_scr[...] += _swiglu_partial(h_scr[...], w1_ref, w3_ref, w2_ref)

    @pl.when(j == pl.num_programs(1) - 1)
    def _():
        o_ref[...] = xn_scr[...] + acc_scr[...]


def attn_out_ffn(x, o_prompt, o_sample, w_o, g, w1, w3, w2, tf, tm=ROW_TILE):
    rows, k = x.shape
    f = w1.shape[1]
    o_specs, n_p = _two_part_specs(o_prompt, o_sample, tm)
    return pl.pallas_call(
        functools.partial(_ffn_kernel, n_p),
        grid=(rows // tm, f // tf),
        in_specs=[pl.BlockSpec((tm, k), lambda i, j: (i, 0))] + o_specs + [
            pl.BlockSpec((k, k), lambda i, j: (0, 0)),
            pl.BlockSpec((1, k), lambda i, j: (0, 0)),
            pl.BlockSpec((k, tf), lambda i, j: (0, j)),
            pl.BlockSpec((k, tf), lambda i, j: (0, j)),
            pl.BlockSpec((tf, k), lambda i, j: (j, 0)),
        ],
        out_specs=pl.BlockSpec((tm, k), lambda i, j: (i, 0)),
        out_shape=jax.ShapeDtypeStruct((rows, k), F32),
        scratch_shapes=[pltpu.VMEM((tm, k), F32), pltpu.VMEM((tm, k), BF16), pltpu.VMEM((tm, k), F32)],
        compiler_params=_cparams("arbitrary", "arbitrary"),
        name="attn_out_ffn",
    )(x, o_prompt, o_sample, w_o, g.reshape(1, k), w1, w3, w2)


def _router_kernel(tm, n_p, x_ref, op_ref, os_ref, wo_ref, g_ref, wr_ref,
                   xn_ref, gate_ref, rank_ref, exp_ref, cnt_ref, carry):
    @pl.when(pl.program_id(0) == 0)
    def _():
        carry[...] = jnp.zeros_like(carry)

    def add_attention(o):
        xn_ref[...] = x_ref[...] + _dot(o.astype(BF16), wo_ref[...])

    _on_row_part(n_p, op_ref, os_ref, add_attention)
    h = _rms(xn_ref[...], g_ref[...])
    h1 = h.astype(BF16)
    h2 = (h - h1.astype(F32)).astype(BF16)
    w = wr_ref[...]
    w1 = w.astype(BF16)
    w2 = (w - w1.astype(F32)).astype(BF16)
    logits = _dot(h1, w1) + _dot(h1, w2) + _dot(h2, w1)
    lane = lax.broadcasted_iota(jnp.int32, logits.shape, 1).astype(F32)
    neg = jnp.float32(-jnp.inf)
    logits = jnp.where(lane < N_EXPERTS, logits, neg)
    m1 = jnp.max(logits, axis=-1, keepdims=True)
    i1 = jnp.min(jnp.where(logits == m1, lane, float(LANES)), axis=-1, keepdims=True)
    rest = jnp.where(lane == i1, neg, logits)
    m2 = jnp.max(rest, axis=-1, keepdims=True)
    i2 = jnp.min(jnp.where(rest == m2, lane, float(LANES)), axis=-1, keepdims=True)
    e2 = jnp.exp(m2 - m1)
    den = 1.0 + e2
    gate_ref[...] = jnp.where(lane == 0.0, 1.0 / den, 0.0) + jnp.where(lane == 1.0, e2 / den, 0.0)

    oh1_t = jnp.where(lane == i1, 1.0, 0.0).T
    oh2_t = jnp.where(lane == i2, 1.0, 0.0).T
    oh_t = oh1_t + oh2_t
    src = lax.broadcasted_iota(jnp.int32, (tm, tm), 0)
    dst = lax.broadcasted_iota(jnp.int32, (tm, tm), 1)
    earlier = jnp.where(src < dst, 1.0, 0.0).astype(BF16)
    before = _dot(oh_t.astype(BF16), earlier)
    base = jnp.concatenate([carry[...]] * (tm // LANES), axis=1) + before
    expert_id = lax.broadcasted_iota(jnp.int32, (LANES, tm), 0).astype(F32)
    col_sum = lambda a: jnp.sum(a, axis=0, keepdims=True)
    rank_ref[...] = jnp.concatenate([col_sum(oh1_t * base), col_sum(oh2_t * base)], axis=1).astype(jnp.int32)
    exp_ref[...] = jnp.concatenate([col_sum(oh1_t * expert_id), col_sum(oh2_t * expert_id)], axis=1).astype(jnp.int32)
    carry[...] += jnp.broadcast_to(jnp.sum(oh_t, axis=1, keepdims=True), carry.shape)
    cnt_ref[...] = carry[...]


def attn_out_router(x, o_prompt, o_sample, w_o, g, w_router, tm=ROW_TILE):
    rows, k = x.shape
    n_t = rows // tm
    wr = jnp.pad(w_router, ((0, 0), (0, LANES - N_EXPERTS)))
    o_specs, n_p = _two_part_specs(o_prompt, o_sample, tm)
    return pl.pallas_call(
        functools.partial(_router_kernel, tm, n_p),
        grid=(n_t,),
        in_specs=[pl.BlockSpec((tm, k), lambda i: (i, 0))] + o_specs + [
            pl.BlockSpec((k, k), lambda i: (0, 0)),
            pl.BlockSpec((1, k), lambda i: (0, 0)),
            pl.BlockSpec((k, LANES), lambda i: (0, 0)),
        ],
        out_specs=[
            pl.BlockSpec((tm, k), lambda i: (i, 0)),
            pl.BlockSpec((tm, LANES), lambda i: (i, 0)),
            pl.BlockSpec((None, 1, TOP_K * tm), lambda i: (i, 0, 0)),
            pl.BlockSpec((None, 1, TOP_K * tm), lambda i: (i, 0, 0)),
            pl.BlockSpec((LANES, LANES), lambda i: (0, 0)),
        ],
        out_shape=[
            jax.ShapeDtypeStruct((rows, k), F32),
            jax.ShapeDtypeStruct((rows, LANES), F32),
            jax.ShapeDtypeStruct((n_t, 1, TOP_K * tm), jnp.int32),
            jax.ShapeDtypeStruct((n_t, 1, TOP_K * tm), jnp.int32),
            jax.ShapeDtypeStruct((LANES, LANES), F32),
        ],
        scratch_shapes=[pltpu.VMEM((LANES, LANES), F32)],
        compiler_params=_cparams("arbitrary"),
        name="attn_out_router",
    )(x, o_prompt, o_sample, w_o, g.reshape(1, k), wr)


def _as_tiles(a):
    return a.reshape(a.shape[0], SUBLANES, LANES)


def _as_rows(a):
    return a.reshape(a.shape[0], SUBLANES * LANES)


def _row_copy(src_ref, src_row, dst_ref, dst_row, sem):
    return pltpu.make_async_copy(src_ref.at[src_row], dst_ref.at[dst_row], sem)


def _dispatch_kernel(tm, pos_ref, x_ref, xs_in_ref, xs_ref, x_tiles, sem):
    del xs_in_ref
    x_tiles[...] = _as_tiles(x_ref[...])

    def start(r, c):
        for choice in range(TOP_K):
            _row_copy(x_tiles, r, xs_ref, pos_ref[0, choice * tm + r], sem).start(priority=choice)
        return c

    lax.fori_loop(0, tm, start, 0, unroll=8)
    for _ in range(TOP_K):
        pltpu.make_async_copy(x_tiles, xs_ref.at[pl.ds(0, tm)], sem).wait()


def dispatch(x, pos, xs_buf, tm=ROW_TILE):
    rows, k = x.shape
    return pl.pallas_call(
        functools.partial(_dispatch_kernel, tm),
        grid=(rows // tm,),
        in_specs=[
            pl.BlockSpec((None, 1, TOP_K * tm), lambda i: (i, 0, 0), memory_space=pltpu.SMEM),
            pl.BlockSpec((tm, k), lambda i: (i, 0)),
            pl.BlockSpec(memory_space=pl.ANY),
        ],
        out_specs=pl.BlockSpec(memory_space=pl.ANY),
        out_shape=jax.ShapeDtypeStruct(xs_buf.shape, F32),
        scratch_shapes=[pltpu.VMEM((tm, SUBLANES, LANES), F32), pltpu.SemaphoreType.DMA(())],
        input_output_aliases={2: 0},
        compiler_params=_cparams("arbitrary"),
        name="moe_dispatch",
    )(pos, x, xs_buf)


def _moe_kernel(texp_ref, tvalid_ref, x_ref, g_ref, w1_ref, w3_ref, w2_ref, o_ref, h_scr, acc_scr):
    del texp_ref
    j = pl.program_id(1)
    last = pl.num_programs(1) - 1
    valid = tvalid_ref[pl.program_id(0)]

    @pl.when(valid > 0)
    def _():
        @pl.when(j == 0)
        def _():
            h_scr[...] = _rms(_as_rows(x_ref[...]), g_ref[...]).astype(BF16)
            acc_scr[...] = jnp.zeros_like(acc_scr)

        w1 = w1_ref[...].astype(BF16)
        w3 = w3_ref[...].astype(BF16)
        w2 = w2_ref[...].astype(BF16)
        half = x_ref.shape[0] // 2
        for lo in (0, half):

            @pl.when(valid > lo)
            def _():
                h = h_scr[lo : lo + half, :]
                a = _dot(h, w1)
                b = _dot(h, w3)
                acc_scr[lo : lo + half, :] += _dot((a * _sigmoid(a) * b).astype(BF16), w2)

        @pl.when(j == last)
        def _():
            o_ref[...] = _as_tiles(acc_scr[...])

    @pl.when(jnp.logical_and(valid == 0, j == last))
    def _():
        o_ref[...] = jnp.zeros_like(o_ref)


def moe_experts(xs, g, w1, w3, w2, layer, tile_expert, tile_valid, tm, tf):
    rows = xs.shape[0]
    k = SUBLANES * LANES
    tiles = lambda: pl.BlockSpec((tm, SUBLANES, LANES), lambda i, j, texp, tvalid: (i, 0, 0))
    f = w1.shape[-1]
    n_f = f // tf

    def jf(i, j, tvalid):
        return jnp.where(tvalid[i] > 0, j, n_f - 1)

    grid_spec = pltpu.PrefetchScalarGridSpec(
        num_scalar_prefetch=2,
        grid=(rows // tm, n_f),
        in_specs=[
            tiles(),
            pl.BlockSpec((1, k), lambda i, j, texp, tvalid: (0, 0)),
            pl.BlockSpec((None, None, k, tf), lambda i, j, texp, tvalid: (layer, texp[i], 0, jf(i, j, tvalid))),
            pl.BlockSpec((None, None, k, tf), lambda i, j, texp, tvalid: (layer, texp[i], 0, jf(i, j, tvalid))),
            pl.BlockSpec((None, None, tf, k), lambda i, j, texp, tvalid: (layer, texp[i], jf(i, j, tvalid), 0)),
        ],
        out_specs=tiles(),
        scratch_shapes=[pltpu.VMEM((tm, k), BF16), pltpu.VMEM((tm, k), F32)],
    )
    return pl.pallas_call(
        _moe_kernel,
        grid_spec=grid_spec,
        out_shape=jax.ShapeDtypeStruct(xs.shape, F32),
        compiler_params=_cparams("arbitrary", "arbitrary"),
        name="moe_experts",
    )(tile_expert, tile_valid, xs, g.reshape(1, k), w1, w3, w2)


def _combine_kernel(tm, pos_ref, x_ref, gate_ref, ys_ref, o_ref, y1_scr, y2_scr, sem):
    bufs = (y1_scr, y2_scr)

    def start(r, c):
        for choice in range(TOP_K):
            _row_copy(ys_ref, pos_ref[0, choice * tm + r], bufs[choice], r, sem).start(priority=choice)
        return c

    lax.fori_loop(0, tm, start, 0, unroll=8)
    for choice in range(TOP_K):
        pltpu.make_async_copy(ys_ref.at[pl.ds(0, tm)], bufs[choice], sem).wait()
    g = gate_ref[...]
    o_ref[...] = x_ref[...] + (g[:, 0:1] * _as_rows(y1_scr[...]) + g[:, 1:2] * _as_rows(y2_scr[...]))


def combine(x, gate, pos, ys, tm=ROW_TILE):
    rows, k = x.shape
    return pl.pallas_call(
        functools.partial(_combine_kernel, tm),
        grid=(rows // tm,),
        in_specs=[
            pl.BlockSpec((None, 1, TOP_K * tm), lambda i: (i, 0, 0), memory_space=pltpu.SMEM),
            pl.BlockSpec((tm, k), lambda i: (i, 0)),
            pl.BlockSpec((tm, LANES), lambda i: (i, 0)),
            pl.BlockSpec(memory_space=pl.ANY),
        ],
        out_specs=pl.BlockSpec((tm, k), lambda i: (i, 0)),
        out_shape=jax.ShapeDtypeStruct((rows, k), F32),
        scratch_shapes=[
            pltpu.VMEM((tm, SUBLANES, LANES), F32),
            pltpu.VMEM((tm, SUBLANES, LANES), F32),
            pltpu.SemaphoreType.DMA(()),
        ],
        compiler_params=_cparams("arbitrary"),
        name="moe_combine",
    )(pos, x, gate, ys)


def attn_out_moe(x, o_prompt, o_sample, w_o, g, w_router, w1, w3, w2, layer, xs_buf=None,
                 tm_e=MOE_TILE, tf=MOE_FF_TILE):
    rows = x.shape[0]
    n_tiles = -(-(TOP_K * rows + N_EXPERTS * (tm_e - 1)) // tm_e)
    x, gate, rank, expert, cnt = attn_out_router(x, o_prompt, o_sample, w_o, g, w_router)
    counts = cnt[:N_EXPERTS, 0].astype(jnp.int32)
    padded = ((counts + tm_e - 1) // tm_e) * tm_e
    ends = jnp.cumsum(padded)
    offs = ends - padded
    pos = rank
    for e in range(N_EXPERTS):
        pos = pos + jnp.where(expert == e, offs[e], 0)
    tile_start = jnp.arange(n_tiles, dtype=jnp.int32) * tm_e
    tile_expert = jnp.minimum(jnp.sum(ends[None, :] <= tile_start[:, None], axis=1), N_EXPERTS - 1).astype(jnp.int32)
    tile_valid = jnp.clip(counts[tile_expert] - (tile_start - offs[tile_expert]), 0, tm_e).astype(jnp.int32)
    if xs_buf is None:
        xs_buf = jnp.zeros((n_tiles * tm_e, SUBLANES, LANES), F32)
    xs = dispatch(x, pos, xs_buf)
    ys = moe_experts(xs, g, w1, w3, w2, layer, tile_expert, tile_valid, tm_e, tf)
    return combine(x, gate, pos, ys), xs


def _conv_rows(xpad, cw, cb):
    ext = xpad[...]
    y = None
    for k in range(CONV_W):
        back = CONV_W - 1 - k
        src = ext if back == 0 else pltpu.roll(ext, back, axis=0)
        term = src[HALO:] * cw[k : k + 1]
        y = term if y is None else y + term
    return y + cb


def _conv_taps(xpad, base, rows, step, cw, cb):
    y = xpad[base - 3 * step : base - 3 * step + rows, :] * cw[0:1]
    for k in range(1, CONV_W):
        lo = base - (CONV_W - 1 - k) * step
        y = y + xpad[lo : lo + rows, :] * cw[k : k + 1]
    return y + cb


def _lru_gates(xc, wa_ref, ba_ref, wx_ref, bx_ref, lam_ref):
    xcb = xc.astype(BF16)
    r_parts, i_parts = [], []
    for k in range(LRU_BLOCKS):
        blk = xcb[:, k * LRU_BW : (k + 1) * LRU_BW]
        r_parts.append(_dot(blk, wa_ref[k]))
        i_parts.append(_dot(blk, wx_ref[k]))
    r = _sigmoid(jnp.concatenate(r_parts, axis=1) + ba_ref[...])
    ig = _sigmoid(jnp.concatenate(i_parts, axis=1) + bx_ref[...])
    log_a = (-LRU_C * r) * _softplus(-lam_ref[...])
    a = jnp.exp(log_a)
    one_minus_a2 = -jnp.tanh(log_a) * (a * a + 1.0)
    u = jnp.sqrt(one_minus_a2) * (ig * xc)
    return a, u


def _lru_prompt_kernel(tt, proj_ref, cw_ref, cb_ref, wa_ref, ba_ref, wx_ref, bx_ref, lam_ref,
                       y_ref, hlast_ref, convn_ref, xpad, a_scr, u_scr, hs_scr, h_scr):
    t = pl.program_id(1)

    @pl.when(t == 0)
    def _():
        xpad[0:HALO, :] = jnp.zeros((HALO, D_RNN), F32)
        h_scr[...] = jnp.zeros_like(h_scr)

    @pl.when(t > 0)
    def _():
        xpad[0:HALO, :] = xpad[tt : tt + HALO, :]

    xpad[HALO : HALO + tt, :] = proj_ref[:, D_RNN:]
    xc = _conv_rows(xpad, cw_ref[...], cb_ref[...])
    a, u = _lru_gates(xc, wa_ref, ba_ref, wx_ref, bx_ref, lam_ref)
    a_scr[...] = a
    u_scr[...] = u

    def body(i, h):
        h = a_scr[pl.ds(i, 1), :] * h + u_scr[pl.ds(i, 1), :]
        hs_scr[pl.ds(i, 1), :] = h
        return h

    h_scr[...] = lax.fori_loop(0, tt, body, h_scr[...], unroll=8)
    y_ref[...] = hs_scr[...] * jax.nn.gelu(proj_ref[:, :D_RNN])

    @pl.when(t == pl.num_programs(1) - 1)
    def _():
        hlast_ref[...] = h_scr[...]
        convn_ref[...] = xpad[tt : tt + HALO, :]


def lru_prompt(proj, p, tt=256):
    n_t = SEQ // tt
    wspec = lambda shape: pl.BlockSpec(shape, lambda b, t: (0,) * len(shape))
    return pl.pallas_call(
        functools.partial(_lru_prompt_kernel, tt),
        grid=(BATCH, n_t),
        in_specs=[
            pl.BlockSpec((tt, 2 * D_RNN), lambda b, t: (b * n_t + t, 0)),
            wspec((CONV_W, D_RNN)), wspec((1, D_RNN)),
            wspec((LRU_BLOCKS, LRU_BW, LRU_BW)), wspec((1, D_RNN)),
            wspec((LRU_BLOCKS, LRU_BW, LRU_BW)), wspec((1, D_RNN)),
            wspec((1, D_RNN)),
        ],
        out_specs=[
            pl.BlockSpec((tt, D_RNN), lambda b, t: (b * n_t + t, 0)),
            pl.BlockSpec((None, 1, D_RNN), lambda b, t: (b, 0, 0)),
            pl.BlockSpec((None, HALO, D_RNN), lambda b, t: (b, 0, 0)),
        ],
        out_shape=[
            jax.ShapeDtypeStruct((N_PROMPT, D_RNN), F32),
            jax.ShapeDtypeStruct((BATCH, 1, D_RNN), F32),
            jax.ShapeDtypeStruct((BATCH, HALO, D_RNN), F32),
        ],
        scratch_shapes=[
            pltpu.VMEM((HALO + tt, D_RNN), F32),
            pltpu.VMEM((tt, D_RNN), F32),
            pltpu.VMEM((tt, D_RNN), F32),
            pltpu.VMEM((tt, D_RNN), F32),
            pltpu.VMEM((1, D_RNN), F32),
        ],
        compiler_params=_cparams("parallel", "arbitrary"),
        name="lru_prompt",
    )(proj, p["conv_w"], p["conv_b"], p["w_a"], p["b_a"], p["w_x"], p["b_x"], p["lam"])


def _lru_sample_kernel(proj_ref, convp_ref, hprev_ref, cw_ref, cb_ref, wa_ref, ba_ref, wx_ref, bx_ref, lam_ref,
                       y_ref, hlast_ref, convn_ref, xpad):
    hist = (CONV_W - 1) * DEC_BATCH
    xpad[0:hist, :] = convp_ref[...]
    xpad[hist:, :] = proj_ref[:, D_RNN:]
    xc = _conv_taps(xpad, hist, N_SAMPLE, DEC_BATCH, cw_ref[...], cb_ref[...])
    a, u = _lru_gates(xc, wa_ref, ba_ref, wx_ref, bx_ref, lam_ref)
    gate = jax.nn.gelu(proj_ref[:, :D_RNN])
    h = hprev_ref[...]
    for t in range(DEC_SEQ):
        rows = slice(t * DEC_BATCH, (t + 1) * DEC_BATCH)
        h = a[rows] * h + u[rows]
        y_ref[rows, :] = h * gate[rows]
    hlast_ref[...] = h
    convn_ref[...] = xpad[N_SAMPLE:, :]


def lru_sample(proj, conv_prev, h_prev, p):
    hist = (CONV_W - 1) * DEC_BATCH
    full = lambda shape: pl.BlockSpec(shape, lambda i: (0,) * len(shape))
    return pl.pallas_call(
        _lru_sample_kernel,
        grid=(1,),
        in_specs=[
            pl.BlockSpec((N_SAMPLE, 2 * D_RNN), lambda i: (SAMPLE_BLOCK, 0)),
            full((hist, D_RNN)), full((DEC_BATCH, D_RNN)),
            full((CONV_W, D_RNN)), full((1, D_RNN)),
            full((LRU_BLOCKS, LRU_BW, LRU_BW)), full((1, D_RNN)),
            full((LRU_BLOCKS, LRU_BW, LRU_BW)), full((1, D_RNN)),
            full((1, D_RNN)),
        ],
        out_specs=[
            full((N_SAMPLE, D_RNN)),
            full((DEC_BATCH, D_RNN)),
            full((hist, D_RNN)),
        ],
        out_shape=[
            jax.ShapeDtypeStruct((N_SAMPLE, D_RNN), F32),
            jax.ShapeDtypeStruct((DEC_BATCH, D_RNN), F32),
            jax.ShapeDtypeStruct((hist, D_RNN), F32),
        ],
        scratch_shapes=[pltpu.VMEM((hist + N_SAMPLE, D_RNN), F32)],
        compiler_params=_cparams("arbitrary"),
        name="lru_sample",
    )(proj, conv_prev, h_prev, p["conv_w"], p["conv_b"], p["w_a"], p["b_a"], p["w_x"], p["b_x"], p["lam"])


def _attend(q, k, v):
    outs = []
    for h in range(MEM_HEADS):
        hs = slice(h * MEM_HD, (h + 1) * MEM_HD)
        s = lax.dot_general(q[:, hs], k[:, hs], (((1,), (1,)), ((), ())), preferred_element_type=F32)
        s = s * (MEM_HD ** -0.5)
        e = jnp.exp(s - jnp.max(s, axis=-1, keepdims=True))
        p = e / jnp.sum(e, axis=-1, keepdims=True)
        outs.append(_dot(p.astype(BF16), v[:, hs]))
    return jnp.concatenate(outs, axis=1)


def _attend_sample(bb, q_ref, k_ref, v_ref, o_ref):
    rows = MEM_HEADS * DEC_SEQ
    cols = N_MEM * MEM_HEADS
    row_head = lax.broadcasted_iota(jnp.int32, (rows, cols), 0) // DEC_SEQ
    col_head = lax.broadcasted_iota(jnp.int32, (rows, cols), 1) % MEM_HEADS
    same_head = row_head == col_head
    for i in range(bb):
        k2 = k_ref[i].reshape(cols, MEM_HD).astype(BF16)
        v2 = v_ref[i].reshape(cols, MEM_HD).astype(BF16)
        q = q_ref[i]
        qh = jnp.concatenate([q[:, h * MEM_HD : (h + 1) * MEM_HD] for h in range(MEM_HEADS)], axis=0)
        s = lax.dot_general(qh, k2, (((1,), (1,)), ((), ())), preferred_element_type=F32) * (MEM_HD ** -0.5)
        s = jnp.where(same_head, s, -jnp.inf)
        e = jnp.exp(s - jnp.max(s, axis=-1, keepdims=True))
        p = e / jnp.sum(e, axis=-1, keepdims=True)
        oh = _dot(p.astype(BF16), v2)
        o_ref[i] = jnp.concatenate([oh[h * DEC_SEQ : (h + 1) * DEC_SEQ] for h in range(MEM_HEADS)], axis=1)


def _attention_kernel(bb, qp_ref, kp_ref, vp_ref, qs_ref, ks_ref, vs_ref, op_ref, os_ref):
    op_ref[...] = _attend(qp_ref[...], kp_ref[...].astype(BF16), vp_ref[...].astype(BF16)).astype(op_ref.dtype)
    _attend_sample(bb, qs_ref, ks_ref, vs_ref, os_ref)


def attention(q, q_sample, k, v, cache_k, cache_v, layer):
    tt = ROW_TILE
    n_t = SEQ // tt
    steps = BATCH * n_t
    bb = DEC_BATCH // steps
    cache = lambda: pl.BlockSpec((None, bb, N_MEM, MEM_HEADS, MEM_HD), lambda i: (layer, i, 0, 0, 0))
    mem = lambda: pl.BlockSpec((None, N_MEM, D_MODEL), lambda i: (layer, i // n_t, 0))
    return pl.pallas_call(
        functools.partial(_attention_kernel, bb),
        grid=(steps,),
        in_specs=[
            pl.BlockSpec((tt, D_MODEL), lambda i: (i, 0)), mem(), mem(),
            pl.BlockSpec((bb, DEC_SEQ, D_MODEL), lambda i: (i, 0, 0)), cache(), cache(),
        ],
        out_specs=[
            pl.BlockSpec((tt, D_MODEL), lambda i: (i, 0)),
            pl.BlockSpec((bb, DEC_SEQ, D_MODEL), lambda i: (i, 0, 0)),
        ],
        out_shape=[
            jax.ShapeDtypeStruct((N_PROMPT, D_MODEL), BF16),
            jax.ShapeDtypeStruct((DEC_BATCH, DEC_SEQ, D_MODEL), F32),
        ],
        compiler_params=_cparams("arbitrary"),
        name="attention",
    )(q, k, v, q_sample, cache_k, cache_v)


def _ssd_gate_norm(y, z, ng):
    y = y * _silu(z)
    outs = []
    for g in range(SSD_GROUPS):
        yg = y[:, g * SSD_GROUP_W : (g + 1) * SSD_GROUP_W]
        outs.append(yg * lax.rsqrt(jnp.mean(yg * yg, axis=-1, keepdims=True) + EPS))
    return jnp.concatenate(outs, axis=1) * ng


def _ssd_kernel(q, slot, n_sample_in, z_ref, xbc_ref, dt_ref, cw_ref, cb_ref, dtb_ref, alog_ref, dskip_ref, ng_ref,
                e_ref, *rest):
    sample_in = rest[:5]
    y_ref, st_ref, convn_ref, ys_ref, sts_ref, xpad, s_t, y_scr = rest[n_sample_in:]
    _ssd_recur_body(1, slot, *sample_in, ys_ref, sts_ref)
    t = pl.program_id(1)

    @pl.when(t == 0)
    def _():
        xpad[0:HALO, :] = jnp.zeros((HALO, SSD_CONV_DIM), F32)
        s_t[...] = jnp.zeros_like(s_t)

    @pl.when(t > 0)
    def _():
        xpad[0:HALO, :] = xpad[q : q + HALO, :]

    xpad[HALO : HALO + q, :] = xbc_ref[...]
    xbc = _silu(_conv_rows(xpad, cw_ref[...], cb_ref[...]))
    xs = xbc[:, :D_INNER]

    dt = _softplus(dt_ref[...] + dtb_ref[...])
    adt = -jnp.exp(alog_ref[...]) * dt
    row_i = lax.broadcasted_iota(jnp.int32, (q, q), 0)
    col_i = lax.broadcasted_iota(jnp.int32, (q, q), 1)
    tril = row_i >= col_i
    a_cs = _dot_f32_rhs(jnp.where(tril, 1.0, 0.0).astype(BF16), adt)
    a_cs_t = a_cs.T
    a_end = a_cs[q - 1 : q, :]
    expand = e_ref[...]
    ecs_x = _expand_heads(jnp.exp(a_cs), expand)
    xb = (xs * _expand_heads(dt, expand)).astype(BF16)
    xd = (xs * _expand_heads(dt * jnp.exp(a_end - a_cs), expand)).astype(BF16)

    for g in range(SSD_GROUPS):
        gc = slice(g * SSD_GROUP_W, (g + 1) * SSD_GROUP_W)
        bg = xbc[:, D_INNER + g * SSD_STATE : D_INNER + (g + 1) * SSD_STATE]
        cg = xbc[:, D_INNER + SSD_GN + g * SSD_STATE : D_INNER + SSD_GN + (g + 1) * SSD_STATE].astype(BF16)
        cb_mat = lax.dot_general(cg, bg.astype(BF16), (((1,), (1,)), ((), ())), preferred_element_type=F32)
        sg = s_t[:, gc]
        y_scr[:, gc] = _dot(cg, sg.astype(BF16)) * ecs_x[:, gc]
        s_t[:, gc] = ecs_x[q - 1 : q, gc] * sg + _dot(bg.T.astype(BF16), xd[:, gc])
        for e in range(SSD_HEADS // SSD_GROUPS):
            h = g * (SSD_HEADS // SSD_GROUPS) + e
            hc = slice(h * SSD_HEADDIM, (h + 1) * SSD_HEADDIM)
            seg = a_cs[:, h : h + 1] - a_cs_t[h : h + 1, :]
            decay = jnp.where(tril, jnp.exp(jnp.minimum(seg, 0.0)), 0.0)
            y_scr[:, hc] += _dot((cb_mat * decay).astype(BF16), xb[:, hc])

    y = y_scr[...] + dskip_ref[...] * xs
    y_ref[...] = _ssd_gate_norm(y, z_ref[...], ng_ref[...]).astype(y_ref.dtype)

    @pl.when(t == pl.num_programs(1) - 1)
    def _():
        for j in range(D_INNER // LANES):
            st_ref[j * LANES : (j + 1) * LANES, :] = s_t[:, j * LANES : (j + 1) * LANES].T
        convn_ref[...] = xpad[q : q + HALO, :]


def ssd(z, xbc, dt, p, xdt_s, dec_s, bm_s, cm_s, state, layer, new_states=None, q=SSD_CHUNK):
    n_t = SEQ // q
    assert BATCH * n_t == DEC_BATCH
    n_layers = state.shape[0]
    rows = lambda w: pl.BlockSpec((q, w), lambda b, t: (b * n_t + t, 0))
    wspec = lambda shape: pl.BlockSpec(shape, lambda b, t: (0,) * len(shape))
    seq = lambda w: pl.BlockSpec((1, DEC_SEQ, w), lambda b, t: (b * n_t + t, 0, 0))
    in_specs = [
        rows(D_INNER), rows(SSD_CONV_DIM), rows(LANES),
        wspec((CONV_W, SSD_CONV_DIM)), wspec((1, SSD_CONV_DIM)),
        wspec((1, LANES)), wspec((1, LANES)), wspec((1, D_INNER)), wspec((1, D_INNER)),
        wspec((LANES, D_INNER)),
        seq(D_INNER), seq(D_INNER), seq(SSD_GN), seq(SSD_GN),
        pl.BlockSpec((None, 1, D_INNER, SSD_STATE), lambda b, t: (layer, b * n_t + t, 0, 0)),
    ]
    args = [z, xbc, dt, p["conv_w"], p["conv_b"], p["dt_bias"], p["a_log"], p["d_skip"], p["norm_g"], p["expand"],
            xdt_s, dec_s, bm_s, cm_s, state]
    if new_states is None:
        slot = layer
        st_out = pl.BlockSpec((n_layers, 1, D_INNER, SSD_STATE), lambda b, t: (0, b * n_t + t, 0, 0))
        aliases = {}
    else:
        slot = 0
        st_out = pl.BlockSpec((1, 1, D_INNER, SSD_STATE), lambda b, t: (layer, b * n_t + t, 0, 0))
        in_specs.append(pl.BlockSpec(memory_space=pl.ANY))
        args.append(new_states)
        aliases = {len(args) - 1: 4}
    return pl.pallas_call(
        functools.partial(_ssd_kernel, q, slot, len(args) - 10),
        grid=(BATCH, n_t),
        in_specs=in_specs,
        out_specs=[
            rows(D_INNER),
            pl.BlockSpec((None, D_INNER, SSD_STATE), lambda b, t: (b, 0, 0)),
            pl.BlockSpec((None, HALO, SSD_CONV_DIM), lambda b, t: (b, 0, 0)),
            seq(D_INNER),
            st_out,
        ],
        out_shape=[
            jax.ShapeDtypeStruct((N_PROMPT, D_INNER), BF16),
            jax.ShapeDtypeStruct((BATCH, D_INNER, SSD_STATE), F32),
            jax.ShapeDtypeStruct((BATCH, HALO, SSD_CONV_DIM), F32),
            jax.ShapeDtypeStruct((DEC_BATCH, DEC_SEQ, D_INNER), F32),
            jax.ShapeDtypeStruct((n_layers, DEC_BATCH, D_INNER, SSD_STATE), F32),
        ],
        scratch_shapes=[
            pltpu.VMEM((HALO + q, SSD_CONV_DIM), F32),
            pltpu.VMEM((SSD_STATE, D_INNER), F32),
            pltpu.VMEM((q, D_INNER), F32),
        ],
        input_output_aliases=aliases,
        compiler_params=_cparams("arbitrary", "arbitrary"),
        name="ssd",
    )(*args)


def _ssd_sample_pre_kernel(xbc_ref, dt_ref, convp_ref, cw_ref, cb_ref, dtb_ref, alog_ref, e_ref,
                           xs_ref, xdt_ref, dec_ref, bm_ref, cm_ref, convn_ref, xpad):
    hist = (CONV_W - 1) * DEC_BATCH
    xpad[0:hist, :] = convp_ref[...]
    xpad[hist:, :] = xbc_ref[...]
    xc = _conv_taps(xpad, hist, N_SAMPLE, DEC_BATCH, cw_ref[...], cb_ref[...])
    xbc = xc * _sigmoid(xc)
    xs = xbc[:, :D_INNER]
    dt = _softplus(dt_ref[...] + dtb_ref[...])
    adt = -jnp.exp(alog_ref[...]) * dt
    expand = e_ref[...]
    xs_ref[...] = xs
    xdt_ref[...] = xs * _dot_f32_lhs(dt, expand)
    dec_ref[...] = _dot_f32_lhs(jnp.exp(adt), expand)
    bm_ref[...] = xbc[:, D_INNER : D_INNER + SSD_GN]
    cm_ref[...] = xbc[:, D_INNER + SSD_GN :]
    convn_ref[...] = xpad[N_SAMPLE:, :]


def ssd_sample_pre(xbc, dt, conv_prev, p):
    hist = (CONV_W - 1) * DEC_BATCH
    full = lambda shape: pl.BlockSpec(shape, lambda i: (0,) * len(shape))
    out_w = [D_INNER, D_INNER, D_INNER, SSD_GN, SSD_GN]
    return pl.pallas_call(
        _ssd_sample_pre_kernel,
        grid=(1,),
        in_specs=[
            pl.BlockSpec((N_SAMPLE, SSD_CONV_DIM), lambda i: (SAMPLE_BLOCK, 0)),
            pl.BlockSpec((N_SAMPLE, LANES), lambda i: (SAMPLE_BLOCK, 0)),
            full((hist, SSD_CONV_DIM)),
            full((CONV_W, SSD_CONV_DIM)), full((1, SSD_CONV_DIM)),
            full((1, LANES)), full((1, LANES)), full((LANES, D_INNER)),
        ],
        out_specs=[full((N_SAMPLE, w)) for w in out_w] + [full((hist, SSD_CONV_DIM))],
        out_shape=[jax.ShapeDtypeStruct((N_SAMPLE, w), F32) for w in out_w]
        + [jax.ShapeDtypeStruct((hist, SSD_CONV_DIM), F32)],
        scratch_shapes=[pltpu.VMEM((hist + N_SAMPLE, SSD_CONV_DIM), F32)],
        compiler_params=_cparams("arbitrary"),
        name="ssd_sample_pre",
    )(xbc, dt, conv_prev, p["conv_w"], p["conv_b"], p["dt_bias"], p["a_log"], p["expand"])


def _ssd_recur_body(bb, slot, xdt_ref, dec_ref, bm_ref, cm_ref, st_in_ref, y_ref, st_out_ref):
    nt = (((1,), (1,)), ((), ()))
    tn = (((0,), (0,)), ((), ()))
    last = DEC_SEQ - 1
    ones = jnp.ones((SUBLANES, SSD_STATE), BF16)
    for i in range(bb):
        xdt = xdt_ref[i]
        dec = dec_ref[i]
        decay = [dec[0:1]]
        for t in range(1, DEC_SEQ):
            decay.append(decay[t - 1] * dec[t : t + 1])
        prop = {(s, s): xdt[s : s + 1] for s in range(DEC_SEQ)}
        for t in range(1, DEC_SEQ):
            for s in range(t):
                prop[(t, s)] = dec[t : t + 1] * prop[(t - 1, s)]
        d3 = jnp.concatenate(list(_split3(decay[last])) + [jnp.zeros((SUBLANES - 3, D_INNER), BF16)], axis=0)
        p_last = jnp.concatenate([prop[(last, s)] for s in range(DEC_SEQ)], axis=0).astype(BF16)
        for g in range(SSD_GROUPS):
            gc = slice(g * SSD_GROUP_W, (g + 1) * SSD_GROUP_W)
            bg = bm_ref[i, :, g * SSD_STATE : (g + 1) * SSD_STATE].astype(BF16)
            cg = cm_ref[i, :, g * SSD_STATE : (g + 1) * SSD_STATE].astype(BF16)
            sg = st_in_ref[i, gc, :]
            c_h = lax.dot_general(cg, sg.astype(BF16), nt, preferred_element_type=F32)
            c_b = lax.dot_general(cg, bg, nt, preferred_element_type=F32)
            for t in range(DEC_SEQ):
                y = decay[t][:, gc] * c_h[t : t + 1]
                for s in range(t + 1):
                    y = y + jnp.broadcast_to(c_b[t : t + 1, s : s + 1], (1, SSD_GROUP_W)) * prop[(t, s)][:, gc]
                y_ref[i, t : t + 1, gc] = y
            d_col = lax.dot_general(d3[:, gc], ones, tn, preferred_element_type=F32)
            st_out_ref[slot, i, gc, :] = d_col * sg + lax.dot_general(p_last[:, gc], bg, tn, preferred_element_type=F32)
    for other in range(st_out_ref.shape[0]):
        if other != slot:
            st_out_ref[other] = jnp.zeros(st_out_ref.shape[1:], F32)


def _ssd_sample_post_kernel(yr_ref, xs_ref, z_ref, dskip_ref, ng_ref, y_ref):
    y = yr_ref[...] + dskip_ref[...] * xs_ref[...]
    y_ref[...] = _ssd_gate_norm(y, z_ref[...], ng_ref[...]).astype(y_ref.dtype)


def ssd_sample_post(y_raw, xs, z, p):
    full = lambda shape: pl.BlockSpec(shape, lambda i: (0,) * len(shape))
    return pl.pallas_call(
        _ssd_sample_post_kernel,
        grid=(1,),
        in_specs=[
            full((N_SAMPLE, D_INNER)), full((N_SAMPLE, D_INNER)),
            pl.BlockSpec((N_SAMPLE, D_INNER), lambda i: (SAMPLE_BLOCK, 0)),
            full((1, D_INNER)), full((1, D_INNER)),
        ],
        out_specs=full((N_SAMPLE, D_INNER)),
        out_shape=jax.ShapeDtypeStruct((N_SAMPLE, D_INNER), BF16),
        compiler_params=_cparams("arbitrary"),
        name="ssd_sample_post",
    )(y_raw, xs, z, p["d_skip"], p["norm_g"])


def _to_time_major(a):
    return jnp.swapaxes(a, 0, 1).reshape(a.shape[0] * a.shape[1], a.shape[2])


def _to_batch_major(a, t):
    return jnp.swapaxes(a.reshape(t, DEC_BATCH, a.shape[1]), 0, 1)


def _row(v):
    return v.reshape(1, -1).astype(F32)


def _pad_lanes(v):
    return jnp.pad(v.reshape(1, -1).astype(F32), ((0, 0), (0, LANES - v.shape[-1])))


def kernel(x_prompt, x_sample, state_lru_h, state_lru_conv, state_ssd, state_ssd_conv, cache_mem_k, cache_mem_v, mem_prompt, norm_mix, norm_mem, norm_memkv, norm_ffn, norm_final, lru_w_in, lru_conv_w, lru_conv_b, lru_w_a, lru_b_a, lru_w_x, lru_b_x, lru_lam, lru_w_out, ssd_w_in, ssd_conv_w, ssd_conv_b, ssd_dt_bias, ssd_a_log, ssd_d, ssd_norm_g, ssd_w_out, mem_w_q, mem_w_k, mem_w_v, mem_w_o, ffn_w1, ffn_w3, ffn_w2, moe_router, moe_w1, moe_w3, moe_w2):
    bf = lambda w: w.astype(BF16)
    x = concat_rows(x_prompt.reshape(N_PROMPT, D_MODEL), _to_time_major(x_sample))
    mem = mem_prompt.reshape(BATCH * N_MEM, D_MODEL)
    head_of_col = jnp.arange(D_INNER, dtype=jnp.int32) // SSD_HEADDIM
    expand = (jnp.arange(LANES, dtype=jnp.int32)[:, None] == head_of_col[None, :]).astype(BF16)

    p_lru_h, p_lru_conv, p_ssd, p_ssd_conv = [], [], [], []
    s_lru_h, s_lru_conv, s_ssd_conv = [], [], []
    s_ssd = None
    xs_buf = None
    hist = CONV_W - 1
    mk, mv, p_mk, p_mv = mem_kv(mem, norm_memkv, mem_w_k, mem_w_v)
    for i in range(DEPTH):
        j = i // 2
        if i % 2 == 0:
            p = dict(conv_w=lru_conv_w[j], conv_b=_row(lru_conv_b[j]), w_a=bf(lru_w_a[j]), b_a=_row(lru_b_a[j]),
                     w_x=bf(lru_w_x[j]), b_x=_row(lru_b_x[j]), lam=_row(lru_lam[j]))
            (proj,) = norm_matmul(x, norm_mix[i], [bf(lru_w_in[j])], [F32])
            y_p, h_p, c_p = lru_prompt(proj, p)
            y_s, h_s, c_s = lru_sample(proj, _to_time_major(state_lru_conv[j]), state_lru_h[j], p)
            p_lru_h.append(h_p.reshape(BATCH, D_RNN))
            p_lru_conv.append(c_p[:, HALO - hist :, :])
            s_lru_h.append(h_s)
            s_lru_conv.append(_to_batch_major(c_s, hist))
            w_out = bf(lru_w_out[j])
        else:
            w_in = ssd_w_in[j]
            w_z = bf(w_in[:, :D_INNER])
            w_xbc = bf(w_in[:, D_INNER : D_INNER + SSD_CONV_DIM])
            w_dt = bf(jnp.pad(w_in[:, D_INNER + SSD_CONV_DIM :], ((0, 0), (0, LANES - SSD_HEADS))))
            p = dict(conv_w=ssd_conv_w[j], conv_b=_row(ssd_conv_b[j]), dt_bias=_pad_lanes(ssd_dt_bias[j]),
                     a_log=_pad_lanes(ssd_a_log[j]), d_skip=_row(jnp.repeat(ssd_d[j], SSD_HEADDIM)),
                     norm_g=_row(ssd_norm_g[j]), expand=expand)
            z, xbc, dt = norm_matmul(x, norm_mix[i], [w_z, w_xbc, w_dt], [F32, F32, F32], tm=256)
            xs_s, xdt_s, dec_s, bm_s, cm_s, c_s = ssd_sample_pre(xbc, dt, _to_time_major(state_ssd_conv[j]), p)
            y_p, st_p, c_p, y_raw, s_ssd = ssd(
                z, xbc, dt, p,
                _to_batch_major(xdt_s, DEC_SEQ), _to_batch_major(dec_s, DEC_SEQ),
                _to_batch_major(bm_s, DEC_SEQ), _to_batch_major(cm_s, DEC_SEQ),
                state_ssd.reshape(-1, DEC_BATCH, D_INNER, SSD_STATE), j, s_ssd)
            y_s = ssd_sample_post(_to_time_major(y_raw), xs_s, z, p)
            p_ssd.append(st_p.reshape(BATCH, SSD_HEADS, SSD_HEADDIM, SSD_STATE))
            p_ssd_conv.append(c_p[:, HALO - hist :, :])
            s_ssd_conv.append(_to_batch_major(c_s, hist))
            w_out = bf(ssd_w_out[j])

        x, qp = out_proj_q(y_p, y_s, w_out, x, norm_mem[i], bf(mem_w_q[i]))
        o_p, o_s = attention(qp, _to_batch_major(qp[N_PROMPT:], DEC_SEQ), mk, mv, cache_mem_k, cache_mem_v, i)
        o_s = _to_time_major(o_s)
        if i % 2 == 0:
            x = attn_out_ffn(x, o_p, o_s, bf(mem_w_o[i]), norm_ffn[i],
                             bf(ffn_w1[j]), bf(ffn_w3[j]), bf(ffn_w2[j]), tf=D_FF // 2)
        else:
            x, xs_buf = attn_out_moe(x, o_p, o_s, bf(mem_w_o[i]), norm_ffn[i], moe_router[j],
                                     moe_w1, moe_w3, moe_w2, j, xs_buf)

    y_prompt = rmsnorm_rows(x, norm_final, 0, N_PROMPT // ROW_TILE).reshape(BATCH, SEQ, D_MODEL)
    y_sample = _to_batch_major(rmsnorm_rows(x, norm_final, N_PROMPT // ROW_TILE, N_SAMPLE // ROW_TILE), DEC_SEQ)
    return (y_prompt, y_sample,
            jnp.stack(p_lru_h), jnp.stack(p_lru_conv), jnp.stack(p_ssd), jnp.stack(p_ssd_conv),
            p_mk, p_mv,
            jnp.stack(s_lru_h), jnp.stack(s_lru_conv), s_ssd.reshape(state_ssd.shape), jnp.stack(s_ssd_conv))
```

```python
import functools
import math

import jax
import jax.numpy as jnp
from jax import lax
from jax.experimental import pallas as pl
from jax.experimental.pallas import tpu as pltpu

F32 = jnp.float32
BF16 = jnp.bfloat16

D_MODEL = 1024
BATCH = 8
SEQ = 2048
DEPTH = 4
DEC_BATCH = 128
DEC_SEQ = 4
CONV_W = 4
EPS = 1e-6
D_RNN = D_MODEL
LRU_BLOCKS = 8
LRU_BW = D_RNN // LRU_BLOCKS
LRU_C = 8.0
D_INNER = 2 * D_MODEL
SSD_HEADDIM = 64
SSD_HEADS = D_INNER // SSD_HEADDIM
SSD_GROUPS = 4
SSD_GROUP_W = D_INNER // SSD_GROUPS
SSD_STATE = 128
SSD_GN = SSD_GROUPS * SSD_STATE
SSD_CONV_DIM = D_INNER + 2 * SSD_GN
SSD_CHUNK = 128
N_MEM = 256
MEM_HEADS = 4
MEM_HD = D_MODEL // MEM_HEADS
D_FF = 2816
N_EXPERTS = 8
TOP_K = 2
D_FF_EXPERT = 3584

LANES = 128
SUBLANES = 8
VMEM_LIMIT_BYTES = 56 * 1024 * 1024

N_PROMPT = BATCH * SEQ
N_SAMPLE = DEC_BATCH * DEC_SEQ
N_ROWS = N_PROMPT + N_SAMPLE
ROW_TILE = 512
SAMPLE_BLOCK = N_PROMPT // N_SAMPLE
HALO = SUBLANES
MOE_TILE = 1024
MOE_FF_TILE = 512


def _cparams(*sem):
    return pltpu.CompilerParams(dimension_semantics=sem, vmem_limit_bytes=VMEM_LIMIT_BYTES)


def _rms(x, g):
    return x * lax.rsqrt(jnp.mean(x * x, axis=-1, keepdims=True) + EPS) * g


def _sigmoid(x):
    return 1.0 / (1.0 + jnp.exp(-x))


def _silu(x):
    h = 0.5 * x
    return h + h * jnp.tanh(h)


def _softplus(x):
    return jnp.maximum(x, 0.0) + jnp.log1p(jnp.exp(-jnp.abs(x)))


def _split3(x):
    a = x.astype(BF16)
    r = x - a.astype(F32)
    b = r.astype(BF16)
    c = (r - b.astype(F32)).astype(BF16)
    return a, b, c


def _dot(a, b):
    return jnp.dot(a, b, preferred_element_type=F32)


def _dot_f32_lhs(x, m):
    a, b, c = _split3(x)
    return _dot(a, m) + _dot(b, m) + _dot(c, m)


def _expand_heads(x, expand):
    hi = x.astype(BF16)
    lo = (x - hi.astype(F32)).astype(BF16)
    return _dot(hi, expand) + _dot(lo, expand)


def _dot_f32_rhs(m, x):
    a, b, c = _split3(x)
    return _dot(m, a) + _dot(m, b) + _dot(m, c)


def _norm_matmul_kernel(n_w, x_ref, g_ref, *refs):
    h = _rms(x_ref[...], g_ref[...]).astype(BF16)
    for w_ref, o_ref in zip(refs[:n_w], refs[n_w:]):
        o_ref[...] = _dot(h, w_ref[...]).astype(o_ref.dtype)


def norm_matmul(x, g, ws, out_dtypes, tm=ROW_TILE):
    rows, k = x.shape
    in_specs = [pl.BlockSpec((tm, k), lambda i: (i, 0)), pl.BlockSpec((1, k), lambda i: (0, 0))]
    in_specs += [pl.BlockSpec(w.shape, lambda i: (0, 0)) for w in ws]
    out_specs = [pl.BlockSpec((tm, w.shape[1]), lambda i: (i, 0)) for w in ws]
    out_shape = [jax.ShapeDtypeStruct((rows, w.shape[1]), dt) for w, dt in zip(ws, out_dtypes)]
    return pl.pallas_call(
        functools.partial(_norm_matmul_kernel, len(ws)),
        grid=(rows // tm,),
        in_specs=in_specs,
        out_specs=out_specs,
        out_shape=out_shape,
        compiler_params=_cparams("parallel"),
        name="norm_matmul",
    )(x, g.reshape(1, k), *ws)


def _concat_rows_kernel(n_p, a_ref, b_ref, o_ref):
    i = pl.program_id(0)

    @pl.when(i < n_p)
    def _():
        o_ref[...] = a_ref[...]

    @pl.when(i >= n_p)
    def _():
        o_ref[...] = b_ref[...]


def concat_rows(a, b, tm=ROW_TILE):
    k = a.shape[1]
    n_p = a.shape[0] // tm
    n_s = b.shape[0] // tm
    return pl.pallas_call(
        functools.partial(_concat_rows_kernel, n_p),
        grid=(n_p + n_s,),
        in_specs=[
            pl.BlockSpec((tm, k), lambda i: (jnp.minimum(i, n_p - 1), 0)),
            pl.BlockSpec((tm, k), lambda i: (jnp.maximum(i - n_p, 0), 0)),
        ],
        out_specs=pl.BlockSpec((tm, k), lambda i: (i, 0)),
        out_shape=jax.ShapeDtypeStruct((a.shape[0] + b.shape[0], k), a.dtype),
        compiler_params=_cparams("arbitrary"),
        name="concat_rows",
    )(a, b)


def _mem_kv_kernel(bt, m_ref, g_ref, wk_ref, wv_ref, k2_ref, v2_ref, k4_ref, v4_ref):
    h = _rms(m_ref[...], g_ref[...]).astype(BF16)
    for w_ref, o2_ref, o4_ref in ((wk_ref, k2_ref, k4_ref), (wv_ref, v2_ref, v4_ref)):
        r = _dot(h, w_ref[...].astype(BF16))
        o2_ref[...] = r
        o4_ref[...] = r.reshape(bt, N_MEM, MEM_HEADS, MEM_HD)


def mem_kv(mem, g, w_k, w_v, bt=2):
    rows, k = mem.shape
    tm = bt * N_MEM
    flat = lambda: pl.BlockSpec((None, tm, k), lambda l, i: (l, i, 0))
    heads = lambda: pl.BlockSpec((None, bt, N_MEM, MEM_HEADS, MEM_HD), lambda l, i: (l, i, 0, 0, 0))
    wspec = lambda: pl.BlockSpec((None, k, k), lambda l, i: (l, 0, 0))
    flat_shape = jax.ShapeDtypeStruct((DEPTH, rows, k), F32)
    heads_shape = jax.ShapeDtypeStruct((DEPTH, BATCH, N_MEM, MEM_HEADS, MEM_HD), F32)
    return pl.pallas_call(
        functools.partial(_mem_kv_kernel, bt),
        grid=(DEPTH, rows // tm),
        in_specs=[
            pl.BlockSpec((tm, k), lambda l, i: (i, 0)),
            pl.BlockSpec((None, 1, k), lambda l, i: (l, 0, 0)),
            wspec(), wspec(),
        ],
        out_specs=[flat(), flat(), heads(), heads()],
        out_shape=[flat_shape, flat_shape, heads_shape, heads_shape],
        compiler_params=_cparams("arbitrary", "arbitrary"),
        name="mem_kv",
    )(mem, g.reshape(DEPTH, 1, k), w_k, w_v)


def _two_part_specs(a_prompt, a_sample, tm):
    n_p = a_prompt.shape[0] // tm
    specs = [
        pl.BlockSpec((tm, a_prompt.shape[1]), lambda i, *_: (jnp.minimum(i, n_p - 1), 0)),
        pl.BlockSpec((tm, a_sample.shape[1]), lambda i, *_: (jnp.maximum(i - n_p, 0), 0)),
    ]
    return specs, n_p


def _on_row_part(n_p, prompt_ref, sample_ref, fn):
    i = pl.program_id(0)

    @pl.when(i < n_p)
    def _():
        fn(prompt_ref[...])

    @pl.when(i >= n_p)
    def _():
        fn(sample_ref[...])


def _out_proj_q_kernel(n_p, yp_ref, ys_ref, w_ref, r_ref, g_ref, wq_ref, o_ref, q_ref):
    def finish(y):
        x_new = r_ref[...] + _dot(y.astype(BF16), w_ref[...])
        o_ref[...] = x_new
        q_ref[...] = _dot(_rms(x_new, g_ref[...]).astype(BF16), wq_ref[...]).astype(q_ref.dtype)

    _on_row_part(n_p, yp_ref, ys_ref, finish)


def out_proj_q(y_prompt, y_sample, w, res, g, w_q, tm=ROW_TILE):
    k, n = w.shape
    y_specs, n_p = _two_part_specs(y_prompt, y_sample, tm)
    full = lambda shape: pl.BlockSpec(shape, lambda i: (0, 0))
    rows = lambda: pl.BlockSpec((tm, n), lambda i: (i, 0))
    return pl.pallas_call(
        functools.partial(_out_proj_q_kernel, n_p),
        grid=(res.shape[0] // tm,),
        in_specs=y_specs + [full((k, n)), rows(), full((1, n)), full((n, n))],
        out_specs=[rows(), rows()],
        out_shape=[jax.ShapeDtypeStruct(res.shape, F32), jax.ShapeDtypeStruct(res.shape, BF16)],
        compiler_params=_cparams("arbitrary"),
        name="out_proj_q",
    )(y_prompt, y_sample, w, res, g.reshape(1, n), w_q)


def _rmsnorm_kernel(x_ref, g_ref, o_ref):
    o_ref[...] = _rms(x_ref[...], g_ref[...])


def rmsnorm_rows(x, g, first_block, n_blocks, tm=ROW_TILE):
    k = x.shape[1]
    return pl.pallas_call(
        _rmsnorm_kernel,
        grid=(n_blocks,),
        in_specs=[pl.BlockSpec((tm, k), lambda i: (i + first_block, 0)), pl.BlockSpec((1, k), lambda i: (0, 0))],
        out_specs=pl.BlockSpec((tm, k), lambda i: (i, 0)),
        out_shape=jax.ShapeDtypeStruct((n_blocks * tm, k), F32),
        compiler_params=_cparams("parallel"),
        name="final_norm",
    )(x, g.reshape(1, k))


def _swiglu_partial(h, w1_ref, w3_ref, w2_ref):
    a = _dot(h, w1_ref[...])
    b = _dot(h, w3_ref[...])
    return _dot((a * _sigmoid(a) * b).astype(BF16), w2_ref[...])


def _ffn_kernel(n_p, x_ref, op_ref, os_ref, wo_ref, g_ref, w1_ref, w3_ref, w2_ref, o_ref, xn_scr, h_scr, acc_scr):
    j = pl.program_id(1)

    @pl.when(j == 0)
    def _():
        def start(o):
            xn = x_ref[...] + _dot(o.astype(BF16), wo_ref[...])
            xn_scr[...] = xn
            h_scr[...] = _rms(xn, g_ref[...]).astype(BF16)

        _on_row_part(n_p, op_ref, os_ref, start)
        acc_scr[...] = jnp.zeros_like(acc_scr)

    acc_scr[...] += _swiglu_partial(h_scr[...], w1_ref, w3_ref, w2_ref)

    @pl.when(j == pl.num_programs(1) - 1)
    def _():
        o_ref[...] = xn_scr[...] + acc_scr[...]


def attn_out_ffn(x, o_prompt, o_sample, w_o, g, w1, w3, w2, tf, tm=ROW_TILE):
    rows, k = x.shape
    f = w1.shape[1]
    o_specs, n_p = _two_part_specs(o_prompt, o_sample, tm)
    return pl.pallas_call(
        functools.partial(_ffn_kernel, n_p),
        grid=(rows // tm, f // tf),
        in_specs=[pl.BlockSpec((tm, k), lambda i, j: (i, 0))] + o_specs + [
            pl.BlockSpec((k, k), lambda i, j: (0, 0)),
            pl.BlockSpec((1, k), lambda i, j: (0, 0)),
            pl.BlockSpec((k, tf), lambda i, j: (0, j)),
            pl.BlockSpec((k, tf), lambda i, j: (0, j)),
            pl.BlockSpec((tf, k), lambda i, j: (j, 0)),
        ],
        out_specs=pl.BlockSpec((tm, k), lambda i, j: (i, 0)),
        out_shape=jax.ShapeDtypeStruct((rows, k), F32),
        scratch_shapes=[pltpu.VMEM((tm, k), F32), pltpu.VMEM((tm, k), BF16), pltpu.VMEM((tm, k), F32)],
        compiler_params=_cparams("arbitrary", "arbitrary"),
        name="attn_out_ffn",
    )(x, o_prompt, o_sample, w_o, g.reshape(1, k), w1, w3, w2)


def _router_kernel(tm, n_p, x_ref, op_ref, os_ref, wo_ref, g_ref, wr_ref,
                   xn_ref, gate_ref, rank_ref, exp_ref, cnt_ref, carry):
    @pl.when(pl.program_id(0) == 0)
    def _():
        carry[...] = jnp.zeros_like(carry)

    def add_attention(o):
        xn_ref[...] = x_ref[...] + _dot(o.astype(BF16), wo_ref[...])

    _on_row_part(n_p, op_ref, os_ref, add_attention)
    h = _rms(xn_ref[...], g_ref[...])
    h1 = h.astype(BF16)
    h2 = (h - h1.astype(F32)).astype(BF16)
    w = wr_ref[...]
    w1 = w.astype(BF16)
    w2 = (w - w1.astype(F32)).astype(BF16)
    logits = _dot(h1, w1) + _dot(h1, w2) + _dot(h2, w1)
    lane = lax.broadcasted_iota(jnp.int32, logits.shape, 1).astype(F32)
    neg = jnp.float32(-jnp.inf)
    logits = jnp.where(lane < N_EXPERTS, logits, neg)
    m1 = jnp.max(logits, axis=-1, keepdims=True)
    i1 = jnp.min(jnp.where(logits == m1, lane, float(LANES)), axis=-1, keepdims=True)
    rest = jnp.where(lane == i1, neg, logits)
    m2 = jnp.max(rest, axis=-1, keepdims=True)
    i2 = jnp.min(jnp.where(rest == m2, lane, float(LANES)), axis=-1, keepdims=True)
    e2 = jnp.exp(m2 - m1)
    den = 1.0 + e2
    gate_ref[...] = jnp.where(lane == 0.0, 1.0 / den, 0.0) + jnp.where(lane == 1.0, e2 / den, 0.0)

    oh1_t = jnp.where(lane == i1, 1.0, 0.0).T
    oh2_t = jnp.where(lane == i2, 1.0, 0.0).T
    oh_t = oh1_t + oh2_t
    src = lax.broadcasted_iota(jnp.int32, (tm, tm), 0)
    dst = lax.broadcasted_iota(jnp.int32, (tm, tm), 1)
    earlier = jnp.where(src < dst, 1.0, 0.0).astype(BF16)
    before = _dot(oh_t.astype(BF16), earlier)
    base = jnp.concatenate([carry[...]] * (tm // LANES), axis=1) + before
    expert_id = lax.broadcasted_iota(jnp.int32, (LANES, tm), 0).astype(F32)
    col_sum = lambda a: jnp.sum(a, axis=0, keepdims=True)
    rank_ref[...] = jnp.concatenate([col_sum(oh1_t * base), col_sum(oh2_t * base)], axis=1).astype(jnp.int32)
    exp_ref[...] = jnp.concatenate([col_sum(oh1_t * expert_id), col_sum(oh2_t * expert_id)], axis=1).astype(jnp.int32)
    carry[...] += jnp.broadcast_to(jnp.sum(oh_t, axis=1, keepdims=True), carry.shape)
    cnt_ref[...] = carry[...]


def attn_out_router(x, o_prompt, o_sample, w_o, g, w_router, tm=ROW_TILE):
    rows, k = x.shape
    n_t = rows // tm
    wr = jnp.pad(w_router, ((0, 0), (0, LANES - N_EXPERTS)))
    o_specs, n_p = _two_part_specs(o_prompt, o_sample, tm)
    return pl.pallas_call(
        functools.partial(_router_kernel, tm, n_p),
        grid=(n_t,),
        in_specs=[pl.BlockSpec((tm, k), lambda i: (i, 0))] + o_specs + [
            pl.BlockSpec((k, k), lambda i: (0, 0)),
            pl.BlockSpec((1, k), lambda i: (0, 0)),
            pl.BlockSpec((k, LANES), lambda i: (0, 0)),
        ],
        out_specs=[
            pl.BlockSpec((tm, k), lambda i: (i, 0)),
            pl.BlockSpec((tm, LANES), lambda i: (i, 0)),
            pl.BlockSpec((None, 1, TOP_K * tm), lambda i: (i, 0, 0)),
            pl.BlockSpec((None, 1, TOP_K * tm), lambda i: (i, 0, 0)),
            pl.BlockSpec((LANES, LANES), lambda i: (0, 0)),
        ],
        out_shape=[
            jax.ShapeDtypeStruct((rows, k), F32),
            jax.ShapeDtypeStruct((rows, LANES), F32),
            jax.ShapeDtypeStruct((n_t, 1, TOP_K * tm), jnp.int32),
            jax.ShapeDtypeStruct((n_t, 1, TOP_K * tm), jnp.int32),
            jax.ShapeDtypeStruct((LANES, LANES), F32),
        ],
        scratch_shapes=[pltpu.VMEM((LANES, LANES), F32)],
        compiler_params=_cparams("arbitrary"),
        name="attn_out_router",
    )(x, o_prompt, o_sample, w_o, g.reshape(1, k), wr)


def _as_tiles(a):
    return a.reshape(a.shape[0], SUBLANES, LANES)


def _as_rows(a):
    return a.reshape(a.shape[0], SUBLANES * LANES)


def _row_copy(src_ref, src_row, dst_ref, dst_row, sem):
    return pltpu.make_async_copy(src_ref.at[src_row], dst_ref.at[dst_row], sem)


def _dispatch_kernel(tm, pos_ref, x_ref, xs_in_ref, xs_ref, x_tiles, sem):
    del xs_in_ref
    x_tiles[...] = _as_tiles(x_ref[...])

    def start(r, c):
        for choice in range(TOP_K):
            _row_copy(x_tiles, r, xs_ref, pos_ref[0, choice * tm + r], sem).start(priority=choice)
        return c

    lax.fori_loop(0, tm, start, 0, unroll=8)
    for _ in range(TOP_K):
        pltpu.make_async_copy(x_tiles, xs_ref.at[pl.ds(0, tm)], sem).wait()


def dispatch(x, pos, xs_buf, tm=ROW_TILE):
    rows, k = x.shape
    return pl.pallas_call(
        functools.partial(_dispatch_kernel, tm),
        grid=(rows // tm,),
        in_specs=[
            pl.BlockSpec((None, 1, TOP_K * tm), lambda i: (i, 0, 0), memory_space=pltpu.SMEM),
            pl.BlockSpec((tm, k), lambda i: (i, 0)),
            pl.BlockSpec(memory_space=pl.ANY),
        ],
        out_specs=pl.BlockSpec(memory_space=pl.ANY),
        out_shape=jax.ShapeDtypeStruct(xs_buf.shape, F32),
        scratch_shapes=[pltpu.VMEM((tm, SUBLANES, LANES), F32), pltpu.SemaphoreType.DMA(())],
        input_output_aliases={2: 0},
        compiler_params=_cparams("arbitrary"),
        name="moe_dispatch",
    )(pos, x, xs_buf)


def _moe_kernel(texp_ref, tvalid_ref, x_ref, g_ref, w1_ref, w3_ref, w2_ref, o_ref, h_scr, acc_scr):
    del texp_ref
    j = pl.program_id(1)
    last = pl.num_programs(1) - 1
    valid = tvalid_ref[pl.program_id(0)]

    @pl.when(valid > 0)
    def _():
        @pl.when(j == 0)
        def _():
            h_scr[...] = _rms(_as_rows(x_ref[...]), g_ref[...]).astype(BF16)
            acc_scr[...] = jnp.zeros_like(acc_scr)

        w1 = w1_ref[...].astype(BF16)
        w3 = w3_ref[...].astype(BF16)
        w2 = w2_ref[...].astype(BF16)
        half = x_ref.shape[0] // 2
        for lo in (0, half):

            @pl.when(valid > lo)
            def _():
                h = h_scr[lo : lo + half, :]
                a = _dot(h, w1)
                b = _dot(h, w3)
                acc_scr[lo : lo + half, :] += _dot((a * _sigmoid(a) * b).astype(BF16), w2)

        @pl.when(j == last)
        def _():
            o_ref[...] = acc_scr[...]

    @pl.when(jnp.logical_and(valid == 0, j == last))
    def _():
        o_ref[...] = jnp.zeros_like(o_ref)


def moe_experts(xs, g, w1, w3, w2, layer, tile_expert, tile_valid, tm, tf):
    rows = xs.shape[0]
    k = SUBLANES * LANES
    f = w1.shape[-1]
    n_f = f // tf

    def jf(i, j, tvalid):
        return jnp.where(tvalid[i] > 0, j, n_f - 1)

    grid_spec = pltpu.PrefetchScalarGridSpec(
        num_scalar_prefetch=2,
        grid=(rows // tm, n_f),
        in_specs=[
            pl.BlockSpec((tm, SUBLANES, LANES), lambda i, j, texp, tvalid: (i, 0, 0)),
            pl.BlockSpec((1, k), lambda i, j, texp, tvalid: (0, 0)),
            pl.BlockSpec((None, None, k, tf), lambda i, j, texp, tvalid: (layer, texp[i], 0, jf(i, j, tvalid))),
            pl.BlockSpec((None, None, k, tf), lambda i, j, texp, tvalid: (layer, texp[i], 0, jf(i, j, tvalid))),
            pl.BlockSpec((None, None, tf, k), lambda i, j, texp, tvalid: (layer, texp[i], jf(i, j, tvalid), 0)),
        ],
        out_specs=pl.BlockSpec((tm, k), lambda i, j, texp, tvalid: (i, 0)),
        scratch_shapes=[pltpu.VMEM((tm, k), BF16), pltpu.VMEM((tm, k), F32)],
    )
    return pl.pallas_call(
        _moe_kernel,
        grid_spec=grid_spec,
        out_shape=jax.ShapeDtypeStruct((rows, k), F32),
        compiler_params=_cparams("arbitrary", "arbitrary"),
        name="moe_experts",
    )(tile_expert, tile_valid, xs, g.reshape(1, k), w1, w3, w2)


def _combine_kernel(tm, pos_ref, x_ref, gate_ref, ys_ref, o_ref, y1_scr, y2_scr, sem):
    bufs = (y1_scr, y2_scr)

    def start(r, c):
        for choice in range(TOP_K):
            src = ys_ref.at[pl.ds(pos_ref[0, choice * tm + r], 1)]
            pltpu.make_async_copy(src, bufs[choice].at[pl.ds(r, 1)], sem).start(priority=choice)
        return c

    lax.fori_loop(0, tm, start, 0, unroll=8)
    for choice in range(TOP_K):
        pltpu.make_async_copy(ys_ref.at[pl.ds(0, tm)], bufs[choice], sem).wait()
    g = gate_ref[...]
    o_ref[...] = x_ref[...] + (g[:, 0:1] * y1_scr[...] + g[:, 1:2] * y2_scr[...])


def combine(x, gate, pos, ys, tm=ROW_TILE):
    rows, k = x.shape
    return pl.pallas_call(
        functools.partial(_combine_kernel, tm),
        grid=(rows // tm,),
        in_specs=[
            pl.BlockSpec((None, 1, TOP_K * tm), lambda i: (i, 0, 0), memory_space=pltpu.SMEM),
            pl.BlockSpec((tm, k), lambda i: (i, 0)),
            pl.BlockSpec((tm, LANES), lambda i: (i, 0)),
            pl.BlockSpec(memory_space=pl.ANY),
        ],
        out_specs=pl.BlockSpec((tm, k), lambda i: (i, 0)),
        out_shape=jax.ShapeDtypeStruct((rows, k), F32),
        scratch_shapes=[pltpu.VMEM((tm, k), F32), pltpu.VMEM((tm, k), F32), pltpu.SemaphoreType.DMA(())],
        compiler_params=_cparams("arbitrary"),
        name="moe_combine",
    )(pos, x, gate, ys)


def attn_out_moe(x, o_prompt, o_sample, w_o, g, w_router, w1, w3, w2, layer, xs_buf=None,
                 tm_e=MOE_TILE, tf=MOE_FF_TILE):
    rows = x.shape[0]
    n_tiles = -(-(TOP_K * rows + N_EXPERTS * (tm_e - 1)) // tm_e)
    x, gate, rank, expert, cnt = attn_out_router(x, o_prompt, o_sample, w_o, g, w_router)
    counts = cnt[:N_EXPERTS, 0].astype(jnp.int32)
    padded = ((counts + tm_e - 1) // tm_e) * tm_e
    ends = jnp.cumsum(padded)
    offs = ends - padded
    pos = rank
    for e in range(N_EXPERTS):
        pos = pos + jnp.where(expert == e, offs[e], 0)
    tile_start = jnp.arange(n_tiles, dtype=jnp.int32) * tm_e
    tile_expert = jnp.minimum(jnp.sum(ends[None, :] <= tile_start[:, None], axis=1), N_EXPERTS - 1).astype(jnp.int32)
    tile_valid = jnp.clip(counts[tile_expert] - (tile_start - offs[tile_expert]), 0, tm_e).astype(jnp.int32)
    if xs_buf is None:
        xs_buf = jnp.zeros((n_tiles * tm_e, SUBLANES, LANES), F32)
    xs = dispatch(x, pos, xs_buf)
    ys = moe_experts(xs, g, w1, w3, w2, layer, tile_expert, tile_valid, tm_e, tf)
    return combine(x, gate, pos, ys), xs


def _conv_rows(xpad, cw, cb):
    ext = xpad[...]
    y = None
    for k in range(CONV_W):
        back = CONV_W - 1 - k
        src = ext if back == 0 else pltpu.roll(ext, back, axis=0)
        term = src[HALO:] * cw[k : k + 1]
        y = term if y is None else y + term
    return y + cb


def _conv_taps(xpad, base, rows, step, cw, cb):
    y = xpad[base - 3 * step : base - 3 * step + rows, :] * cw[0:1]
    for k in range(1, CONV_W):
        lo = base - (CONV_W - 1 - k) * step
        y = y + xpad[lo : lo + rows, :] * cw[k : k + 1]
    return y + cb


def _lru_gates(xc, wa_ref, ba_ref, wx_ref, bx_ref, lam_ref):
    xcb = xc.astype(BF16)
    r_parts, i_parts = [], []
    for k in range(LRU_BLOCKS):
        blk = xcb[:, k * LRU_BW : (k + 1) * LRU_BW]
        r_parts.append(_dot(blk, wa_ref[k]))
        i_parts.append(_dot(blk, wx_ref[k]))
    r = _sigmoid(jnp.concatenate(r_parts, axis=1) + ba_ref[...])
    ig = _sigmoid(jnp.concatenate(i_parts, axis=1) + bx_ref[...])
    log_a = (-LRU_C * r) * _softplus(-lam_ref[...])
    a = jnp.exp(log_a)
    one_minus_a2 = -jnp.tanh(log_a) * (a * a + 1.0)
    u = jnp.sqrt(one_minus_a2) * (ig * xc)
    return a, u


def _lru_in_proj(x_ref, g_ref, win_ref, gate_scr, xpad, base):
    h = _rms(x_ref[...], g_ref[...]).astype(BF16)
    gate_scr[...] = _dot(h, win_ref[:, :D_RNN])
    xpad[base : base + x_ref.shape[0], :] = _dot(h, win_ref[:, D_RNN:])


def _lru_prompt_kernel(tt, x_ref, g_ref, win_ref, cw_ref, cb_ref, wa_ref, ba_ref, wx_ref, bx_ref, lam_ref,
                       y_ref, hlast_ref, convn_ref, xpad, gate_scr, a_scr, u_scr, hs_scr, h_scr):
    t = pl.program_id(1)

    @pl.when(t == 0)
    def _():
        xpad[0:HALO, :] = jnp.zeros((HALO, D_RNN), F32)
        h_scr[...] = jnp.zeros_like(h_scr)

    @pl.when(t > 0)
    def _():
        xpad[0:HALO, :] = xpad[tt : tt + HALO, :]

    _lru_in_proj(x_ref, g_ref, win_ref, gate_scr, xpad, HALO)
    xc = _conv_rows(xpad, cw_ref[...], cb_ref[...])
    a, u = _lru_gates(xc, wa_ref, ba_ref, wx_ref, bx_ref, lam_ref)
    a_scr[...] = a
    u_scr[...] = u

    def body(i, h):
        h = a_scr[pl.ds(i, 1), :] * h + u_scr[pl.ds(i, 1), :]
        hs_scr[pl.ds(i, 1), :] = h
        return h

    h_scr[...] = lax.fori_loop(0, tt, body, h_scr[...], unroll=8)
    y_ref[...] = hs_scr[...] * jax.nn.gelu(gate_scr[...])

    @pl.when(t == pl.num_programs(1) - 1)
    def _():
        hlast_ref[...] = h_scr[...]
        convn_ref[...] = xpad[tt : tt + HALO, :]


def lru_prompt(x, g, w_in, p, tt=256):
    n_t = SEQ // tt
    wspec = lambda shape: pl.BlockSpec(shape, lambda b, t: (0,) * len(shape))
    return pl.pallas_call(
        functools.partial(_lru_prompt_kernel, tt),
        grid=(BATCH, n_t),
        in_specs=[
            pl.BlockSpec((tt, D_MODEL), lambda b, t: (b * n_t + t, 0)),
            wspec((1, D_MODEL)), wspec((D_MODEL, 2 * D_RNN)),
            wspec((CONV_W, D_RNN)), wspec((1, D_RNN)),
            wspec((LRU_BLOCKS, LRU_BW, LRU_BW)), wspec((1, D_RNN)),
            wspec((LRU_BLOCKS, LRU_BW, LRU_BW)), wspec((1, D_RNN)),
            wspec((1, D_RNN)),
        ],
        out_specs=[
            pl.BlockSpec((tt, D_RNN), lambda b, t: (b * n_t + t, 0)),
            pl.BlockSpec((None, 1, D_RNN), lambda b, t: (b, 0, 0)),
            pl.BlockSpec((None, HALO, D_RNN), lambda b, t: (b, 0, 0)),
        ],
        out_shape=[
            jax.ShapeDtypeStruct((N_PROMPT, D_RNN), F32),
            jax.ShapeDtypeStruct((BATCH, 1, D_RNN), F32),
            jax.ShapeDtypeStruct((BATCH, HALO, D_RNN), F32),
        ],
        scratch_shapes=[
            pltpu.VMEM((HALO + tt, D_RNN), F32),
            pltpu.VMEM((tt, D_RNN), F32),
            pltpu.VMEM((tt, D_RNN), F32),
            pltpu.VMEM((tt, D_RNN), F32),
            pltpu.VMEM((tt, D_RNN), F32),
            pltpu.VMEM((1, D_RNN), F32),
        ],
        compiler_params=_cparams("arbitrary", "arbitrary"),
        name="lru_prompt",
    )(x, g.reshape(1, D_MODEL), w_in, p["conv_w"], p["conv_b"], p["w_a"], p["b_a"], p["w_x"], p["b_x"], p["lam"])


def _lru_sample_kernel(x_ref, g_ref, win_ref, convp_ref, hprev_ref, cw_ref, cb_ref, wa_ref, ba_ref, wx_ref, bx_ref,
                       lam_ref, y_ref, hlast_ref, convn_ref, xpad, gate_scr):
    hist = (CONV_W - 1) * DEC_BATCH
    xpad[0:hist, :] = convp_ref[...]
    _lru_in_proj(x_ref, g_ref, win_ref, gate_scr, xpad, hist)
    xc = _conv_taps(xpad, hist, N_SAMPLE, DEC_BATCH, cw_ref[...], cb_ref[...])
    a, u = _lru_gates(xc, wa_ref, ba_ref, wx_ref, bx_ref, lam_ref)
    gate = jax.nn.gelu(gate_scr[...])
    h = hprev_ref[...]
    for t in range(DEC_SEQ):
        rows = slice(t * DEC_BATCH, (t + 1) * DEC_BATCH)
        h = a[rows] * h + u[rows]
        y_ref[rows, :] = h * gate[rows]
    hlast_ref[...] = h
    convn_ref[...] = xpad[N_SAMPLE:, :]


def lru_sample(x, g, w_in, conv_prev, h_prev, p):
    hist = (CONV_W - 1) * DEC_BATCH
    full = lambda shape: pl.BlockSpec(shape, lambda i: (0,) * len(shape))
    return pl.pallas_call(
        _lru_sample_kernel,
        grid=(1,),
        in_specs=[
            pl.BlockSpec((N_SAMPLE, D_MODEL), lambda i: (SAMPLE_BLOCK, 0)),
            full((1, D_MODEL)), full((D_MODEL, 2 * D_RNN)),
            full((hist, D_RNN)), full((DEC_BATCH, D_RNN)),
            full((CONV_W, D_RNN)), full((1, D_RNN)),
            full((LRU_BLOCKS, LRU_BW, LRU_BW)), full((1, D_RNN)),
            full((LRU_BLOCKS, LRU_BW, LRU_BW)), full((1, D_RNN)),
            full((1, D_RNN)),
        ],
        out_specs=[
            full((N_SAMPLE, D_RNN)),
            full((DEC_BATCH, D_RNN)),
            full((hist, D_RNN)),
        ],
        out_shape=[
            jax.ShapeDtypeStruct((N_SAMPLE, D_RNN), F32),
            jax.ShapeDtypeStruct((DEC_BATCH, D_RNN), F32),
            jax.ShapeDtypeStruct((hist, D_RNN), F32),
        ],
        scratch_shapes=[pltpu.VMEM((hist + N_SAMPLE, D_RNN), F32), pltpu.VMEM((N_SAMPLE, D_RNN), F32)],
        compiler_params=_cparams("arbitrary"),
        name="lru_sample",
    )(x, g.reshape(1, D_MODEL), w_in, conv_prev, h_prev,
      p["conv_w"], p["conv_b"], p["w_a"], p["b_a"], p["w_x"], p["b_x"], p["lam"])


def _attend(q, k, v):
    outs = []
    for h in range(MEM_HEADS):
        hs = slice(h * MEM_HD, (h + 1) * MEM_HD)
        s = lax.dot_general(q[:, hs], k[:, hs], (((1,), (1,)), ((), ())), preferred_element_type=F32)
        s = s * (MEM_HD ** -0.5)
        e = jnp.exp(s - jnp.max(s, axis=-1, keepdims=True))
        p = e / jnp.sum(e, axis=-1, keepdims=True)
        outs.append(_dot(p.astype(BF16), v[:, hs]))
    return jnp.concatenate(outs, axis=1)


def _attend_sample(bb, q_ref, k_ref, v_ref, o_ref):
    rows = MEM_HEADS * DEC_SEQ
    cols = N_MEM * MEM_HEADS
    row_head = lax.broadcasted_iota(jnp.int32, (rows, cols), 0) // DEC_SEQ
    col_head = lax.broadcasted_iota(jnp.int32, (rows, cols), 1) % MEM_HEADS
    same_head = row_head == col_head
    for i in range(bb):
        k2 = k_ref[i].reshape(cols, MEM_HD).astype(BF16)
        v2 = v_ref[i].reshape(cols, MEM_HD).astype(BF16)
        q = q_ref[i]
        qh = jnp.concatenate([q[:, h * MEM_HD : (h + 1) * MEM_HD] for h in range(MEM_HEADS)], axis=0)
        s = lax.dot_general(qh, k2, (((1,), (1,)), ((), ())), preferred_element_type=F32) * (MEM_HD ** -0.5)
        s = jnp.where(same_head, s, -jnp.inf)
        e = jnp.exp(s - jnp.max(s, axis=-1, keepdims=True))
        p = e / jnp.sum(e, axis=-1, keepdims=True)
        oh = _dot(p.astype(BF16), v2)
        o_ref[i] = jnp.concatenate([oh[h * DEC_SEQ : (h + 1) * DEC_SEQ] for h in range(MEM_HEADS)], axis=1)


def _attention_kernel(bb, qp_ref, kp_ref, vp_ref, qs_ref, ks_ref, vs_ref, op_ref, os_ref):
    op_ref[...] = _attend(qp_ref[...], kp_ref[...].astype(BF16), vp_ref[...].astype(BF16)).astype(op_ref.dtype)
    _attend_sample(bb, qs_ref, ks_ref, vs_ref, os_ref)


def attention(q, q_sample, k, v, cache_k, cache_v, layer):
    tt = ROW_TILE
    n_t = SEQ // tt
    steps = BATCH * n_t
    bb = DEC_BATCH // steps
    cache = lambda: pl.BlockSpec((None, bb, N_MEM, MEM_HEADS, MEM_HD), lambda i: (layer, i, 0, 0, 0))
    mem = lambda: pl.BlockSpec((None, N_MEM, D_MODEL), lambda i: (layer, i // n_t, 0))
    return pl.pallas_call(
        functools.partial(_attention_kernel, bb),
        grid=(steps,),
        in_specs=[
            pl.BlockSpec((tt, D_MODEL), lambda i: (i, 0)), mem(), mem(),
            pl.BlockSpec((bb, DEC_SEQ, D_MODEL), lambda i: (i, 0, 0)), cache(), cache(),
        ],
        out_specs=[
            pl.BlockSpec((tt, D_MODEL), lambda i: (i, 0)),
            pl.BlockSpec((bb, DEC_SEQ, D_MODEL), lambda i: (i, 0, 0)),
        ],
        out_shape=[
            jax.ShapeDtypeStruct((N_PROMPT, D_MODEL), BF16),
            jax.ShapeDtypeStruct((DEC_BATCH, DEC_SEQ, D_MODEL), F32),
        ],
        compiler_params=_cparams("arbitrary"),
        name="attention",
    )(q, k, v, q_sample, cache_k, cache_v)


def _ssd_gate_norm(y, z, ng):
    y = y * _silu(z)
    outs = []
    for g in range(SSD_GROUPS):
        yg = y[:, g * SSD_GROUP_W : (g + 1) * SSD_GROUP_W]
        outs.append(yg * lax.rsqrt(jnp.mean(yg * yg, axis=-1, keepdims=True) + EPS))
    return jnp.concatenate(outs, axis=1) * ng


def _ssd_kernel(q, slot, n_sample_in, z_ref, xbc_ref, dt_ref, cw_ref, cb_ref, dtb_ref, alog_ref, dskip_ref, ng_ref,
                e_ref, *rest):
    sample_in = rest[:5]
    y_ref, st_ref, convn_ref, ys_ref, sts_ref, xpad, s_t, y_scr = rest[n_sample_in:]
    _ssd_recur_body(1, slot, *sample_in, ys_ref, sts_ref)
    t = pl.program_id(1)

    @pl.when(t == 0)
    def _():
        xpad[0:HALO, :] = jnp.zeros((HALO, SSD_CONV_DIM), F32)
        s_t[...] = jnp.zeros_like(s_t)

    @pl.when(t > 0)
    def _():
        xpad[0:HALO, :] = xpad[q : q + HALO, :]

    xpad[HALO : HALO + q, :] = xbc_ref[...]
    xbc = _silu(_conv_rows(xpad, cw_ref[...], cb_ref[...]))
    xs = xbc[:, :D_INNER]

    dt = _softplus(dt_ref[...] + dtb_ref[...])
    adt = -jnp.exp(alog_ref[...]) * dt
    row_i = lax.broadcasted_iota(jnp.int32, (q, q), 0)
    col_i = lax.broadcasted_iota(jnp.int32, (q, q), 1)
    tril = row_i >= col_i
    a_cs = _dot_f32_rhs(jnp.where(tril, 1.0, 0.0).astype(BF16), adt)
    a_cs_t = a_cs.T
    a_end = a_cs[q - 1 : q, :]
    expand = e_ref[...]
    ecs_x = _expand_heads(jnp.exp(a_cs), expand)
    xb = (xs * _expand_heads(dt, expand)).astype(BF16)
    xd = (xs * _expand_heads(dt * jnp.exp(a_end - a_cs), expand)).astype(BF16)

    for g in range(SSD_GROUPS):
        gc = slice(g * SSD_GROUP_W, (g + 1) * SSD_GROUP_W)
        bg = xbc[:, D_INNER + g * SSD_STATE : D_INNER + (g + 1) * SSD_STATE]
        cg = xbc[:, D_INNER + SSD_GN + g * SSD_STATE : D_INNER + SSD_GN + (g + 1) * SSD_STATE].astype(BF16)
        cb_mat = lax.dot_general(cg, bg.astype(BF16), (((1,), (1,)), ((), ())), preferred_element_type=F32)
        sg = s_t[:, gc]
        y_scr[:, gc] = _dot(cg, sg.astype(BF16)) * ecs_x[:, gc]
        s_t[:, gc] = ecs_x[q - 1 : q, gc] * sg + _dot(bg.T.astype(BF16), xd[:, gc])
        for e in range(SSD_HEADS // SSD_GROUPS):
            h = g * (SSD_HEADS // SSD_GROUPS) + e
            hc = slice(h * SSD_HEADDIM, (h + 1) * SSD_HEADDIM)
            seg = a_cs[:, h : h + 1] - a_cs_t[h : h + 1, :]
            decay = jnp.where(tril, jnp.exp(jnp.minimum(seg, 0.0)), 0.0)
            y_scr[:, hc] += _dot((cb_mat * decay).astype(BF16), xb[:, hc])

    y = y_scr[...] + dskip_ref[...] * xs
    y_ref[...] = _ssd_gate_norm(y, z_ref[...], ng_ref[...]).astype(y_ref.dtype)

    @pl.when(t == pl.num_programs(1) - 1)
    def _():
        for j in range(D_INNER // LANES):
            st_ref[j * LANES : (j + 1) * LANES, :] = s_t[:, j * LANES : (j + 1) * LANES].T
        convn_ref[...] = xpad[q : q + HALO, :]


def ssd(z, xbc, dt, p, xdt_s, dec_s, bm_s, cm_s, state, layer, new_states=None, q=SSD_CHUNK):
    n_t = SEQ // q
    assert BATCH * n_t == DEC_BATCH
    n_layers = state.shape[0]
    rows = lambda w: pl.BlockSpec((q, w), lambda b, t: (b * n_t + t, 0))
    wspec = lambda shape: pl.BlockSpec(shape, lambda b, t: (0,) * len(shape))
    seq = lambda w: pl.BlockSpec((1, DEC_SEQ, w), lambda b, t: (b * n_t + t, 0, 0))
    in_specs = [
        rows(D_INNER), rows(SSD_CONV_DIM), rows(LANES),
        wspec((CONV_W, SSD_CONV_DIM)), wspec((1, SSD_CONV_DIM)),
        wspec((1, LANES)), wspec((1, LANES)), wspec((1, D_INNER)), wspec((1, D_INNER)),
        wspec((LANES, D_INNER)),
        seq(D_INNER), seq(D_INNER), seq(SSD_GN), seq(SSD_GN),
        pl.BlockSpec((None, 1, D_INNER, SSD_STATE), lambda b, t: (layer, b * n_t + t, 0, 0)),
    ]
    args = [z, xbc, dt, p["conv_w"], p["conv_b"], p["dt_bias"], p["a_log"], p["d_skip"], p["norm_g"], p["expand"],
            xdt_s, dec_s, bm_s, cm_s, state]
    if new_states is None:
        slot = layer
        st_out = pl.BlockSpec((n_layers, 1, D_INNER, SSD_STATE), lambda b, t: (0, b * n_t + t, 0, 0))
        aliases = {}
    else:
        slot = 0
        st_out = pl.BlockSpec((1, 1, D_INNER, SSD_STATE), lambda b, t: (layer, b * n_t + t, 0, 0))
        in_specs.append(pl.BlockSpec(memory_space=pl.ANY))
        args.append(new_states)
        aliases = {len(args) - 1: 4}
    return pl.pallas_call(
        functools.partial(_ssd_kernel, q, slot, len(args) - 10),
        grid=(BATCH, n_t),
        in_specs=in_specs,
        out_specs=[
            rows(D_INNER),
            pl.BlockSpec((None, D_INNER, SSD_STATE), lambda b, t: (b, 0, 0)),
            pl.BlockSpec((None, HALO, SSD_CONV_DIM), lambda b, t: (b, 0, 0)),
            seq(D_INNER),
            st_out,
        ],
        out_shape=[
            jax.ShapeDtypeStruct((N_PROMPT, D_INNER), BF16),
            jax.ShapeDtypeStruct((BATCH, D_INNER, SSD_STATE), F32),
            jax.ShapeDtypeStruct((BATCH, HALO, SSD_CONV_DIM), F32),
            jax.ShapeDtypeStruct((DEC_BATCH, DEC_SEQ, D_INNER), F32),
            jax.ShapeDtypeStruct((n_layers, DEC_BATCH, D_INNER, SSD_STATE), F32),
        ],
        scratch_shapes=[
            pltpu.VMEM((HALO + q, SSD_CONV_DIM), F32),
            pltpu.VMEM((SSD_STATE, D_INNER), F32),
            pltpu.VMEM((q, D_INNER), F32),
        ],
        input_output_aliases=aliases,
        compiler_params=_cparams("arbitrary", "arbitrary"),
        name="ssd",
    )(*args)


def _ssd_sample_pre_kernel(xbc_ref, dt_ref, convp_ref, cw_ref, cb_ref, dtb_ref, alog_ref, e_ref,
                           xs_ref, xdt_ref, dec_ref, bm_ref, cm_ref, convn_ref, xpad):
    hist = (CONV_W - 1) * DEC_BATCH
    xpad[0:hist, :] = convp_ref[...]
    xpad[hist:, :] = xbc_ref[...]
    xc = _conv_taps(xpad, hist, N_SAMPLE, DEC_BATCH, cw_ref[...], cb_ref[...])
    xbc = xc * _sigmoid(xc)
    xs = xbc[:, :D_INNER]
    dt = _softplus(dt_ref[...] + dtb_ref[...])
    adt = -jnp.exp(alog_ref[...]) * dt
    expand = e_ref[...]
    xs_ref[...] = xs
    xdt_ref[...] = xs * _dot_f32_lhs(dt, expand)
    dec_ref[...] = _dot_f32_lhs(jnp.exp(adt), expand)
    bm_ref[...] = xbc[:, D_INNER : D_INNER + SSD_GN]
    cm_ref[...] = xbc[:, D_INNER + SSD_GN :]
    convn_ref[...] = xpad[N_SAMPLE:, :]


def ssd_sample_pre(xbc, dt, conv_prev, p):
    hist = (CONV_W - 1) * DEC_BATCH
    full = lambda shape: pl.BlockSpec(shape, lambda i: (0,) * len(shape))
    out_w = [D_INNER, D_INNER, D_INNER, SSD_GN, SSD_GN]
    return pl.pallas_call(
        _ssd_sample_pre_kernel,
        grid=(1,),
        in_specs=[
            pl.BlockSpec((N_SAMPLE, SSD_CONV_DIM), lambda i: (SAMPLE_BLOCK, 0)),
            pl.BlockSpec((N_SAMPLE, LANES), lambda i: (SAMPLE_BLOCK, 0)),
            full((hist, SSD_CONV_DIM)),
            full((CONV_W, SSD_CONV_DIM)), full((1, SSD_CONV_DIM)),
            full((1, LANES)), full((1, LANES)), full((LANES, D_INNER)),
        ],
        out_specs=[full((N_SAMPLE, w)) for w in out_w] + [full((hist, SSD_CONV_DIM))],
        out_shape=[jax.ShapeDtypeStruct((N_SAMPLE, w), F32) for w in out_w]
        + [jax.ShapeDtypeStruct((hist, SSD_CONV_DIM), F32)],
        scratch_shapes=[pltpu.VMEM((hist + N_SAMPLE, SSD_CONV_DIM), F32)],
        compiler_params=_cparams("arbitrary"),
        name="ssd_sample_pre",
    )(xbc, dt, conv_prev, p["conv_w"], p["conv_b"], p["dt_bias"], p["a_log"], p["expand"])


def _ssd_recur_body(bb, slot, xdt_ref, dec_ref, bm_ref, cm_ref, st_in_ref, y_ref, st_out_ref):
    nt = (((1,), (1,)), ((), ()))
    tn = (((0,), (0,)), ((), ()))
    last = DEC_SEQ - 1
    ones = jnp.ones((SUBLANES, SSD_STATE), BF16)
    for i in range(bb):
        xdt = xdt_ref[i]
        dec = dec_ref[i]
        decay = [dec[0:1]]
        for t in range(1, DEC_SEQ):
            decay.append(decay[t - 1] * dec[t : t + 1])
        prop = {(s, s): xdt[s : s + 1] for s in range(DEC_SEQ)}
        for t in range(1, DEC_SEQ):
            for s in range(t):
                prop[(t, s)] = dec[t : t + 1] * prop[(t - 1, s)]
        d3 = jnp.concatenate(list(_split3(decay[last])) + [jnp.zeros((SUBLANES - 3, D_INNER), BF16)], axis=0)
        p_last = jnp.concatenate([prop[(last, s)] for s in range(DEC_SEQ)], axis=0).astype(BF16)
        for g in range(SSD_GROUPS):
            gc = slice(g * SSD_GROUP_W, (g + 1) * SSD_GROUP_W)
            bg = bm_ref[i, :, g * SSD_STATE : (g + 1) * SSD_STATE].astype(BF16)
            cg = cm_ref[i, :, g * SSD_STATE : (g + 1) * SSD_STATE].astype(BF16)
            sg = st_in_ref[i, gc, :]
            c_h = lax.dot_general(cg, sg.astype(BF16), nt, preferred_element_type=F32)
            c_b = lax.dot_general(cg, bg, nt, preferred_element_type=F32)
            for t in range(DEC_SEQ):
                y = decay[t][:, gc] * c_h[t : t + 1]
                for s in range(t + 1):
                    y = y + jnp.broadcast_to(c_b[t : t + 1, s : s + 1], (1, SSD_GROUP_W)) * prop[(t, s)][:, gc]
                y_ref[i, t : t + 1, gc] = y
            d_col = lax.dot_general(d3[:, gc], ones, tn, preferred_element_type=F32)
            st_out_ref[slot, i, gc, :] = d_col * sg + lax.dot_general(p_last[:, gc], bg, tn, preferred_element_type=F32)
    for other in range(st_out_ref.shape[0]):
        if other != slot:
            st_out_ref[other] = jnp.zeros(st_out_ref.shape[1:], F32)


def _ssd_sample_post_kernel(yr_ref, xs_ref, z_ref, dskip_ref, ng_ref, y_ref):
    y = yr_ref[...] + dskip_ref[...] * xs_ref[...]
    y_ref[...] = _ssd_gate_norm(y, z_ref[...], ng_ref[...]).astype(y_ref.dtype)


def ssd_sample_post(y_raw, xs, z, p):
    full = lambda shape: pl.BlockSpec(shape, lambda i: (0,) * len(shape))
    return pl.pallas_call(
        _ssd_sample_post_kernel,
        grid=(1,),
        in_specs=[
            full((N_SAMPLE, D_INNER)), full((N_SAMPLE, D_INNER)),
            pl.BlockSpec((N_SAMPLE, D_INNER), lambda i: (SAMPLE_BLOCK, 0)),
            full((1, D_INNER)), full((1, D_INNER)),
        ],
        out_specs=full((N_SAMPLE, D_INNER)),
        out_shape=jax.ShapeDtypeStruct((N_SAMPLE, D_INNER), BF16),
        compiler_params=_cparams("arbitrary"),
        name="ssd_sample_post",
    )(y_raw, xs, z, p["d_skip"], p["norm_g"])


def _to_time_major(a):
    return jnp.swapaxes(a, 0, 1).reshape(a.shape[0] * a.shape[1], a.shape[2])


def _to_batch_major(a, t):
    return jnp.swapaxes(a.reshape(t, DEC_BATCH, a.shape[1]), 0, 1)


def _row(v):
    return v.reshape(1, -1).astype(F32)


def _pad_lanes(v):
    return jnp.pad(v.reshape(1, -1).astype(F32), ((0, 0), (0, LANES - v.shape[-1])))


def kernel(x_prompt, x_sample, state_lru_h, state_lru_conv, state_ssd, state_ssd_conv, cache_mem_k, cache_mem_v, mem_prompt, norm_mix, norm_mem, norm_memkv, norm_ffn, norm_final, lru_w_in, lru_conv_w, lru_conv_b, lru_w_a, lru_b_a, lru_w_x, lru_b_x, lru_lam, lru_w_out, ssd_w_in, ssd_conv_w, ssd_conv_b, ssd_dt_bias, ssd_a_log, ssd_d, ssd_norm_g, ssd_w_out, mem_w_q, mem_w_k, mem_w_v, mem_w_o, ffn_w1, ffn_w3, ffn_w2, moe_router, moe_w1, moe_w3, moe_w2):
    bf = lambda w: w.astype(BF16)
    x = concat_rows(x_prompt.reshape(N_PROMPT, D_MODEL), _to_time_major(x_sample))
    mem = mem_prompt.reshape(BATCH * N_MEM, D_MODEL)
    head_of_col = jnp.arange(D_INNER, dtype=jnp.int32) // SSD_HEADDIM
    expand = (jnp.arange(LANES, dtype=jnp.int32)[:, None] == head_of_col[None, :]).astype(BF16)

    p_lru_h, p_lru_conv, p_ssd, p_ssd_conv = [], [], [], []
    s_lru_h, s_lru_conv, s_ssd_conv = [], [], []
    s_ssd = None
    xs_buf = None
    hist = CONV_W - 1
    mk, mv, p_mk, p_mv = mem_kv(mem, norm_memkv, mem_w_k, mem_w_v)
    for i in range(DEPTH):
        j = i // 2
        if i % 2 == 0:
            p = dict(conv_w=lru_conv_w[j], conv_b=_row(lru_conv_b[j]), w_a=bf(lru_w_a[j]), b_a=_row(lru_b_a[j]),
                     w_x=bf(lru_w_x[j]), b_x=_row(lru_b_x[j]), lam=_row(lru_lam[j]))
            w_in = bf(lru_w_in[j])
            y_p, h_p, c_p = lru_prompt(x, norm_mix[i], w_in, p)
            y_s, h_s, c_s = lru_sample(x, norm_mix[i], w_in, _to_time_major(state_lru_conv[j]), state_lru_h[j], p)
            p_lru_h.append(h_p.reshape(BATCH, D_RNN))
            p_lru_conv.append(c_p[:, HALO - hist :, :])
            s_lru_h.append(h_s)
            s_lru_conv.append(_to_batch_major(c_s, hist))
            w_out = bf(lru_w_out[j])
        else:
            w_in = ssd_w_in[j]
            w_z = bf(w_in[:, :D_INNER])
            w_xbc = bf(w_in[:, D_INNER : D_INNER + SSD_CONV_DIM])
            w_dt = bf(jnp.pad(w_in[:, D_INNER + SSD_CONV_DIM :], ((0, 0), (0, LANES - SSD_HEADS))))
            p = dict(conv_w=ssd_conv_w[j], conv_b=_row(ssd_conv_b[j]), dt_bias=_pad_lanes(ssd_dt_bias[j]),
                     a_log=_pad_lanes(ssd_a_log[j]), d_skip=_row(jnp.repeat(ssd_d[j], SSD_HEADDIM)),
                     norm_g=_row(ssd_norm_g[j]), expand=expand)
            z, xbc, dt = norm_matmul(x, norm_mix[i], [w_z, w_xbc, w_dt], [F32, F32, F32], tm=256)
            xs_s, xdt_s, dec_s, bm_s, cm_s, c_s = ssd_sample_pre(xbc, dt, _to_time_major(state_ssd_conv[j]), p)
            y_p, st_p, c_p, y_raw, s_ssd = ssd(
                z, xbc, dt, p,
                _to_batch_major(xdt_s, DEC_SEQ), _to_batch_major(dec_s, DEC_SEQ),
                _to_batch_major(bm_s, DEC_SEQ), _to_batch_major(cm_s, DEC_SEQ),
                state_ssd.reshape(-1, DEC_BATCH, D_INNER, SSD_STATE), j, s_ssd)
            y_s = ssd_sample_post(_to_time_major(y_raw), xs_s, z, p)
            p_ssd.append(st_p.reshape(BATCH, SSD_HEADS, SSD_HEADDIM, SSD_STATE))
            p_ssd_conv.append(c_p[:, HALO - hist :, :])
            s_ssd_conv.append(_to_batch_major(c_s, hist))
            w_out = bf(ssd_w_out[j])

        x, qp = out_proj_q(y_p, y_s, w_out, x, norm_mem[i], bf(mem_w_q[i]))
        o_p, o_s = attention(qp, _to_batch_major(qp[N_PROMPT:], DEC_SEQ), mk, mv, cache_mem_k, cache_mem_v, i)
        o_s = _to_time_major(o_s)
        if i % 2 == 0:
            x = attn_out_ffn(x, o_p, o_s, bf(mem_w_o[i]), norm_ffn[i],
                             bf(ffn_w1[j]), bf(ffn_w3[j]), bf(ffn_w2[j]), tf=D_FF // 2)
        else:
            x, xs_buf = attn_out_moe(x, o_p, o_s, bf(mem_w_o[i]), norm_ffn[i], moe_router[j],
                                     moe_w1, moe_w3, moe_w2, j, xs_buf)

    y_prompt = rmsnorm_rows(x, norm_final, 0, N_PROMPT // ROW_TILE).reshape(BATCH, SEQ, D_MODEL)
    y_sample = _to_batch_major(rmsnorm_rows(x, norm_final, N_PROMPT // ROW_TILE, N_SAMPLE // ROW_TILE), DEC_SEQ)
    return (y_prompt, y_sample,
            jnp.stack(p_lru_h), jnp.stack(p_lru_conv), jnp.stack(p_ssd), jnp.stack(p_ssd_conv),
            p_mk, p_mv,
            jnp.stack(s_lru_h), jnp.stack(s_lru_conv), s_ssd.reshape(state_ssd.shape), jnp.stack(s_ssd_conv))
```

```python
import functools
import math

import jax
import jax.numpy as jnp
from jax import lax
from jax.experimental import pallas as pl
from jax.experimental.pallas import tpu as pltpu

F32 = jnp.float32
BF16 = jnp.bfloat16

D_MODEL = 1024
BATCH = 8
SEQ = 2048
DEPTH = 4
DEC_BATCH = 128
DEC_SEQ = 4
CONV_W = 4
EPS = 1e-6
D_RNN = D_MODEL
LRU_BLOCKS = 8
LRU_BW = D_RNN // LRU_BLOCKS
LRU_C = 8.0
D_INNER = 2 * D_MODEL
SSD_HEADDIM = 64
SSD_HEADS = D_INNER // SSD_HEADDIM
SSD_GROUPS = 4
SSD_GROUP_W = D_INNER // SSD_GROUPS
SSD_STATE = 128
SSD_GN = SSD_GROUPS * SSD_STATE
SSD_CONV_DIM = D_INNER + 2 * SSD_GN
SSD_CHUNK = 128
N_MEM = 256
MEM_HEADS = 4
MEM_HD = D_MODEL // MEM_HEADS
D_FF = 2816
N_EXPERTS = 8
TOP_K = 2
D_FF_EXPERT = 3584

LANES = 128
SUBLANES = 8
VMEM_LIMIT_BYTES = 56 * 1024 * 1024

N_PROMPT = BATCH * SEQ
N_SAMPLE = DEC_BATCH * DEC_SEQ
N_ROWS = N_PROMPT + N_SAMPLE
ROW_TILE = 512
SAMPLE_BLOCK = N_PROMPT // N_SAMPLE
HALO = SUBLANES
MOE_TILE = 1024
MOE_FF_TILE = 512


def _cparams(*sem):
    return pltpu.CompilerParams(dimension_semantics=sem, vmem_limit_bytes=VMEM_LIMIT_BYTES)


def _rms(x, g):
    return x * lax.rsqrt(jnp.mean(x * x, axis=-1, keepdims=True) + EPS) * g


def _sigmoid(x):
    return 1.0 / (1.0 + jnp.exp(-x))


def _silu(x):
    h = 0.5 * x
    return h + h * jnp.tanh(h)


def _softplus(x):
    return jnp.maximum(x, 0.0) + jnp.log1p(jnp.exp(-jnp.abs(x)))


def _split3(x):
    a = x.astype(BF16)
    r = x - a.astype(F32)
    b = r.astype(BF16)
    c = (r - b.astype(F32)).astype(BF16)
    return a, b, c


def _dot(a, b):
    return jnp.dot(a, b, preferred_element_type=F32)


def _dot_f32_lhs(x, m):
    a, b, c = _split3(x)
    return _dot(a, m) + _dot(b, m) + _dot(c, m)


def _expand_heads(x, expand):
    hi = x.astype(BF16)
    lo = (x - hi.astype(F32)).astype(BF16)
    return _dot(hi, expand) + _dot(lo, expand)


def _dot_f32_rhs(m, x):
    a, b, c = _split3(x)
    return _dot(m, a) + _dot(m, b) + _dot(m, c)


def _norm_matmul_kernel(n_w, x_ref, g_ref, *refs):
    h = _rms(x_ref[...], g_ref[...]).astype(BF16)
    for w_ref, o_ref in zip(refs[:n_w], refs[n_w:]):
        o_ref[...] = _dot(h, w_ref[...]).astype(o_ref.dtype)


def norm_matmul(x, g, ws, out_dtypes, tm=ROW_TILE):
    rows, k = x.shape
    in_specs = [pl.BlockSpec((tm, k), lambda i: (i, 0)), pl.BlockSpec((1, k), lambda i: (0, 0))]
    in_specs += [pl.BlockSpec(w.shape, lambda i: (0, 0)) for w in ws]
    out_specs = [pl.BlockSpec((tm, w.shape[1]), lambda i: (i, 0)) for w in ws]
    out_shape = [jax.ShapeDtypeStruct((rows, w.shape[1]), dt) for w, dt in zip(ws, out_dtypes)]
    return pl.pallas_call(
        functools.partial(_norm_matmul_kernel, len(ws)),
        grid=(rows // tm,),
        in_specs=in_specs,
        out_specs=out_specs,
        out_shape=out_shape,
        compiler_params=_cparams("parallel"),
        name="norm_matmul",
    )(x, g.reshape(1, k), *ws)


def _concat_rows_kernel(n_p, a_ref, b_ref, o_ref):
    i = pl.program_id(0)

    @pl.when(i < n_p)
    def _():
        o_ref[...] = a_ref[...]

    @pl.when(i >= n_p)
    def _():
        o_ref[...] = b_ref[...]


def concat_rows(a, b, tm=ROW_TILE):
    k = a.shape[1]
    n_p = a.shape[0] // tm
    n_s = b.shape[0] // tm
    return pl.pallas_call(
        functools.partial(_concat_rows_kernel, n_p),
        grid=(n_p + n_s,),
        in_specs=[
            pl.BlockSpec((tm, k), lambda i: (jnp.minimum(i, n_p - 1), 0)),
            pl.BlockSpec((tm, k), lambda i: (jnp.maximum(i - n_p, 0), 0)),
        ],
        out_specs=pl.BlockSpec((tm, k), lambda i: (i, 0)),
        out_shape=jax.ShapeDtypeStruct((a.shape[0] + b.shape[0], k), a.dtype),
        compiler_params=_cparams("arbitrary"),
        name="concat_rows",
    )(a, b)


def _mem_kv_kernel(bt, m_ref, g_ref, wk_ref, wv_ref, k2_ref, v2_ref, k4_ref, v4_ref):
    h = _rms(m_ref[...], g_ref[...]).astype(BF16)
    for w_ref, o2_ref, o4_ref in ((wk_ref, k2_ref, k4_ref), (wv_ref, v2_ref, v4_ref)):
        r = _dot(h, w_ref[...].astype(BF16))
        o2_ref[...] = r
        o4_ref[...] = r.reshape(bt, N_MEM, MEM_HEADS, MEM_HD)


def mem_kv(mem, g, w_k, w_v, bt=2):
    rows, k = mem.shape
    tm = bt * N_MEM
    flat = lambda: pl.BlockSpec((None, tm, k), lambda l, i: (l, i, 0))
    heads = lambda: pl.BlockSpec((None, bt, N_MEM, MEM_HEADS, MEM_HD), lambda l, i: (l, i, 0, 0, 0))
    wspec = lambda: pl.BlockSpec((None, k, k), lambda l, i: (l, 0, 0))
    flat_shape = jax.ShapeDtypeStruct((DEPTH, rows, k), F32)
    heads_shape = jax.ShapeDtypeStruct((DEPTH, BATCH, N_MEM, MEM_HEADS, MEM_HD), F32)
    return pl.pallas_call(
        functools.partial(_mem_kv_kernel, bt),
        grid=(DEPTH, rows // tm),
        in_specs=[
            pl.BlockSpec((tm, k), lambda l, i: (i, 0)),
            pl.BlockSpec((None, 1, k), lambda l, i: (l, 0, 0)),
            wspec(), wspec(),
        ],
        out_specs=[flat(), flat(), heads(), heads()],
        out_shape=[flat_shape, flat_shape, heads_shape, heads_shape],
        compiler_params=_cparams("arbitrary", "arbitrary"),
        name="mem_kv",
    )(mem, g.reshape(DEPTH, 1, k), w_k, w_v)


def _two_part_specs(a_prompt, a_sample, tm):
    n_p = a_prompt.shape[0] // tm
    specs = [
        pl.BlockSpec((tm, a_prompt.shape[1]), lambda i, *_: (jnp.minimum(i, n_p - 1), 0)),
        pl.BlockSpec((tm, a_sample.shape[1]), lambda i, *_: (jnp.maximum(i - n_p, 0), 0)),
    ]
    return specs, n_p


def _on_row_part(n_p, prompt_ref, sample_ref, fn):
    i = pl.program_id(0)

    @pl.when(i < n_p)
    def _():
        fn(prompt_ref[...])

    @pl.when(i >= n_p)
    def _():
        fn(sample_ref[...])


def _out_proj_q_kernel(n_p, yp_ref, ys_ref, w_ref, r_ref, g_ref, wq_ref, o_ref, q_ref):
    def finish(y):
        x_new = r_ref[...] + _dot(y.astype(BF16), w_ref[...])
        o_ref[...] = x_new
        q_ref[...] = _dot(_rms(x_new, g_ref[...]).astype(BF16), wq_ref[...]).astype(q_ref.dtype)

    _on_row_part(n_p, yp_ref, ys_ref, finish)


def out_proj_q(y_prompt, y_sample, w, res, g, w_q, tm=ROW_TILE):
    k, n = w.shape
    y_specs, n_p = _two_part_specs(y_prompt, y_sample, tm)
    full = lambda shape: pl.BlockSpec(shape, lambda i: (0, 0))
    rows = lambda: pl.BlockSpec((tm, n), lambda i: (i, 0))
    return pl.pallas_call(
        functools.partial(_out_proj_q_kernel, n_p),
        grid=(res.shape[0] // tm,),
        in_specs=y_specs + [full((k, n)), rows(), full((1, n)), full((n, n))],
        out_specs=[rows(), rows()],
        out_shape=[jax.ShapeDtypeStruct(res.shape, F32), jax.ShapeDtypeStruct(res.shape, BF16)],
        compiler_params=_cparams("arbitrary"),
        name="out_proj_q",
    )(y_prompt, y_sample, w, res, g.reshape(1, n), w_q)


def _rmsnorm_kernel(x_ref, g_ref, o_ref):
    o_ref[...] = _rms(x_ref[...], g_ref[...])


def rmsnorm_rows(x, g, first_block, n_blocks, tm=ROW_TILE):
    k = x.shape[1]
    return pl.pallas_call(
        _rmsnorm_kernel,
        grid=(n_blocks,),
        in_specs=[pl.BlockSpec((tm, k), lambda i: (i + first_block, 0)), pl.BlockSpec((1, k), lambda i: (0, 0))],
        out_specs=pl.BlockSpec((tm, k), lambda i: (i, 0)),
        out_shape=jax.ShapeDtypeStruct((n_blocks * tm, k), F32),
        compiler_params=_cparams("parallel"),
        name="final_norm",
    )(x, g.reshape(1, k))


def _swiglu_partial(h, w1_ref, w3_ref, w2_ref):
    a = _dot(h, w1_ref[...])
    b = _dot(h, w3_ref[...])
    return _dot((a * _sigmoid(a) * b).astype(BF16), w2_ref[...])


def _ffn_kernel(n_p, x_ref, op_ref, os_ref, wo_ref, g_ref, w1_ref, w3_ref, w2_ref, o_ref, xn_scr, h_scr, acc_scr):
    j = pl.program_id(1)

    @pl.when(j == 0)
    def _():
        def start(o):
            xn = x_ref[...] + _dot(o.astype(BF16), wo_ref[...])
            xn_scr[...] = xn
            h_scr[...] = _rms(xn, g_ref[...]).astype(BF16)

        _on_row_part(n_p, op_ref, os_ref, start)
        acc_scr[...] = jnp.zeros_like(acc_scr)

    acc_scr[...] += _swiglu_partial(h_scr[...], w1_ref, w3_ref, w2_ref)

    @pl.when(j == pl.num_programs(1) - 1)
    def _():
        o_ref[...] = xn_scr[...] + acc_scr[...]


def attn_out_ffn(x, o_prompt, o_sample, w_o, g, w1, w3, w2, tf, tm=ROW_TILE):
    rows, k = x.shape
    f = w1.shape[1]
    o_specs, n_p = _two_part_specs(o_prompt, o_sample, tm)
    return pl.pallas_call(
        functools.partial(_ffn_kernel, n_p),
        grid=(rows // tm, f // tf),
        in_specs=[pl.BlockSpec((tm, k), lambda i, j: (i, 0))] + o_specs + [
            pl.BlockSpec((k, k), lambda i, j: (0, 0)),
            pl.BlockSpec((1, k), lambda i, j: (0, 0)),
            pl.BlockSpec((k, tf), lambda i, j: (0, j)),
            pl.BlockSpec((k, tf), lambda i, j: (0, j)),
            pl.BlockSpec((tf, k), lambda i, j: (j, 0)),
        ],
        out_specs=pl.BlockSpec((tm, k), lambda i, j: (i, 0)),
        out_shape=jax.ShapeDtypeStruct((rows, k), F32),
        scratch_shapes=[pltpu.VMEM((tm, k), F32), pltpu.VMEM((tm, k), BF16), pltpu.VMEM((tm, k), F32)],
        compiler_params=_cparams("arbitrary", "arbitrary"),
        name="attn_out_ffn",
    )(x, o_prompt, o_sample, w_o, g.reshape(1, k), w1, w3, w2)


def _router_kernel(tm, n_p, x_ref, op_ref, os_ref, wo_ref, g_ref, wr_ref,
                   xn_ref, gate_ref, rank_ref, exp_ref, cnt_ref, carry):
    @pl.when(pl.program_id(0) == 0)
    def _():
        carry[...] = jnp.zeros_like(carry)

    def add_attention(o):
        xn_ref[...] = x_ref[...] + _dot(o.astype(BF16), wo_ref[...])

    _on_row_part(n_p, op_ref, os_ref, add_attention)
    h = _rms(xn_ref[...], g_ref[...])
    h1 = h.astype(BF16)
    h2 = (h - h1.astype(F32)).astype(BF16)
    w = wr_ref[...]
    w1 = w.astype(BF16)
    w2 = (w - w1.astype(F32)).astype(BF16)
    logits = _dot(h1, w1) + _dot(h1, w2) + _dot(h2, w1)
    lane = lax.broadcasted_iota(jnp.int32, logits.shape, 1).astype(F32)
    neg = jnp.float32(-jnp.inf)
    logits = jnp.where(lane < N_EXPERTS, logits, neg)
    m1 = jnp.max(logits, axis=-1, keepdims=True)
    i1 = jnp.min(jnp.where(logits == m1, lane, float(LANES)), axis=-1, keepdims=True)
    rest = jnp.where(lane == i1, neg, logits)
    m2 = jnp.max(rest, axis=-1, keepdims=True)
    i2 = jnp.min(jnp.where(rest == m2, lane, float(LANES)), axis=-1, keepdims=True)
    e2 = jnp.exp(m2 - m1)
    den = 1.0 + e2
    gate_ref[...] = jnp.where(lane == 0.0, 1.0 / den, 0.0) + jnp.where(lane == 1.0, e2 / den, 0.0)

    oh1_t = jnp.where(lane == i1, 1.0, 0.0).T
    oh2_t = jnp.where(lane == i2, 1.0, 0.0).T
    oh_t = oh1_t + oh2_t
    src = lax.broadcasted_iota(jnp.int32, (tm, tm), 0)
    dst = lax.broadcasted_iota(jnp.int32, (tm, tm), 1)
    earlier = jnp.where(src < dst, 1.0, 0.0).astype(BF16)
    before = _dot(oh_t.astype(BF16), earlier)
    base = jnp.concatenate([carry[...]] * (tm // LANES), axis=1) + before
    expert_id = lax.broadcasted_iota(jnp.int32, (LANES, tm), 0).astype(F32)
    col_sum = lambda a: jnp.sum(a, axis=0, keepdims=True)
    rank_ref[...] = jnp.concatenate([col_sum(oh1_t * base), col_sum(oh2_t * base)], axis=1).astype(jnp.int32)
    exp_ref[...] = jnp.concatenate([col_sum(oh1_t * expert_id), col_sum(oh2_t * expert_id)], axis=1).astype(jnp.int32)
    carry[...] += jnp.broadcast_to(jnp.sum(oh_t, axis=1, keepdims=True), carry.shape)
    cnt_ref[...] = carry[...]


def attn_out_router(x, o_prompt, o_sample, w_o, g, w_router, tm=ROW_TILE):
    rows, k = x.shape
    n_t = rows // tm
    wr = jnp.pad(w_router, ((0, 0), (0, LANES - N_EXPERTS)))
    o_specs, n_p = _two_part_specs(o_prompt, o_sample, tm)
    return pl.pallas_call(
        functools.partial(_router_kernel, tm, n_p),
        grid=(n_t,),
        in_specs=[pl.BlockSpec((tm, k), lambda i: (i, 0))] + o_specs + [
            pl.BlockSpec((k, k), lambda i: (0, 0)),
            pl.BlockSpec((1, k), lambda i: (0, 0)),
            pl.BlockSpec((k, LANES), lambda i: (0, 0)),
        ],
        out_specs=[
            pl.BlockSpec((tm, k), lambda i: (i, 0)),
            pl.BlockSpec((tm, LANES), lambda i: (i, 0)),
            pl.BlockSpec((None, 1, TOP_K * tm), lambda i: (i, 0, 0)),
            pl.BlockSpec((None, 1, TOP_K * tm), lambda i: (i, 0, 0)),
            pl.BlockSpec((LANES, LANES), lambda i: (0, 0)),
        ],
        out_shape=[
            jax.ShapeDtypeStruct((rows, k), F32),
            jax.ShapeDtypeStruct((rows, LANES), F32),
            jax.ShapeDtypeStruct((n_t, 1, TOP_K * tm), jnp.int32),
            jax.ShapeDtypeStruct((n_t, 1, TOP_K * tm), jnp.int32),
            jax.ShapeDtypeStruct((LANES, LANES), F32),
        ],
        scratch_shapes=[pltpu.VMEM((LANES, LANES), F32)],
        compiler_params=_cparams("arbitrary"),
        name="attn_out_router",
    )(x, o_prompt, o_sample, w_o, g.reshape(1, k), wr)


def _as_tiles(a):
    return a.reshape(a.shape[0], SUBLANES, LANES)


def _as_rows(a):
    return a.reshape(a.shape[0], SUBLANES * LANES)


def _row_copy(src_ref, src_row, dst_ref, dst_row, sem):
    return pltpu.make_async_copy(src_ref.at[src_row], dst_ref.at[dst_row], sem)


def _dispatch_kernel(tm, pos_ref, x_ref, xs_in_ref, xs_ref, x_tiles, sem):
    del xs_in_ref
    x_tiles[...] = _as_tiles(x_ref[...])

    def start(r, c):
        for choice in range(TOP_K):
            _row_copy(x_tiles, r, xs_ref, pos_ref[0, choice * tm + r], sem).start(priority=choice)
        return c

    lax.fori_loop(0, tm, start, 0, unroll=8)
    for _ in range(TOP_K):
        pltpu.make_async_copy(x_tiles, xs_ref.at[pl.ds(0, tm)], sem).wait()


def dispatch(x, pos, xs_buf, tm=ROW_TILE):
    rows, k = x.shape
    return pl.pallas_call(
        functools.partial(_dispatch_kernel, tm),
        grid=(rows // tm,),
        in_specs=[
            pl.BlockSpec((None, 1, TOP_K * tm), lambda i: (i, 0, 0), memory_space=pltpu.SMEM),
            pl.BlockSpec((tm, k), lambda i: (i, 0)),
            pl.BlockSpec(memory_space=pl.ANY),
        ],
        out_specs=pl.BlockSpec(memory_space=pl.ANY),
        out_shape=jax.ShapeDtypeStruct(xs_buf.shape, F32),
        scratch_shapes=[pltpu.VMEM((tm, SUBLANES, LANES), F32), pltpu.SemaphoreType.DMA(())],
        input_output_aliases={2: 0},
        compiler_params=_cparams("arbitrary"),
        name="moe_dispatch",
    )(pos, x, xs_buf)


def _moe_kernel(texp_ref, tvalid_ref, x_ref, g_ref, w1_ref, w3_ref, w2_ref, o_ref, h_scr, acc_scr):
    del texp_ref
    j = pl.program_id(1)
    last = pl.num_programs(1) - 1
    valid = tvalid_ref[pl.program_id(0)]

    @pl.when(valid > 0)
    def _():
        @pl.when(j == 0)
        def _():
            h_scr[...] = _rms(_as_rows(x_ref[...]), g_ref[...]).astype(BF16)
            acc_scr[...] = jnp.zeros_like(acc_scr)

        w1 = w1_ref[...].astype(BF16)
        w3 = w3_ref[...].astype(BF16)
        w2 = w2_ref[...].astype(BF16)
        half = x_ref.shape[0] // 2
        for lo in (0, half):

            @pl.when(valid > lo)
            def _():
                h = h_scr[lo : lo + half, :]
                a = _dot(h, w1)
                b = _dot(h, w3)
                acc_scr[lo : lo + half, :] += _dot((a * _sigmoid(a) * b).astype(BF16), w2)

        @pl.when(j == last)
        def _():
            o_ref[...] = acc_scr[...]

    @pl.when(jnp.logical_and(valid == 0, j == last))
    def _():
        o_ref[...] = jnp.zeros_like(o_ref)


def moe_experts(xs, g, w1, w3, w2, layer, tile_expert, tile_valid, tm, tf):
    rows = xs.shape[0]
    k = SUBLANES * LANES
    f = w1.shape[-1]
    n_f = f // tf

    def jf(i, j, tvalid):
        return jnp.where(tvalid[i] > 0, j, n_f - 1)

    grid_spec = pltpu.PrefetchScalarGridSpec(
        num_scalar_prefetch=2,
        grid=(rows // tm, n_f),
        in_specs=[
            pl.BlockSpec((tm, SUBLANES, LANES), lambda i, j, texp, tvalid: (i, 0, 0)),
            pl.BlockSpec((1, k), lambda i, j, texp, tvalid: (0, 0)),
            pl.BlockSpec((None, None, k, tf), lambda i, j, texp, tvalid: (layer, texp[i], 0, jf(i, j, tvalid))),
            pl.BlockSpec((None, None, k, tf), lambda i, j, texp, tvalid: (layer, texp[i], 0, jf(i, j, tvalid))),
            pl.BlockSpec((None, None, tf, k), lambda i, j, texp, tvalid: (layer, texp[i], jf(i, j, tvalid), 0)),
        ],
        out_specs=pl.BlockSpec((tm, k), lambda i, j, texp, tvalid: (i, 0)),
        scratch_shapes=[pltpu.VMEM((tm, k), BF16), pltpu.VMEM((tm, k), F32)],
    )
    return pl.pallas_call(
        _moe_kernel,
        grid_spec=grid_spec,
        out_shape=jax.ShapeDtypeStruct((rows, k), F32),
        compiler_params=_cparams("arbitrary", "arbitrary"),
        name="moe_experts",
    )(tile_expert, tile_valid, xs, g.reshape(1, k), w1, w3, w2)


def _combine_kernel(tm, pos_ref, x_ref, gate_ref, ys_ref, o_ref, y1_scr, y2_scr, sem):
    bufs = (y1_scr, y2_scr)

    def start(r, c):
        for choice in range(TOP_K):
            src = ys_ref.at[pl.ds(pos_ref[0, choice * tm + r], 1)]
            pltpu.make_async_copy(src, bufs[choice].at[pl.ds(r, 1)], sem).start(priority=choice)
        return c

    lax.fori_loop(0, tm, start, 0, unroll=8)
    for choice in range(TOP_K):
        pltpu.make_async_copy(ys_ref.at[pl.ds(0, tm)], bufs[choice], sem).wait()
    g = gate_ref[...]
    o_ref[...] = x_ref[...] + (g[:, 0:1] * y1_scr[...] + g[:, 1:2] * y2_scr[...])


def combine(x, gate, pos, ys, tm=ROW_TILE):
    rows, k = x.shape
    return pl.pallas_call(
        functools.partial(_combine_kernel, tm),
        grid=(rows // tm,),
        in_specs=[
            pl.BlockSpec((None, 1, TOP_K * tm), lambda i: (i, 0, 0), memory_space=pltpu.SMEM),
            pl.BlockSpec((tm, k), lambda i: (i, 0)),
            pl.BlockSpec((tm, LANES), lambda i: (i, 0)),
            pl.BlockSpec(memory_space=pl.ANY),
        ],
        out_specs=pl.BlockSpec((tm, k), lambda i: (i, 0)),
        out_shape=jax.ShapeDtypeStruct((rows, k), F32),
        scratch_shapes=[pltpu.VMEM((tm, k), F32), pltpu.VMEM((tm, k), F32), pltpu.SemaphoreType.DMA(())],
        compiler_params=_cparams("arbitrary"),
        name="moe_combine",
    )(pos, x, gate, ys)


def attn_out_moe(x, o_prompt, o_sample, w_o, g, w_router, w1, w3, w2, layer, xs_buf=None,
                 tm_e=MOE_TILE, tf=MOE_FF_TILE):
    rows = x.shape[0]
    n_tiles = -(-(TOP_K * rows + N_EXPERTS * (tm_e - 1)) // tm_e)
    x, gate, rank, expert, cnt = attn_out_router(x, o_prompt, o_sample, w_o, g, w_router)
    counts = cnt[:N_EXPERTS, 0].astype(jnp.int32)
    padded = ((counts + tm_e - 1) // tm_e) * tm_e
    ends = jnp.cumsum(padded)
    offs = ends - padded
    pos = rank
    for e in range(N_EXPERTS):
        pos = pos + jnp.where(expert == e, offs[e], 0)
    tile_start = jnp.arange(n_tiles, dtype=jnp.int32) * tm_e
    tile_expert = jnp.minimum(jnp.sum(ends[None, :] <= tile_start[:, None], axis=1), N_EXPERTS - 1).astype(jnp.int32)
    tile_valid = jnp.clip(counts[tile_expert] - (tile_start - offs[tile_expert]), 0, tm_e).astype(jnp.int32)
    if xs_buf is None:
        xs_buf = jnp.zeros((n_tiles * tm_e, SUBLANES, LANES), F32)
    xs = dispatch(x, pos, xs_buf)
    ys = moe_experts(xs, g, w1, w3, w2, layer, tile_expert, tile_valid, tm_e, tf)
    return combine(x, gate, pos, ys), xs


def _conv_rows(xpad, cw, cb):
    ext = xpad[...]
    y = None
    for k in range(CONV_W):
        back = CONV_W - 1 - k
        src = ext if back == 0 else pltpu.roll(ext, back, axis=0)
        term = src[HALO:] * cw[k : k + 1]
        y = term if y is None else y + term
    return y + cb


def _conv_taps(xpad, base, rows, step, cw, cb):
    y = xpad[base - 3 * step : base - 3 * step + rows, :] * cw[0:1]
    for k in range(1, CONV_W):
        lo = base - (CONV_W - 1 - k) * step
        y = y + xpad[lo : lo + rows, :] * cw[k : k + 1]
    return y + cb


def _lru_gates(xc, wa_ref, ba_ref, wx_ref, bx_ref, lam_ref):
    xcb = xc.astype(BF16)
    r_parts, i_parts = [], []
    for k in range(LRU_BLOCKS):
        blk = xcb[:, k * LRU_BW : (k + 1) * LRU_BW]
        r_parts.append(_dot(blk, wa_ref[k]))
        i_parts.append(_dot(blk, wx_ref[k]))
    r = _sigmoid(jnp.concatenate(r_parts, axis=1) + ba_ref[...])
    ig = _sigmoid(jnp.concatenate(i_parts, axis=1) + bx_ref[...])
    log_a = (-LRU_C * r) * _softplus(-lam_ref[...])
    a = jnp.exp(log_a)
    one_minus_a2 = -jnp.tanh(log_a) * (a * a + 1.0)
    u = jnp.sqrt(one_minus_a2) * (ig * xc)
    return a, u


def _lru_in_proj(x_ref, g_ref, win_ref, gate_scr, xpad, base):
    h = _rms(x_ref[...], g_ref[...]).astype(BF16)
    gate_scr[...] = _dot(h, win_ref[:, :D_RNN])
    xpad[base : base + x_ref.shape[0], :] = _dot(h, win_ref[:, D_RNN:])


def _lru_prompt_kernel(tt, x_ref, g_ref, win_ref, cw_ref, cb_ref, wa_ref, ba_ref, wx_ref, bx_ref, lam_ref,
                       y_ref, hlast_ref, convn_ref, xpad, gate_scr, a_scr, u_scr, hs_scr, h_scr):
    t = pl.program_id(1)

    @pl.when(t == 0)
    def _():
        xpad[0:HALO, :] = jnp.zeros((HALO, D_RNN), F32)
        h_scr[...] = jnp.zeros_like(h_scr)

    @pl.when(t > 0)
    def _():
        xpad[0:HALO, :] = xpad[tt : tt + HALO, :]

    _lru_in_proj(x_ref, g_ref, win_ref, gate_scr, xpad, HALO)
    xc = _conv_rows(xpad, cw_ref[...], cb_ref[...])
    a, u = _lru_gates(xc, wa_ref, ba_ref, wx_ref, bx_ref, lam_ref)
    a_scr[...] = a
    u_scr[...] = u

    def body(i, h):
        h = a_scr[pl.ds(i, 1), :] * h + u_scr[pl.ds(i, 1), :]
        hs_scr[pl.ds(i, 1), :] = h
        return h

    h_scr[...] = lax.fori_loop(0, tt, body, h_scr[...], unroll=8)
    y_ref[...] = hs_scr[...] * jax.nn.gelu(gate_scr[...])

    @pl.when(t == pl.num_programs(1) - 1)
    def _():
        hlast_ref[...] = h_scr[...]
        convn_ref[...] = xpad[tt : tt + HALO, :]


def lru_prompt(x, g, w_in, p, tt=512):
    n_t = SEQ // tt
    wspec = lambda shape: pl.BlockSpec(shape, lambda b, t: (0,) * len(shape))
    return pl.pallas_call(
        functools.partial(_lru_prompt_kernel, tt),
        grid=(BATCH, n_t),
        in_specs=[
            pl.BlockSpec((tt, D_MODEL), lambda b, t: (b * n_t + t, 0)),
            wspec((1, D_MODEL)), wspec((D_MODEL, 2 * D_RNN)),
            wspec((CONV_W, D_RNN)), wspec((1, D_RNN)),
            wspec((LRU_BLOCKS, LRU_BW, LRU_BW)), wspec((1, D_RNN)),
            wspec((LRU_BLOCKS, LRU_BW, LRU_BW)), wspec((1, D_RNN)),
            wspec((1, D_RNN)),
        ],
        out_specs=[
            pl.BlockSpec((tt, D_RNN), lambda b, t: (b * n_t + t, 0)),
            pl.BlockSpec((None, 1, D_RNN), lambda b, t: (b, 0, 0)),
            pl.BlockSpec((None, HALO, D_RNN), lambda b, t: (b, 0, 0)),
        ],
        out_shape=[
            jax.ShapeDtypeStruct((N_PROMPT, D_RNN), F32),
            jax.ShapeDtypeStruct((BATCH, 1, D_RNN), F32),
            jax.ShapeDtypeStruct((BATCH, HALO, D_RNN), F32),
        ],
        scratch_shapes=[
            pltpu.VMEM((HALO + tt, D_RNN), F32),
            pltpu.VMEM((tt, D_RNN), F32),
            pltpu.VMEM((tt, D_RNN), F32),
            pltpu.VMEM((tt, D_RNN), F32),
            pltpu.VMEM((tt, D_RNN), F32),
            pltpu.VMEM((1, D_RNN), F32),
        ],
        compiler_params=_cparams("arbitrary", "arbitrary"),
        name="lru_prompt",
    )(x, g.reshape(1, D_MODEL), w_in, p["conv_w"], p["conv_b"], p["w_a"], p["b_a"], p["w_x"], p["b_x"], p["lam"])


def _lru_sample_kernel(x_ref, g_ref, win_ref, convp_ref, hprev_ref, cw_ref, cb_ref, wa_ref, ba_ref, wx_ref, bx_ref,
                       lam_ref, y_ref, hlast_ref, convn_ref, xpad, gate_scr):
    hist = (CONV_W - 1) * DEC_BATCH
    xpad[0:hist, :] = convp_ref[...]
    _lru_in_proj(x_ref, g_ref, win_ref, gate_scr, xpad, hist)
    xc = _conv_taps(xpad, hist, N_SAMPLE, DEC_BATCH, cw_ref[...], cb_ref[...])
    a, u = _lru_gates(xc, wa_ref, ba_ref, wx_ref, bx_ref, lam_ref)
    gate = jax.nn.gelu(gate_scr[...])
    h = hprev_ref[...]
    for t in range(DEC_SEQ):
        rows = slice(t * DEC_BATCH, (t + 1) * DEC_BATCH)
        h = a[rows] * h + u[rows]
        y_ref[rows, :] = h * gate[rows]
    hlast_ref[...] = h
    convn_ref[...] = xpad[N_SAMPLE:, :]


def lru_sample(x, g, w_in, conv_prev, h_prev, p):
    hist = (CONV_W - 1) * DEC_BATCH
    full = lambda shape: pl.BlockSpec(shape, lambda i: (0,) * len(shape))
    return pl.pallas_call(
        _lru_sample_kernel,
        grid=(1,),
        in_specs=[
            pl.BlockSpec((N_SAMPLE, D_MODEL), lambda i: (SAMPLE_BLOCK, 0)),
            full((1, D_MODEL)), full((D_MODEL, 2 * D_RNN)),
            full((hist, D_RNN)), full((DEC_BATCH, D_RNN)),
            full((CONV_W, D_RNN)), full((1, D_RNN)),
            full((LRU_BLOCKS, LRU_BW, LRU_BW)), full((1, D_RNN)),
            full((LRU_BLOCKS, LRU_BW, LRU_BW)), full((1, D_RNN)),
            full((1, D_RNN)),
        ],
        out_specs=[
            full((N_SAMPLE, D_RNN)),
            full((DEC_BATCH, D_RNN)),
            full((hist, D_RNN)),
        ],
        out_shape=[
            jax.ShapeDtypeStruct((N_SAMPLE, D_RNN), F32),
            jax.ShapeDtypeStruct((DEC_BATCH, D_RNN), F32),
            jax.ShapeDtypeStruct((hist, D_RNN), F32),
        ],
        scratch_shapes=[pltpu.VMEM((hist + N_SAMPLE, D_RNN), F32), pltpu.VMEM((N_SAMPLE, D_RNN), F32)],
        compiler_params=_cparams("arbitrary"),
        name="lru_sample",
    )(x, g.reshape(1, D_MODEL), w_in, conv_prev, h_prev,
      p["conv_w"], p["conv_b"], p["w_a"], p["b_a"], p["w_x"], p["b_x"], p["lam"])


def _attend(q, k, v):
    outs = []
    for h in range(MEM_HEADS):
        hs = slice(h * MEM_HD, (h + 1) * MEM_HD)
        s = lax.dot_general(q[:, hs], k[:, hs], (((1,), (1,)), ((), ())), preferred_element_type=F32)
        s = s * (MEM_HD ** -0.5)
        e = jnp.exp(s - jnp.max(s, axis=-1, keepdims=True))
        p = e / jnp.sum(e, axis=-1, keepdims=True)
        outs.append(_dot(p.astype(BF16), v[:, hs]))
    return jnp.concatenate(outs, axis=1)


def _attend_sample(bb, q_ref, k_ref, v_ref, o_ref):
    rows = MEM_HEADS * DEC_SEQ
    cols = N_MEM * MEM_HEADS
    row_head = lax.broadcasted_iota(jnp.int32, (rows, cols), 0) // DEC_SEQ
    col_head = lax.broadcasted_iota(jnp.int32, (rows, cols), 1) % MEM_HEADS
    same_head = row_head == col_head
    for i in range(bb):
        k2 = k_ref[i].reshape(cols, MEM_HD).astype(BF16)
        v2 = v_ref[i].reshape(cols, MEM_HD).astype(BF16)
        q = q_ref[i]
        qh = jnp.concatenate([q[:, h * MEM_HD : (h + 1) * MEM_HD] for h in range(MEM_HEADS)], axis=0)
        s = lax.dot_general(qh, k2, (((1,), (1,)), ((), ())), preferred_element_type=F32) * (MEM_HD ** -0.5)
        s = jnp.where(same_head, s, -jnp.inf)
        e = jnp.exp(s - jnp.max(s, axis=-1, keepdims=True))
        p = e / jnp.sum(e, axis=-1, keepdims=True)
        oh = _dot(p.astype(BF16), v2)
        o_ref[i] = jnp.concatenate([oh[h * DEC_SEQ : (h + 1) * DEC_SEQ] for h in range(MEM_HEADS)], axis=1)


def _attention_kernel(bb, qp_ref, kp_ref, vp_ref, qs_ref, ks_ref, vs_ref, op_ref, os_ref):
    op_ref[...] = _attend(qp_ref[...], kp_ref[...].astype(BF16), vp_ref[...].astype(BF16)).astype(op_ref.dtype)
    _attend_sample(bb, qs_ref, ks_ref, vs_ref, os_ref)


def attention(q, q_sample, k, v, cache_k, cache_v, layer):
    tt = ROW_TILE
    n_t = SEQ // tt
    steps = BATCH * n_t
    bb = DEC_BATCH // steps
    cache = lambda: pl.BlockSpec((None, bb, N_MEM, MEM_HEADS, MEM_HD), lambda i: (layer, i, 0, 0, 0))
    mem = lambda: pl.BlockSpec((None, N_MEM, D_MODEL), lambda i: (layer, i // n_t, 0))
    return pl.pallas_call(
        functools.partial(_attention_kernel, bb),
        grid=(steps,),
        in_specs=[
            pl.BlockSpec((tt, D_MODEL), lambda i: (i, 0)), mem(), mem(),
            pl.BlockSpec((bb, DEC_SEQ, D_MODEL), lambda i: (i, 0, 0)), cache(), cache(),
        ],
        out_specs=[
            pl.BlockSpec((tt, D_MODEL), lambda i: (i, 0)),
            pl.BlockSpec((bb, DEC_SEQ, D_MODEL), lambda i: (i, 0, 0)),
        ],
        out_shape=[
            jax.ShapeDtypeStruct((N_PROMPT, D_MODEL), BF16),
            jax.ShapeDtypeStruct((DEC_BATCH, DEC_SEQ, D_MODEL), F32),
        ],
        compiler_params=_cparams("arbitrary"),
        name="attention",
    )(q, k, v, q_sample, cache_k, cache_v)


def _ssd_gate_norm(y, z, ng):
    y = y * _silu(z)
    outs = []
    for g in range(SSD_GROUPS):
        yg = y[:, g * SSD_GROUP_W : (g + 1) * SSD_GROUP_W]
        outs.append(yg * lax.rsqrt(jnp.mean(yg * yg, axis=-1, keepdims=True) + EPS))
    return jnp.concatenate(outs, axis=1) * ng


def _ssd_kernel(q, slot, n_sample_in, z_ref, xbc_ref, dt_ref, cw_ref, cb_ref, dtb_ref, alog_ref, dskip_ref, ng_ref,
                e_ref, *rest):
    sample_in = rest[:5]
    y_ref, st_ref, convn_ref, ys_ref, sts_ref, xpad, s_t, y_scr = rest[n_sample_in:]
    _ssd_recur_body(1, slot, *sample_in, ys_ref, sts_ref)
    t = pl.program_id(1)

    @pl.when(t == 0)
    def _():
        xpad[0:HALO, :] = jnp.zeros((HALO, SSD_CONV_DIM), F32)
        s_t[...] = jnp.zeros_like(s_t)

    @pl.when(t > 0)
    def _():
        xpad[0:HALO, :] = xpad[q : q + HALO, :]

    xpad[HALO : HALO + q, :] = xbc_ref[...]
    xbc = _silu(_conv_rows(xpad, cw_ref[...], cb_ref[...]))
    xs = xbc[:, :D_INNER]

    dt = _softplus(dt_ref[...] + dtb_ref[...])
    adt = -jnp.exp(alog_ref[...]) * dt
    row_i = lax.broadcasted_iota(jnp.int32, (q, q), 0)
    col_i = lax.broadcasted_iota(jnp.int32, (q, q), 1)
    tril = row_i >= col_i
    a_cs = _dot_f32_rhs(jnp.where(tril, 1.0, 0.0).astype(BF16), adt)
    a_cs_t = a_cs.T
    a_end = a_cs[q - 1 : q, :]
    expand = e_ref[...]
    ecs_x = _expand_heads(jnp.exp(a_cs), expand)
    xb = (xs * _expand_heads(dt, expand)).astype(BF16)
    xd = (xs * _expand_heads(dt * jnp.exp(a_end - a_cs), expand)).astype(BF16)

    for g in range(SSD_GROUPS):
        gc = slice(g * SSD_GROUP_W, (g + 1) * SSD_GROUP_W)
        bg = xbc[:, D_INNER + g * SSD_STATE : D_INNER + (g + 1) * SSD_STATE]
        cg = xbc[:, D_INNER + SSD_GN + g * SSD_STATE : D_INNER + SSD_GN + (g + 1) * SSD_STATE].astype(BF16)
        cb_mat = lax.dot_general(cg, bg.astype(BF16), (((1,), (1,)), ((), ())), preferred_element_type=F32)
        sg = s_t[:, gc]
        y_scr[:, gc] = _dot(cg, sg.astype(BF16)) * ecs_x[:, gc]
        s_t[:, gc] = ecs_x[q - 1 : q, gc] * sg + _dot(bg.T.astype(BF16), xd[:, gc])
        heads_per_block = LANES // SSD_HEADDIM
        lane = lax.broadcasted_iota(jnp.int32, (q, LANES), 1)
        for e in range(0, SSD_HEADS // SSD_GROUPS, heads_per_block):
            h0 = g * (SSD_HEADS // SSD_GROUPS) + e
            pc = slice(h0 * SSD_HEADDIM, h0 * SSD_HEADDIM + LANES)
            masks = []
            for h in range(h0, h0 + heads_per_block):
                seg = a_cs[:, h : h + 1] - a_cs_t[h : h + 1, :]
                decay = jnp.where(tril, jnp.exp(jnp.minimum(seg, 0.0)), 0.0)
                masks.append((cb_mat * decay).astype(BF16))
            x_pair = xb[:, pc]
            zero = jnp.zeros_like(x_pair)
            x_diag = jnp.concatenate(
                [jnp.where(lane // SSD_HEADDIM == k, x_pair, zero) for k in range(heads_per_block)], axis=0)
            y_scr[:, pc] += _dot(jnp.concatenate(masks, axis=1), x_diag)

    y = y_scr[...] + dskip_ref[...] * xs
    y_ref[...] = _ssd_gate_norm(y, z_ref[...], ng_ref[...]).astype(y_ref.dtype)

    @pl.when(t == pl.num_programs(1) - 1)
    def _():
        for j in range(D_INNER // LANES):
            st_ref[j * LANES : (j + 1) * LANES, :] = s_t[:, j * LANES : (j + 1) * LANES].T
        convn_ref[...] = xpad[q : q + HALO, :]


def ssd(z, xbc, dt, p, xdt_s, dec_s, bm_s, cm_s, state, layer, new_states=None, q=SSD_CHUNK):
    n_t = SEQ // q
    assert BATCH * n_t == DEC_BATCH
    n_layers = state.shape[0]
    rows = lambda w: pl.BlockSpec((q, w), lambda b, t: (b * n_t + t, 0))
    wspec = lambda shape: pl.BlockSpec(shape, lambda b, t: (0,) * len(shape))
    seq = lambda w: pl.BlockSpec((1, DEC_SEQ, w), lambda b, t: (b * n_t + t, 0, 0))
    in_specs = [
        rows(D_INNER), rows(SSD_CONV_DIM), rows(LANES),
        wspec((CONV_W, SSD_CONV_DIM)), wspec((1, SSD_CONV_DIM)),
        wspec((1, LANES)), wspec((1, LANES)), wspec((1, D_INNER)), wspec((1, D_INNER)),
        wspec((LANES, D_INNER)),
        seq(D_INNER), seq(D_INNER), seq(SSD_GN), seq(SSD_GN),
        pl.BlockSpec((None, 1, D_INNER, SSD_STATE), lambda b, t: (layer, b * n_t + t, 0, 0)),
    ]
    args = [z, xbc, dt, p["conv_w"], p["conv_b"], p["dt_bias"], p["a_log"], p["d_skip"], p["norm_g"], p["expand"],
            xdt_s, dec_s, bm_s, cm_s, state]
    if new_states is None:
        slot = layer
        st_out = pl.BlockSpec((n_layers, 1, D_INNER, SSD_STATE), lambda b, t: (0, b * n_t + t, 0, 0))
        aliases = {}
    else:
        slot = 0
        st_out = pl.BlockSpec((1, 1, D_INNER, SSD_STATE), lambda b, t: (layer, b * n_t + t, 0, 0))
        in_specs.append(pl.BlockSpec(memory_space=pl.ANY))
        args.append(new_states)
        aliases = {len(args) - 1: 4}
    return pl.pallas_call(
        functools.partial(_ssd_kernel, q, slot, len(args) - 10),
        grid=(BATCH, n_t),
        in_specs=in_specs,
        out_specs=[
            rows(D_INNER),
            pl.BlockSpec((None, D_INNER, SSD_STATE), lambda b, t: (b, 0, 0)),
            pl.BlockSpec((None, HALO, SSD_CONV_DIM), lambda b, t: (b, 0, 0)),
            seq(D_INNER),
            st_out,
        ],
        out_shape=[
            jax.ShapeDtypeStruct((N_PROMPT, D_INNER), BF16),
            jax.ShapeDtypeStruct((BATCH, D_INNER, SSD_STATE), F32),
            jax.ShapeDtypeStruct((BATCH, HALO, SSD_CONV_DIM), F32),
            jax.ShapeDtypeStruct((DEC_BATCH, DEC_SEQ, D_INNER), F32),
            jax.ShapeDtypeStruct((n_layers, DEC_BATCH, D_INNER, SSD_STATE), F32),
        ],
        scratch_shapes=[
            pltpu.VMEM((HALO + q, SSD_CONV_DIM), F32),
            pltpu.VMEM((SSD_STATE, D_INNER), F32),
            pltpu.VMEM((q, D_INNER), F32),
        ],
        input_output_aliases=aliases,
        compiler_params=_cparams("arbitrary", "arbitrary"),
        name="ssd",
    )(*args)


def _ssd_sample_pre_kernel(xbc_ref, dt_ref, convp_ref, cw_ref, cb_ref, dtb_ref, alog_ref, e_ref,
                           xs_ref, xdt_ref, dec_ref, bm_ref, cm_ref, convn_ref, xpad):
    hist = (CONV_W - 1) * DEC_BATCH
    xpad[0:hist, :] = convp_ref[...]
    xpad[hist:, :] = xbc_ref[...]
    xc = _conv_taps(xpad, hist, N_SAMPLE, DEC_BATCH, cw_ref[...], cb_ref[...])
    xbc = xc * _sigmoid(xc)
    xs = xbc[:, :D_INNER]
    dt = _softplus(dt_ref[...] + dtb_ref[...])
    adt = -jnp.exp(alog_ref[...]) * dt
    expand = e_ref[...]
    xs_ref[...] = xs
    xdt_ref[...] = xs * _dot_f32_lhs(dt, expand)
    dec_ref[...] = _dot_f32_lhs(jnp.exp(adt), expand)
    bm_ref[...] = xbc[:, D_INNER : D_INNER + SSD_GN]
    cm_ref[...] = xbc[:, D_INNER + SSD_GN :]
    convn_ref[...] = xpad[N_SAMPLE:, :]


def ssd_sample_pre(xbc, dt, conv_prev, p):
    hist = (CONV_W - 1) * DEC_BATCH
    full = lambda shape: pl.BlockSpec(shape, lambda i: (0,) * len(shape))
    out_w = [D_INNER, D_INNER, D_INNER, SSD_GN, SSD_GN]
    return pl.pallas_call(
        _ssd_sample_pre_kernel,
        grid=(1,),
        in_specs=[
            pl.BlockSpec((N_SAMPLE, SSD_CONV_DIM), lambda i: (SAMPLE_BLOCK, 0)),
            pl.BlockSpec((N_SAMPLE, LANES), lambda i: (SAMPLE_BLOCK, 0)),
            full((hist, SSD_CONV_DIM)),
            full((CONV_W, SSD_CONV_DIM)), full((1, SSD_CONV_DIM)),
            full((1, LANES)), full((1, LANES)), full((LANES, D_INNER)),
        ],
        out_specs=[full((N_SAMPLE, w)) for w in out_w] + [full((hist, SSD_CONV_DIM))],
        out_shape=[jax.ShapeDtypeStruct((N_SAMPLE, w), F32) for w in out_w]
        + [jax.ShapeDtypeStruct((hist, SSD_CONV_DIM), F32)],
        scratch_shapes=[pltpu.VMEM((hist + N_SAMPLE, SSD_CONV_DIM), F32)],
        compiler_params=_cparams("arbitrary"),
        name="ssd_sample_pre",
    )(xbc, dt, conv_prev, p["conv_w"], p["conv_b"], p["dt_bias"], p["a_log"], p["expand"])


def _ssd_recur_body(bb, slot, xdt_ref, dec_ref, bm_ref, cm_ref, st_in_ref, y_ref, st_out_ref):
    nt = (((1,), (1,)), ((), ()))
    tn = (((0,), (0,)), ((), ()))
    last = DEC_SEQ - 1
    ones = jnp.ones((SUBLANES, SSD_STATE), BF16)
    for i in range(bb):
        xdt = xdt_ref[i]
        dec = dec_ref[i]
        decay = [dec[0:1]]
        for t in range(1, DEC_SEQ):
            decay.append(decay[t - 1] * dec[t : t + 1])
        prop = {(s, s): xdt[s : s + 1] for s in range(DEC_SEQ)}
        for t in range(1, DEC_SEQ):
            for s in range(t):
                prop[(t, s)] = dec[t : t + 1] * prop[(t - 1, s)]
        d3 = jnp.concatenate(list(_split3(decay[last])) + [jnp.zeros((SUBLANES - 3, D_INNER), BF16)], axis=0)
        p_last = jnp.concatenate([prop[(last, s)] for s in range(DEC_SEQ)], axis=0).astype(BF16)
        for g in range(SSD_GROUPS):
            gc = slice(g * SSD_GROUP_W, (g + 1) * SSD_GROUP_W)
            bg = bm_ref[i, :, g * SSD_STATE : (g + 1) * SSD_STATE].astype(BF16)
            cg = cm_ref[i, :, g * SSD_STATE : (g + 1) * SSD_STATE].astype(BF16)
            sg = st_in_ref[i, gc, :]
            c_h = lax.dot_general(cg, sg.astype(BF16), nt, preferred_element_type=F32)
            c_b = lax.dot_general(cg, bg, nt, preferred_element_type=F32)
            for t in range(DEC_SEQ):
                y = decay[t][:, gc] * c_h[t : t + 1]
                for s in range(t + 1):
                    y = y + jnp.broadcast_to(c_b[t : t + 1, s : s + 1], (1, SSD_GROUP_W)) * prop[(t, s)][:, gc]
                y_ref[i, t : t + 1, gc] = y
            d_col = lax.dot_general(d3[:, gc], ones, tn, preferred_element_type=F32)
            st_out_ref[slot, i, gc, :] = d_col * sg + lax.dot_general(p_last[:, gc], bg, tn, preferred_element_type=F32)
    for other in range(st_out_ref.shape[0]):
        if other != slot:
            st_out_ref[other] = jnp.zeros(st_out_ref.shape[1:], F32)


def _ssd_sample_post_kernel(yr_ref, xs_ref, z_ref, dskip_ref, ng_ref, y_ref):
    y = yr_ref[...] + dskip_ref[...] * xs_ref[...]
    y_ref[...] = _ssd_gate_norm(y, z_ref[...], ng_ref[...]).astype(y_ref.dtype)


def ssd_sample_post(y_raw, xs, z, p):
    full = lambda shape: pl.BlockSpec(shape, lambda i: (0,) * len(shape))
    return pl.pallas_call(
        _ssd_sample_post_kernel,
        grid=(1,),
        in_specs=[
            full((N_SAMPLE, D_INNER)), full((N_SAMPLE, D_INNER)),
            pl.BlockSpec((N_SAMPLE, D_INNER), lambda i: (SAMPLE_BLOCK, 0)),
            full((1, D_INNER)), full((1, D_INNER)),
        ],
        out_specs=full((N_SAMPLE, D_INNER)),
        out_shape=jax.ShapeDtypeStruct((N_SAMPLE, D_INNER), BF16),
        compiler_params=_cparams("arbitrary"),
        name="ssd_sample_post",
    )(y_raw, xs, z, p["d_skip"], p["norm_g"])


def _to_time_major(a):
    return jnp.swapaxes(a, 0, 1).reshape(a.shape[0] * a.shape[1], a.shape[2])


def _to_batch_major(a, t):
    return jnp.swapaxes(a.reshape(t, DEC_BATCH, a.shape[1]), 0, 1)


def _row(v):
    return v.reshape(1, -1).astype(F32)


def _pad_lanes(v):
    return jnp.pad(v.reshape(1, -1).astype(F32), ((0, 0), (0, LANES - v.shape[-1])))


def kernel(x_prompt, x_sample, state_lru_h, state_lru_conv, state_ssd, state_ssd_conv, cache_mem_k, cache_mem_v, mem_prompt, norm_mix, norm_mem, norm_memkv, norm_ffn, norm_final, lru_w_in, lru_conv_w, lru_conv_b, lru_w_a, lru_b_a, lru_w_x, lru_b_x, lru_lam, lru_w_out, ssd_w_in, ssd_conv_w, ssd_conv_b, ssd_dt_bias, ssd_a_log, ssd_d, ssd_norm_g, ssd_w_out, mem_w_q, mem_w_k, mem_w_v, mem_w_o, ffn_w1, ffn_w3, ffn_w2, moe_router, moe_w1, moe_w3, moe_w2):
    bf = lambda w: w.astype(BF16)
    x = concat_rows(x_prompt.reshape(N_PROMPT, D_MODEL), _to_time_major(x_sample))
    mem = mem_prompt.reshape(BATCH * N_MEM, D_MODEL)
    head_of_col = jnp.arange(D_INNER, dtype=jnp.int32) // SSD_HEADDIM
    expand = (jnp.arange(LANES, dtype=jnp.int32)[:, None] == head_of_col[None, :]).astype(BF16)

    p_lru_h, p_lru_conv, p_ssd, p_ssd_conv = [], [], [], []
    s_lru_h, s_lru_conv, s_ssd_conv = [], [], []
    s_ssd = None
    xs_buf = None
    hist = CONV_W - 1
    mk, mv, p_mk, p_mv = mem_kv(mem, norm_memkv, mem_w_k, mem_w_v)
    for i in range(DEPTH):
        j = i // 2
        if i % 2 == 0:
            p = dict(conv_w=lru_conv_w[j], conv_b=_row(lru_conv_b[j]), w_a=bf(lru_w_a[j]), b_a=_row(lru_b_a[j]),
                     w_x=bf(lru_w_x[j]), b_x=_row(lru_b_x[j]), lam=_row(lru_lam[j]))
            w_in = bf(lru_w_in[j])
            y_p, h_p, c_p = lru_prompt(x, norm_mix[i], w_in, p)
            y_s, h_s, c_s = lru_sample(x, norm_mix[i], w_in, _to_time_major(state_lru_conv[j]), state_lru_h[j], p)
            p_lru_h.append(h_p.reshape(BATCH, D_RNN))
            p_lru_conv.append(c_p[:, HALO - hist :, :])
            s_lru_h.append(h_s)
            s_lru_conv.append(_to_batch_major(c_s, hist))
            w_out = bf(lru_w_out[j])
        else:
            w_in = ssd_w_in[j]
            w_z = bf(w_in[:, :D_INNER])
            w_xbc = bf(w_in[:, D_INNER : D_INNER + SSD_CONV_DIM])
            w_dt = bf(jnp.pad(w_in[:, D_INNER + SSD_CONV_DIM :], ((0, 0), (0, LANES - SSD_HEADS))))
            p = dict(conv_w=ssd_conv_w[j], conv_b=_row(ssd_conv_b[j]), dt_bias=_pad_lanes(ssd_dt_bias[j]),
                     a_log=_pad_lanes(ssd_a_log[j]), d_skip=_row(jnp.repeat(ssd_d[j], SSD_HEADDIM)),
                     norm_g=_row(ssd_norm_g[j]), expand=expand)
            z, xbc, dt = norm_matmul(x, norm_mix[i], [w_z, w_xbc, w_dt], [F32, F32, F32], tm=256)
            xs_s, xdt_s, dec_s, bm_s, cm_s, c_s = ssd_sample_pre(xbc, dt, _to_time_major(state_ssd_conv[j]), p)
            y_p, st_p, c_p, y_raw, s_ssd = ssd(
                z, xbc, dt, p,
                _to_batch_major(xdt_s, DEC_SEQ), _to_batch_major(dec_s, DEC_SEQ),
                _to_batch_major(bm_s, DEC_SEQ), _to_batch_major(cm_s, DEC_SEQ),
                state_ssd.reshape(-1, DEC_BATCH, D_INNER, SSD_STATE), j, s_ssd)
            y_s = ssd_sample_post(_to_time_major(y_raw), xs_s, z, p)
            p_ssd.append(st_p.reshape(BATCH, SSD_HEADS, SSD_HEADDIM, SSD_STATE))
            p_ssd_conv.append(c_p[:, HALO - hist :, :])
            s_ssd_conv.append(_to_batch_major(c_s, hist))
            w_out = bf(ssd_w_out[j])

        x, qp = out_proj_q(y_p, y_s, w_out, x, norm_mem[i], bf(mem_w_q[i]))
        o_p, o_s = attention(qp, _to_batch_major(qp[N_PROMPT:], DEC_SEQ), mk, mv, cache_mem_k, cache_mem_v, i)
        o_s = _to_time_major(o_s)
        if i % 2 == 0:
            x = attn_out_ffn(x, o_p, o_s, bf(mem_w_o[i]), norm_ffn[i],
                             bf(ffn_w1[j]), bf(ffn_w3[j]), bf(ffn_w2[j]), tf=D_FF // 2)
        else:
            x, xs_buf = attn_out_moe(x, o_p, o_s, bf(mem_w_o[i]), norm_ffn[i], moe_router[j],
                                     moe_w1, moe_w3, moe_w2, j, xs_buf)

    y_prompt = rmsnorm_rows(x, norm_final, 0, N_PROMPT // ROW_TILE).reshape(BATCH, SEQ, D_MODEL)
    y_sample = _to_batch_major(rmsnorm_rows(x, norm_final, N_PROMPT // ROW_TILE, N_SAMPLE // ROW_TILE), DEC_SEQ)
    return (y_prompt, y_sample,
            jnp.stack(p_lru_h), jnp.stack(p_lru_conv), jnp.stack(p_ssd), jnp.stack(p_ssd_conv),
            p_mk, p_mv,
            jnp.stack(s_lru_h), jnp.stack(s_lru_conv), s_ssd.reshape(state_ssd.shape), jnp.stack(s_ssd_conv))
```
